```python
import jax, jax.numpy as jnp
from jax import lax
import numpy as np

D_MODEL = 2048
BATCH = 32
SEQ = 256
DEPTH = 4
DEC_BATCH = 4
DEC_SEQ = 2048
PAST_LEN = 512

GRID_W = 64
N_MIXERS = 3
Q_BLOCK = 128
ROPE_THETA = 10000.0
EPS = 1e-6

MLA_HEADS = 16
MLA_Q_LORA = 512
MLA_KV_LORA = 512
MLA_NOPE = 128
MLA_ROPE = 64
MLA_V = 128
HG_HEADS = 16
HG_DK = D_MODEL // HG_HEADS
HG_DV = D_MODEL // HG_HEADS
HG_CHUNK = 32
GQA_HEADS = 16
GQA_KV_HEADS = 4
GQA_HD = 128
FF_DIM = 5632
N_EXPERTS = 8
TOP_K = 2
FF_EXPERT = 5632

N_A = (DEPTH + 2) // 3
N_B = (DEPTH + 1) // 3
N_C = DEPTH // 3
N_DENSE = (DEPTH + 1) // 2
N_MOE = DEPTH // 2

kernel_name = "hybrid_mla_hgrn2_gqa_prefix_dit_step"


def rmsnorm(x, w):
    xf = x.astype(jnp.float32)
    y = xf * lax.rsqrt(jnp.mean(xf * xf, axis=-1, keepdims=True) + EPS)
    return (y * w.astype(jnp.float32)).astype(x.dtype)


def grid_positions(n_tokens):
    rows = n_tokens // GRID_W
    row = jnp.repeat(jnp.arange(rows, dtype=jnp.int32), GRID_W)
    col = jnp.tile(jnp.arange(GRID_W, dtype=jnp.int32), rows)
    return row, col


def _rope_half(x, pos):
    d = x.shape[-1]
    n = d // 2
    inv = ROPE_THETA ** (-jnp.arange(n, dtype=jnp.float32) / n)
    ang = pos.astype(jnp.float32)[:, None] * inv[None, :]
    cos = jnp.concatenate([jnp.cos(ang), jnp.cos(ang)], -1)[:, None, :]
    sin = jnp.concatenate([jnp.sin(ang), jnp.sin(ang)], -1)[:, None, :]
    xf = x.astype(jnp.float32)
    rot = jnp.concatenate([-xf[..., n:], xf[..., :n]], -1)
    return (xf * cos + rot * sin).astype(x.dtype)


def axial_rope(x):
    r = x.shape[-1]
    row, col = grid_positions(x.shape[1])
    return jnp.concatenate([_rope_half(x[..., : r // 2], row), _rope_half(x[..., r // 2:], col)], -1)


def attend_blocked(q, k, v, scale):
    B, T, KVH, G, Dk = q.shape
    nblk = T // Q_BLOCK
    qb = jnp.moveaxis(q.reshape(B, nblk, Q_BLOCK, KVH, G, Dk), 1, 0)

    def one_block(qi):
        s = jnp.einsum('bqkgd,bskd->bkgqs', qi, k, preferred_element_type=jnp.float32) * scale
        p = jax.nn.softmax(s, axis=-1).astype(v.dtype)
        return jnp.einsum('bkgqs,bskd->bqkgd', p, v)

    o = lax.map(one_block, qb)
    return jnp.moveaxis(o, 0, 1).reshape(B, T, KVH * G * v.shape[-1])


def mla_project(h, wq_a, q_norm, wq_b, wkv_a, kv_norm):
    B, T, _ = h.shape
    q = (rmsnorm(h @ wq_a, q_norm) @ wq_b).reshape(B, T, MLA_HEADS, MLA_NOPE + MLA_ROPE)
    kv_a = h @ wkv_a
    c_kv = rmsnorm(kv_a[..., :MLA_KV_LORA], kv_norm)
    k_pe = kv_a[..., MLA_KV_LORA:]
    return q[..., :MLA_NOPE], q[..., MLA_NOPE:], c_kv, k_pe


def mla_attend(q_nope, q_pe, c_kv, k_pe, wkv_b, wo):
    B, S, _ = c_kv.shape
    kv = (c_kv @ wkv_b).reshape(B, S, MLA_HEADS, MLA_NOPE + MLA_V)
    k = jnp.concatenate([kv[..., :MLA_NOPE],
                         jnp.broadcast_to(k_pe[:, :, None, :], (B, S, MLA_HEADS, MLA_ROPE))], -1)
    v = kv[..., MLA_NOPE:]
    q = jnp.concatenate([q_nope, q_pe], -1)[:, :, :, None, :]
    return attend_blocked(q, k, v, (MLA_NOPE + MLA_ROPE) ** -0.5) @ wo


def mla_context(h, wq_a, q_norm, wq_b, wkv_a, kv_norm, wkv_b, wo):
    q_nope, q_pe, c_kv, k_pe = mla_project(h, wq_a, q_norm, wq_b, wkv_a, kv_norm)
    return mla_attend(q_nope, q_pe, c_kv, k_pe, wkv_b, wo), c_kv, k_pe


def mla_latent(h, wq_a, q_norm, wq_b, wkv_a, kv_norm, wkv_b, wo, ctx_ckv, ctx_kpe):
    q_nope, q_pe, c_kv, k_pe = mla_project(h, wq_a, q_norm, wq_b, wkv_a, kv_norm)
    q_pe = axial_rope(q_pe)
    k_pe = axial_rope(k_pe[:, :, None, :])[:, :, 0, :]
    c_all = jnp.concatenate([c_kv, ctx_ckv.astype(c_kv.dtype)], axis=1)
    kpe_all = jnp.concatenate([k_pe, ctx_kpe.astype(k_pe.dtype)], axis=1)
    return mla_attend(q_nope, q_pe, c_all, kpe_all, wkv_b, wo)


def gqa_project(h, w_qkv, q_norm, k_norm):
    B, T, _ = h.shape
    nq = GQA_HEADS * GQA_HD
    nk = GQA_KV_HEADS * GQA_HD
    qkv = h @ w_qkv
    q = rmsnorm(qkv[..., :nq].reshape(B, T, GQA_HEADS, GQA_HD), q_norm)
    k = rmsnorm(qkv[..., nq:nq + nk].reshape(B, T, GQA_KV_HEADS, GQA_HD), k_norm)
    v = qkv[..., nq + nk:].reshape(B, T, GQA_KV_HEADS, GQA_HD)
    return q, k, v


def gqa_attend(q, k, v, wo):
    B, T = q.shape[:2]
    q = q.reshape(B, T, GQA_KV_HEADS, GQA_HEADS // GQA_KV_HEADS, GQA_HD)
    return attend_blocked(q, k, v, GQA_HD ** -0.5) @ wo


def gqa_context(h, w_qkv, q_norm, k_norm, wo):
    q, k, v = gqa_project(h, w_qkv, q_norm, k_norm)
    return gqa_attend(q, k, v, wo), k, v


def gqa_latent(h, w_qkv, q_norm, k_norm, wo, ctx_k, ctx_v):
    q, k, v = gqa_project(h, w_qkv, q_norm, k_norm)
    q = axial_rope(q)
    k = axial_rope(k)
    k_all = jnp.concatenate([k, ctx_k.astype(k.dtype)], axis=1)
    v_all = jnp.concatenate([v, ctx_v.astype(v.dtype)], axis=1)
    return gqa_attend(q, k_all, v_all, wo)


def hgrn_lower_bounds(lb_logits, layer):
    p = jax.nn.softmax(lb_logits.astype(jnp.float32), axis=1)
    cum = jnp.cumsum(p, axis=1)
    return (cum - cum[:, :1])[:, layer]


def hgrn_chunk_scan(q, k, v, logf, S0):
    B, T, H, _ = q.shape
    DV = v.shape[-1]
    n = T // HG_CHUNK

    def to_chunks(a):
        return jnp.moveaxis(a.reshape(B, n, HG_CHUNK, H, a.shape[-1]), 1, 0).transpose(0, 1, 3, 2, 4)

    causal = jnp.tril(jnp.ones((HG_CHUNK, HG_CHUNK), dtype=bool))[:, :, None]

    def step(S, inp):
        qc, kc, vc, gc = inp
        G = jnp.cumsum(gc, axis=2)
        o_inter = jnp.einsum('bhtd,bhde->bhte', qc * jnp.exp(G), S)
        diff = G[:, :, :, None, :] - G[:, :, None, :, :]
        decay = jnp.exp(jnp.where(causal, diff, -jnp.inf))
        A = jnp.einsum('bhtd,bhtsd,bhsd->bhts', qc, decay, kc)
        o = o_inter + jnp.einsum('bhts,bhse->bhte', A, vc)
        G_last = G[:, :, -1:, :]
        S_new = jnp.exp(G_last[:, :, 0, :])[..., None] * S + jnp.einsum(
            'bhsd,bhse->bhde', kc * jnp.exp(G_last - G), vc)
        return S_new, o

    S_fin, o = lax.scan(step, S0, (to_chunks(q), to_chunks(k), to_chunks(v), to_chunks(logf)))
    o = jnp.transpose(o, (1, 0, 3, 2, 4)).reshape(B, T, H, DV)
    return o, S_fin


def hgrn_mixer(h, w_in, lb, o_norm, wo, S0_fwd, S0_bwd):
    B, T, _ = h.shape
    proj = (h @ w_in).astype(jnp.float32)
    q, v, g, z_fwd, z_bwd = jnp.split(proj, 5, axis=-1)

    def heads(a):
        return a.reshape(B, T, HG_HEADS, -1)

    def flip(a):
        return jnp.flip(a, axis=1)

    q, v, z_fwd, z_bwd = heads(q), heads(v), heads(z_fwd), heads(z_bwd)
    lb = lb.reshape(2, HG_HEADS, HG_DK)
    f_fwd = lb[0] + (1.0 - lb[0]) * jax.nn.sigmoid(z_fwd)
    f_bwd = lb[1] + (1.0 - lb[1]) * jax.nn.sigmoid(z_bwd)
    o_f, S_f = hgrn_chunk_scan(q, 1.0 - f_fwd, v, jnp.log(f_fwd), S0_fwd)
    o_b, S_b = hgrn_chunk_scan(flip(q), flip(1.0 - f_bwd), flip(v), flip(jnp.log(f_bwd)), S0_bwd)
    o = o_f + flip(o_b)
    o = rmsnorm(o, o_norm.reshape(HG_HEADS, HG_DV)) * jax.nn.silu(heads(g))
    out = o.reshape(B, T, D_MODEL).astype(h.dtype) @ wo
    return out, S_f, S_b


def swiglu(h, w1, w3, w2):
    return (jax.nn.silu(h @ w1) * (h @ w3)) @ w2


def moe_swiglu(h, router, w1, w3, w2):
    logits = (h @ router).astype(jnp.float32)
    top_val, top_idx = lax.top_k(logits, TOP_K)
    top_w = jax.nn.softmax(top_val, axis=-1)
    gates = jnp.sum(jax.nn.one_hot(top_idx, N_EXPERTS, dtype=jnp.float32) * top_w[..., None], axis=-2)
    gates = gates.astype(h.dtype)
    out = jnp.zeros_like(h)
    for e in range(N_EXPERTS):
        out = out + gates[..., e:e + 1] * swiglu(h, w1[e], w3[e], w2[e])
    return out


def modulation(cond, ada_w, ada_b):
    m = jax.nn.silu(cond) @ ada_w + ada_b
    return [a[:, None, :] for a in jnp.split(m, 6, axis=-1)]


def adaln_in(x, gain, shift, scale):
    return rmsnorm(x, gain) * (1.0 + scale) + shift


def adaln_out(x, y, gain, gate):
    return x + gate * rmsnorm(y, gain)


def setup_inputs(seed: int = 0) -> dict:
    key = jax.random.key(seed)
    ks = iter(jax.random.split(key, 40))

    def nrm(shape, scale):
        return jax.random.normal(next(ks), shape, jnp.float32) * scale

    def gain(shape):
        return 1.0 + nrm(shape, 0.02)

    D = D_MODEL
    return {
        "x_prompt": nrm((BATCH, SEQ, D), 1.0),
        "x_sample": nrm((DEC_BATCH, DEC_SEQ, D), 1.0),
        "c": nrm((DEC_BATCH, D), 1.0),
        "cache_mla_ckv": nrm((DEC_BATCH, N_A, PAST_LEN, MLA_KV_LORA), 1.0),
        "cache_mla_kpe": nrm((DEC_BATCH, N_A, PAST_LEN, MLA_ROPE), 1.0),
        "state_hgrn": nrm((DEC_BATCH, N_B, 2, HG_HEADS, HG_DK, HG_DV), 0.5),
        "cache_gqa_k": nrm((DEC_BATCH, N_C, PAST_LEN, GQA_KV_HEADS, GQA_HD), 1.0),
        "cache_gqa_v": nrm((DEC_BATCH, N_C, PAST_LEN, GQA_KV_HEADS, GQA_HD), 1.0),
        "c_ctx": nrm((D,), 1.0),
        "ada_w": nrm((DEPTH, D, 6 * D), 0.5 * D ** -0.5),
        "ada_b": nrm((DEPTH, 6 * D), 0.02),
        "norm_w": gain((DEPTH, 4, D)),
        "mla_wq_a": nrm((N_A, D, MLA_Q_LORA), D ** -0.5),
        "mla_q_norm": gain((N_A, MLA_Q_LORA)),
        "mla_wq_b": nrm((N_A, MLA_Q_LORA, MLA_HEADS * (MLA_NOPE + MLA_ROPE)), MLA_Q_LORA ** -0.5),
        "mla_wkv_a": nrm((N_A, D, MLA_KV_LORA + MLA_ROPE), D ** -0.5),
        "mla_kv_norm": gain((N_A, MLA_KV_LORA)),
        "mla_wkv_b": nrm((N_A, MLA_KV_LORA, MLA_HEADS * (MLA_NOPE + MLA_V)), MLA_KV_LORA ** -0.5),
        "mla_wo": nrm((N_A, MLA_HEADS * MLA_V, D), (MLA_HEADS * MLA_V) ** -0.5),
        "hgrn_w_in": nrm((N_B, D, 5 * D), D ** -0.5),
        "hgrn_lb_logits": nrm((2, DEPTH, D), 0.5),
        "hgrn_o_norm": gain((N_B, D)),
        "hgrn_wo": nrm((N_B, D, D), D ** -0.5),
        "gqa_w_qkv": nrm((N_C, D, (GQA_HEADS + 2 * GQA_KV_HEADS) * GQA_HD), D ** -0.5),
        "gqa_q_norm": gain((N_C, GQA_HD)),
        "gqa_k_norm": gain((N_C, GQA_HD)),
        "gqa_wo": nrm((N_C, GQA_HEADS * GQA_HD, D), (GQA_HEADS * GQA_HD) ** -0.5),
        "ffn_w1": nrm((N_DENSE, D, FF_DIM), D ** -0.5),
        "ffn_w3": nrm((N_DENSE, D, FF_DIM), D ** -0.5),
        "ffn_w2": nrm((N_DENSE, FF_DIM, D), FF_DIM ** -0.5),
        "moe_router": nrm((N_MOE, D, N_EXPERTS), D ** -0.5),
        "moe_w1": nrm((N_MOE, N_EXPERTS, D, FF_EXPERT), D ** -0.5),
        "moe_w3": nrm((N_MOE, N_EXPERTS, D, FF_EXPERT), D ** -0.5),
        "moe_w2": nrm((N_MOE, N_EXPERTS, FF_EXPERT, D), FF_EXPERT ** -0.5),
    }


def reference(x_prompt, x_sample, c, cache_mla_ckv, cache_mla_kpe, state_hgrn, cache_gqa_k,
              cache_gqa_v, c_ctx, ada_w, ada_b, norm_w, mla_wq_a, mla_q_norm, mla_wq_b,
              mla_wkv_a, mla_kv_norm, mla_wkv_b, mla_wo, hgrn_w_in, hgrn_lb_logits, hgrn_o_norm,
              hgrn_wo, gqa_w_qkv, gqa_q_norm, gqa_k_norm, gqa_wo, ffn_w1, ffn_w3, ffn_w2,
              moe_router, moe_w1, moe_w3, moe_w2):
    y_prompt = x_prompt
    y_sample = x_sample
    ctx_cond = c_ctx[None, :]
    Bp = x_prompt.shape[0]
    ckv_list, kpe_list, hg_list, gk_list, gv_list = [], [], [], [], []

    for layer in range(DEPTH):
        kind = layer % N_MIXERS
        j = layer // N_MIXERS
        nw = norm_w[layer]
        sh1_p, sc1_p, g1_p, sh2_p, sc2_p, g2_p = modulation(ctx_cond, ada_w[layer], ada_b[layer])
        sh1_s, sc1_s, g1_s, sh2_s, sc2_s, g2_s = modulation(c, ada_w[layer], ada_b[layer])
        h_p = adaln_in(y_prompt, nw[0], sh1_p, sc1_p)
        h_s = adaln_in(y_sample, nw[0], sh1_s, sc1_s)

        if kind == 0:
            p = (mla_wq_a[j], mla_q_norm[j], mla_wq_b[j], mla_wkv_a[j], mla_kv_norm[j],
                 mla_wkv_b[j], mla_wo[j])
            out_p, ckv, kpe = mla_context(h_p, *p)
            ckv_list.append(ckv)
            kpe_list.append(kpe)
            out_s = mla_latent(h_s, *p, cache_mla_ckv[:, j], cache_mla_kpe[:, j])
        elif kind == 1:
            lb = hgrn_lower_bounds(hgrn_lb_logits, layer)
            zeros = jnp.zeros((Bp, HG_HEADS, HG_DK, HG_DV), jnp.float32)
            out_p, S_f, S_b = hgrn_mixer(h_p, hgrn_w_in[j], lb, hgrn_o_norm[j], hgrn_wo[j],
                                         zeros, zeros)
            hg_list.append(jnp.stack([S_f, S_b], axis=1).astype(x_prompt.dtype))
            st = state_hgrn[:, j].astype(jnp.float32)
            out_s, _, _ = hgrn_mixer(h_s, hgrn_w_in[j], lb, hgrn_o_norm[j], hgrn_wo[j],
                                     st[:, 0], st[:, 1])
        else:
            p = (gqa_w_qkv[j], gqa_q_norm[j], gqa_k_norm[j], gqa_wo[j])
            out_p, k_ctx, v_ctx = gqa_context(h_p, *p)
            gk_list.append(k_ctx)
            gv_list.append(v_ctx)
            out_s = gqa_latent(h_s, *p, cache_gqa_k[:, j], cache_gqa_v[:, j])

        y_prompt = adaln_out(y_prompt, out_p, nw[1], g1_p)
        y_sample = adaln_out(y_sample, out_s, nw[1], g1_s)

        h_p = adaln_in(y_prompt, nw[2], sh2_p, sc2_p)
        h_s = adaln_in(y_sample, nw[2], sh2_s, sc2_s)
        fi = layer // 2
        if layer % 2 == 0:
            f_p = swiglu(h_p, ffn_w1[fi], ffn_w3[fi], ffn_w2[fi])
            f_s = swiglu(h_s, ffn_w1[fi], ffn_w3[fi], ffn_w2[fi])
        else:
            f_p = moe_swiglu(h_p, moe_router[fi], moe_w1[fi], moe_w3[fi], moe_w2[fi])
            f_s = moe_swiglu(h_s, moe_router[fi], moe_w1[fi], moe_w3[fi], moe_w2[fi])
        y_prompt = adaln_out(y_prompt, f_p, nw[3], g2_p)
        y_sample = adaln_out(y_sample, f_s, nw[3], g2_s)

    new_mla_ckv = jnp.stack(ckv_list, axis=1)
    new_mla_kpe = jnp.stack(kpe_list, axis=1)
    new_hgrn_state = jnp.stack(hg_list, axis=1)
    new_gqa_k = jnp.stack(gk_list, axis=1)
    new_gqa_v = jnp.stack(gv_list, axis=1)
    return (y_prompt, y_sample, new_mla_ckv, new_mla_kpe, new_hgrn_state, new_gqa_k, new_gqa_v)
```

```python
import functools
import math

import numpy as np
import jax
import jax.numpy as jnp
from jax import lax
from jax.experimental import pallas as pl
from jax.experimental.pallas import tpu as pltpu

F32 = jnp.float32
BF16 = jnp.bfloat16
EPS = 1e-6
LANE = 128

GRID_W = 64
ROPE_THETA = 10000.0
MLA_HEADS, MLA_NOPE, MLA_ROPE, MLA_V = 16, 128, 64, 128
HG_HEADS = 16
GQA_HEADS, GQA_KV_HEADS, GQA_HD = 16, 4, 128
N_EXPERTS = 8

VMEM_LIMIT_BYTES = 56 * 1024 * 1024

SDS = jax.ShapeDtypeStruct


def _cparams(*sem):
    return pltpu.CompilerParams(dimension_semantics=sem, vmem_limit_bytes=VMEM_LIMIT_BYTES)


def _tile(n, pref, step=8):
    t = min(pref, n)
    while n % t:
        t -= step
    return t


def _dot(a, b):
    return jnp.dot(a, b, preferred_element_type=F32)


def _dot_nt(a, b):
    return lax.dot_general(a, b, (((1,), (1,)), ((), ())), preferred_element_type=F32)


def _dot_tn(a, b):
    return lax.dot_general(a, b, (((0,), (0,)), ((), ())), preferred_element_type=F32)


def _rms(x):
    return x * lax.rsqrt(jnp.mean(x * x, axis=-1, keepdims=True) + EPS)


def _silu(x):
    return x * jax.nn.sigmoid(x)


def _row_tile(n_prompt, t_sample, pref):
    return _tile(math.gcd(n_prompt, t_sample), pref)


def _group_of(i, tm, n_prompt, t_sample):
    r = i * tm
    return jnp.where(r < n_prompt, 0, 1 + (r - n_prompt) // t_sample)


def _mod_kernel(c_ref, w_ref, b_ref, o_ref):
    s = _silu(c_ref[...]).astype(BF16)
    o_ref[0] = _dot(s, w_ref[0].astype(BF16)) + b_ref[0]


def _modulation(cond8, ada_w, ada_b):
    L, D, N6 = ada_w.shape
    tn = _tile(N6, 1024)
    return pl.pallas_call(
        _mod_kernel,
        out_shape=SDS((L, 8, N6), F32),
        grid=(L, N6 // tn),
        in_specs=[
            pl.BlockSpec((8, D), lambda l, j: (0, 0)),
            pl.BlockSpec((1, D, tn), lambda l, j: (l, 0, j)),
            pl.BlockSpec((1, 1, tn), lambda l, j: (l, 0, j)),
        ],
        out_specs=pl.BlockSpec((1, 8, tn), lambda l, j: (l, 0, j)),
        compiler_params=_cparams("parallel", "parallel"),
        name="modulation",
    )(cond8, ada_w, ada_b.reshape(L, 1, N6))


def _adaln_in_kernel(y_ref, g_ref, sh_ref, sc_ref, h_ref):
    h = (_rms(y_ref[...]) * g_ref[...]) * (1.0 + sc_ref[0]) + sh_ref[0]
    h_ref[...] = h.astype(h_ref.dtype)


def _to_slab(ref, x):
    rows, D = x.shape
    S = D // LANE
    for s in range(S):
        ref[pl.ds(s, rows, stride=S), :] = x[:, s * LANE:(s + 1) * LANE]


def _from_slab(ref, rows, s, S):
    return ref[pl.ds(s, rows, stride=S), :]


def _adaln_route_kernel(y_ref, g_ref, sh_ref, sc_ref, r_ref, h_ref, sel_ref, gw_ref):
    h = (_rms(y_ref[...]) * g_ref[...]) * (1.0 + sc_ref[0]) + sh_ref[0]
    _to_slab(h_ref, h)
    logits = lax.dot_general(h, r_ref[...], (((1,), (0,)), ((), ())),
                             precision=lax.Precision.HIGHEST, preferred_element_type=F32)
    lane = lax.broadcasted_iota(jnp.int32, logits.shape, 1)
    neg = jnp.float32(-jnp.inf)
    logits = jnp.where(lane < N_EXPERTS, logits, neg)
    lane_f = lane.astype(F32)
    m1 = jnp.max(logits, axis=-1, keepdims=True)
    i1 = jnp.min(jnp.where(logits == m1, lane_f, float(LANE)), axis=-1, keepdims=True)
    rest = jnp.where(lane_f == i1, neg, logits)
    m2 = jnp.max(rest, axis=-1, keepdims=True)
    i2 = jnp.min(jnp.where(rest == m2, lane_f, float(LANE)), axis=-1, keepdims=True)
    e = jnp.exp(m2 - m1)
    w1 = 1.0 / (1.0 + e)
    w2 = e / (1.0 + e)
    i1, i2 = i1.astype(jnp.int32), i2.astype(jnp.int32)
    sel_ref[...] = jnp.where(lane == 0, i1, jnp.where(lane == 1, i2, 0))
    gw_ref[...] = jnp.where(lane == 0, w1, jnp.where(lane == 1, w2, 0.0))


def _adaln_in(y, gain, shift, scale, *, n_prompt, t_sample, tm, router=None):
    M, D = y.shape
    grp = functools.partial(_group_of, tm=tm, n_prompt=n_prompt, t_sample=t_sample)
    in_specs = [
        pl.BlockSpec((tm, D), lambda i: (i, 0)),
        pl.BlockSpec((1, D), lambda i: (0, 0)),
        pl.BlockSpec((1, 1, D), lambda i: (grp(i), 0, 0)),
        pl.BlockSpec((1, 1, D), lambda i: (grp(i), 0, 0)),
    ]
    args = [y, gain.reshape(1, D), shift, scale]
    if router is None:
        return pl.pallas_call(
            _adaln_in_kernel,
            out_shape=SDS((M, D), BF16),
            grid=(M // tm,),
            in_specs=in_specs,
            out_specs=pl.BlockSpec((tm, D), lambda i: (i, 0)),
            compiler_params=_cparams("parallel"),
            name="adaln_in",
        )(*args)
    r_pad = jnp.pad(router, ((0, 0), (0, LANE - router.shape[1])))
    return pl.pallas_call(
        _adaln_route_kernel,
        out_shape=(SDS((M * (D // LANE), LANE), F32), SDS((M, LANE), jnp.int32), SDS((M, LANE), F32)),
        grid=(M // tm,),
        in_specs=in_specs + [pl.BlockSpec((D, LANE), lambda i: (0, 0))],
        out_specs=(
            pl.BlockSpec((tm * (D // LANE), LANE), lambda i: (i, 0)),
            pl.BlockSpec((tm, LANE), lambda i: (i, 0)),
            pl.BlockSpec((tm, LANE), lambda i: (i, 0)),
        ),
        compiler_params=_cparams("parallel"),
        name="adaln_route",
    )(*args, r_pad)


def _mm_kernel(x_ref, w_ref, o_ref, wbf):
    @pl.when(pl.program_id(1) == 0)
    def _():
        wbf[...] = w_ref[...].astype(BF16)

    o_ref[...] = _dot(x_ref[...].astype(BF16), wbf[...]).astype(o_ref.dtype)


def _mm(x, w, *, out_dtype, tm=1024, tn=1024):
    M, K = x.shape
    N = w.shape[1]
    tm = _tile(M, tm)
    tn = _tile(N, tn, LANE)
    return pl.pallas_call(
        _mm_kernel,
        out_shape=SDS((M, N), out_dtype),
        grid=(N // tn, M // tm),
        in_specs=[
            pl.BlockSpec((tm, K), lambda j, i: (i, 0)),
            pl.BlockSpec((K, tn), lambda j, i: (0, j)),
        ],
        out_specs=pl.BlockSpec((tm, tn), lambda j, i: (i, j)),
        scratch_shapes=[pltpu.VMEM((K, tn), BF16)],
        compiler_params=_cparams("parallel", "arbitrary"),
        name="mm",
    )(x, w)


def _mm_resid_kernel(x_ref, w_ref, y_ref, g_ref, gate_ref, o_ref, wbf):
    @pl.when(pl.program_id(0) == 0)
    def _():
        wbf[...] = w_ref[...].astype(BF16)

    a = _dot(x_ref[...].astype(BF16), wbf[...])
    o_ref[...] = y_ref[...] + gate_ref[0] * (_rms(a) * g_ref[...])


def _mm_resid(x, w, y, gain, gate, *, n_prompt, t_sample, tm=256):
    M, K = x.shape
    D = w.shape[1]
    tm = _row_tile(n_prompt, t_sample, tm)
    grp = functools.partial(_group_of, tm=tm, n_prompt=n_prompt, t_sample=t_sample)
    return pl.pallas_call(
        _mm_resid_kernel,
        out_shape=SDS((M, D), F32),
        grid=(M // tm,),
        in_specs=[
            pl.BlockSpec((tm, K), lambda i: (i, 0)),
            pl.BlockSpec((K, D), lambda i: (0, 0), pipeline_mode=pl.Buffered(1)),
            pl.BlockSpec((tm, D), lambda i: (i, 0)),
            pl.BlockSpec((1, D), lambda i: (0, 0)),
            pl.BlockSpec((1, 1, D), lambda i: (grp(i), 0, 0)),
        ],
        out_specs=pl.BlockSpec((tm, D), lambda i: (i, 0)),
        scratch_shapes=[pltpu.VMEM((K, D), BF16)],
        compiler_params=_cparams("arbitrary"),
        name="mm_resid",
    )(x, w, y, gain.reshape(1, D), gate)


def _ffn_kernel(y_ref, gin_ref, sh_ref, sc_ref, w1_ref, w3_ref, w2_ref, gout_ref, gate_ref,
                o_ref, hbf, acc, *, nf):
    f = pl.program_id(1)

    @pl.when(f == 0)
    def _():
        h = (_rms(y_ref[...]) * gin_ref[...]) * (1.0 + sc_ref[0]) + sh_ref[0]
        hbf[...] = h.astype(BF16)

    h = hbf[...]
    a = _dot(h, w1_ref[...].astype(BF16))
    b = _dot(h, w3_ref[...].astype(BF16))
    p = _dot((_silu(a) * b).astype(BF16), w2_ref[...].astype(BF16))

    @pl.when(f == 0)
    def _():
        acc[...] = p

    @pl.when(f > 0)
    def _():
        acc[...] += p

    @pl.when(f == nf - 1)
    def _():
        o_ref[...] = y_ref[...] + gate_ref[0] * (_rms(acc[...]) * gout_ref[...])


def _ffn(y, gin, shift, scale, w1, w3, w2, gout, gate, *, n_prompt, t_sample, tm=512, tf=512):
    M, D = y.shape
    F = w1.shape[1]
    tm = _row_tile(n_prompt, t_sample, tm)
    tf = _tile(F, tf, LANE)
    nf = F // tf
    grp = functools.partial(_group_of, tm=tm, n_prompt=n_prompt, t_sample=t_sample)
    vec = pl.BlockSpec((1, D), lambda i, f: (0, 0))
    gvec = pl.BlockSpec((1, 1, D), lambda i, f: (grp(i), 0, 0))
    return pl.pallas_call(
        functools.partial(_ffn_kernel, nf=nf),
        out_shape=SDS((M, D), F32),
        grid=(M // tm, nf),
        in_specs=[
            pl.BlockSpec((tm, D), lambda i, f: (i, 0)),
            vec, gvec, gvec,
            pl.BlockSpec((D, tf), lambda i, f: (0, f)),
            pl.BlockSpec((D, tf), lambda i, f: (0, f)),
            pl.BlockSpec((tf, D), lambda i, f: (f, 0)),
            vec, gvec,
        ],
        out_specs=pl.BlockSpec((tm, D), lambda i, f: (i, 0)),
        scratch_shapes=[pltpu.VMEM((tm, D), BF16), pltpu.VMEM((tm, D), F32)],
        compiler_params=_cparams("parallel", "arbitrary"),
        name="ffn_dense",
    )(y, gin.reshape(1, D), shift, scale, w1, w3, w2, gout.reshape(1, D), gate)


def _moe_kernel(te_ref, tv_ref, x_ref, gate_ref, w1_ref, w3_ref, w2_ref, o_ref, xbf, acc, *, nf, tm, S):
    i = pl.program_id(0)
    f = pl.program_id(1)
    valid = tv_ref[i] > 0

    @pl.when(jnp.logical_and(jnp.logical_not(valid), f == 0))
    def _():
        o_ref[...] = jnp.zeros_like(o_ref)

    @pl.when(valid)
    def _():
        @pl.when(f == 0)
        def _():
            for s in range(S):
                xbf[:, s * LANE:(s + 1) * LANE] = _from_slab(x_ref, tm, s, S).astype(BF16)

        h = xbf[...]
        a = _dot(h, w1_ref[0].astype(BF16))
        b = _dot(h, w3_ref[0].astype(BF16))
        p = _dot((_silu(a) * b).astype(BF16), w2_ref[0].astype(BF16))

        @pl.when(f == 0)
        def _():
            acc[...] = p

        @pl.when(f > 0)
        def _():
            acc[...] += p

        @pl.when(f == nf - 1)
        def _():
            _to_slab(o_ref, acc[...] * gate_ref[...])


def _moe_ffn(x_sorted, gate_rows, tile_e, tile_v, w1, w3, w2, *, tm, tf=256):
    D = w1.shape[1]
    S = D // LANE
    R = x_sorted.shape[0] // S
    F = w1.shape[2]
    tf = _tile(F, tf, LANE)
    nf = F // tf

    def fidx(i, f, tv):
        return jnp.where(tv[i] > 0, f, nf - 1)

    grid_spec = pltpu.PrefetchScalarGridSpec(
        num_scalar_prefetch=2,
        grid=(R // tm, nf),
        in_specs=[
            pl.BlockSpec((tm * S, LANE), lambda i, f, te, tv: (i, 0)),
            pl.BlockSpec((tm, 1), lambda i, f, te, tv: (i, 0)),
            pl.BlockSpec((1, D, tf), lambda i, f, te, tv: (te[i], 0, fidx(i, f, tv))),
            pl.BlockSpec((1, D, tf), lambda i, f, te, tv: (te[i], 0, fidx(i, f, tv))),
            pl.BlockSpec((1, tf, D), lambda i, f, te, tv: (te[i], fidx(i, f, tv), 0)),
        ],
        out_specs=pl.BlockSpec((tm * S, LANE), lambda i, f, te, tv: (i, 0)),
        scratch_shapes=[pltpu.VMEM((tm, D), BF16), pltpu.VMEM((tm, D), F32)],
    )
    return pl.pallas_call(
        functools.partial(_moe_kernel, nf=nf, tm=tm, S=S),
        out_shape=SDS((R * S, LANE), F32),
        grid_spec=grid_spec,
        compiler_params=_cparams("parallel", "arbitrary"),
        name="moe_ffn",
    )(tile_e, tile_v, x_sorted, gate_rows, w1, w3, w2)


def _gather_rows(src, idx, S, *, rows=256):
    Rd = idx.shape[0]
    rows = _tile(Rd, rows)

    def kern(idx_ref, src_ref, dst_ref, sem):
        base = pl.program_id(0) * rows

        def issue(r, c):
            s0 = pl.multiple_of(idx_ref[base + r] * S, S)
            d0 = pl.multiple_of((base + r) * S, S)
            pltpu.make_async_copy(src_ref.at[pl.ds(s0, S)], dst_ref.at[pl.ds(d0, S)], sem).start()
            return c

        lax.fori_loop(0, rows, issue, 0)
        d0 = pl.multiple_of(base * S, S)
        pltpu.make_async_copy(src_ref.at[pl.ds(0, rows * S)], dst_ref.at[pl.ds(d0, rows * S)], sem).wait()

    grid_spec = pltpu.PrefetchScalarGridSpec(
        num_scalar_prefetch=1,
        grid=(Rd // rows,),
        in_specs=[pl.BlockSpec(memory_space=pl.ANY)],
        out_specs=pl.BlockSpec(memory_space=pl.ANY),
        scratch_shapes=[pltpu.SemaphoreType.DMA(())],
    )
    return pl.pallas_call(
        kern,
        out_shape=SDS((Rd * S, LANE), src.dtype),
        grid_spec=grid_spec,
        compiler_params=pltpu.CompilerParams(dimension_semantics=("arbitrary",), has_side_effects=True),
        name="gather_rows",
    )(idx, src)


def _combine_kernel(a_ref, b_ref, y_ref, g_ref, gate_ref, o_ref, *, tm, S):
    parts = [_from_slab(a_ref, tm, s, S) + _from_slab(b_ref, tm, s, S) for s in range(S)]
    ss = parts[0] * parts[0]
    for t in parts[1:]:
        ss = ss + t * t
    r = lax.rsqrt(jnp.sum(ss, axis=-1, keepdims=True) / (S * LANE) + EPS)
    for s, t in enumerate(parts):
        c = slice(s * LANE, (s + 1) * LANE)
        o_ref[:, c] = y_ref[:, c] + gate_ref[0][:, c] * (t * r * g_ref[:, c])


def _moe_combine(slots, y, gain, gate, *, n_prompt, t_sample, tm=256):
    M, D = y.shape
    S = D // LANE
    tm = _row_tile(n_prompt, t_sample, tm)
    nb = M // tm
    grp = functools.partial(_group_of, tm=tm, n_prompt=n_prompt, t_sample=t_sample)
    return pl.pallas_call(
        functools.partial(_combine_kernel, tm=tm, S=S),
        out_shape=SDS((M, D), F32),
        grid=(nb,),
        in_specs=[
            pl.BlockSpec((tm * S, LANE), lambda i: (i, 0)),
            pl.BlockSpec((tm * S, LANE), lambda i: (nb + i, 0)),
            pl.BlockSpec((tm, D), lambda i: (i, 0)),
            pl.BlockSpec((1, D), lambda i: (0, 0)),
            pl.BlockSpec((1, 1, D), lambda i: (grp(i), 0, 0)),
        ],
        out_specs=pl.BlockSpec((tm, D), lambda i: (i, 0)),
        compiler_params=_cparams("parallel"),
        name="moe_combine",
    )(slots, slots, y, gain.reshape(1, D), gate)


def _dispatch_tables(sel, gw, tm):
    M = sel.shape[0]
    E = N_EXPERTS
    e_flat = sel.reshape(-1)
    order = jnp.argsort(e_flat, stable=True).astype(jnp.int32)
    counts = jnp.sum(e_flat[:, None] == jnp.arange(E, dtype=jnp.int32)[None, :], axis=0).astype(jnp.int32)
    padded = ((counts + tm - 1) // tm) * tm
    ends_p = jnp.cumsum(padded)
    starts_p = ends_p - padded
    starts = jnp.cumsum(counts) - counts
    e_sorted = e_flat[order]
    pos_sorted = starts_p[e_sorted] + jnp.arange(2 * M, dtype=jnp.int32) - starts[e_sorted]
    R = 2 * M + E * tm
    src_tok = jnp.zeros((R,), jnp.int32).at[pos_sorted].set(order // 2)
    gate_rows = jnp.zeros((R,), F32).at[pos_sorted].set(gw.reshape(-1)[order])
    pos_of_pair = jnp.zeros((2 * M,), jnp.int32).at[order].set(pos_sorted).reshape(M, 2)
    tile_start = jnp.arange(R // tm, dtype=jnp.int32) * tm
    tile_e = jnp.minimum(jnp.searchsorted(ends_p, tile_start, side="right"), E - 1).astype(jnp.int32)
    tile_v = (tile_start < ends_p[-1]).astype(jnp.int32)
    slot_rows = pos_of_pair.T.reshape(-1)
    return src_tok, gate_rows.reshape(R, 1), tile_e, tile_v, slot_rows


def _attn_kernel(*refs, n_parts, scale, nk):
    q_refs = refs[:n_parts]
    k_refs = refs[n_parts:2 * n_parts]
    v_ref = refs[2 * n_parts]
    o_ref = refs[2 * n_parts + 1]
    m_sc, l_sc, acc_sc = refs[2 * n_parts + 2:]
    j = pl.program_id(3)

    @pl.when(j == 0)
    def _():
        m_sc[...] = jnp.full_like(m_sc, -jnp.inf)
        l_sc[...] = jnp.zeros_like(l_sc)
        acc_sc[...] = jnp.zeros_like(acc_sc)

    s = _dot_nt(q_refs[0][...].astype(BF16), k_refs[0][...].astype(BF16))
    for p in range(1, n_parts):
        s = s + _dot_nt(q_refs[p][...].astype(BF16), k_refs[p][...].astype(BF16))
    s = s * scale
    m_prev = m_sc[...]
    m_new = jnp.maximum(m_prev, jnp.max(s, axis=-1, keepdims=True))
    alpha = jnp.exp(m_prev - m_new)
    p = jnp.exp(s - m_new)
    l_sc[...] = alpha * l_sc[...] + jnp.sum(p, axis=-1, keepdims=True)
    acc_sc[...] = alpha * acc_sc[...] + _dot(p.astype(BF16), v_ref[...].astype(BF16))
    m_sc[...] = m_new

    @pl.when(j == nk - 1)
    def _():
        o_ref[...] = (acc_sc[...] / l_sc[...]).astype(o_ref.dtype)


def _attention(q_parts, k_parts, v_part, *, B, T, S, H, group, q_row0, k_row0, tq, tk, scale):
    tq = _tile(T, tq)
    tk = _tile(S, tk)
    nq, nk = T // tq, S // tk
    qb0, kb0 = q_row0 // tq, k_row0 // tk

    def q_spec(off, per_head):
        return pl.BlockSpec((tq, LANE), lambda b, h, i, j: (qb0 + b * nq + i, off + (h if per_head else 0)))

    def k_spec(off, per_head):
        return pl.BlockSpec((tk, LANE),
                            lambda b, h, i, j: (kb0 + b * nk + j, off + (h // group if per_head else 0)))

    in_specs = [q_spec(off, ph) for (_, off, ph) in q_parts]
    in_specs += [k_spec(off, ph) for (_, off, ph) in k_parts]
    in_specs += [k_spec(v_part[1], v_part[2])]
    args = [a for (a, _, _) in q_parts] + [a for (a, _, _) in k_parts] + [v_part[0]]
    return pl.pallas_call(
        functools.partial(_attn_kernel, n_parts=len(q_parts), scale=scale, nk=nk),
        out_shape=SDS((B * T, H * LANE), BF16),
        grid=(B, H, nq, nk),
        in_specs=in_specs,
        out_specs=pl.BlockSpec((tq, LANE), lambda b, h, i, j: (b * nq + i, h)),
        scratch_shapes=[pltpu.VMEM((tq, 1), F32), pltpu.VMEM((tq, 1), F32), pltpu.VMEM((tq, LANE), F32)],
        compiler_params=_cparams("parallel", "parallel", "parallel", "arbitrary"),
        name="attention",
    )(*args)


def _rmsnorm_cols_kernel(x_ref, g_ref, o_ref):
    o_ref[...] = (_rms(x_ref[...]) * g_ref[...]).astype(o_ref.dtype)


def _rmsnorm_cols(x, gain, *, col_block, width, out_dtype, tm=1024):
    M = x.shape[0]
    tm = _tile(M, tm)
    return pl.pallas_call(
        _rmsnorm_cols_kernel,
        out_shape=SDS((M, width), out_dtype),
        grid=(M // tm,),
        in_specs=[pl.BlockSpec((tm, width), lambda i: (i, col_block)),
                  pl.BlockSpec((1, width), lambda i: (0, 0))],
        out_specs=pl.BlockSpec((tm, width), lambda i: (i, 0)),
        compiler_params=_cparams("parallel"),
        name="rmsnorm_cols",
    )(x, gain.reshape(1, width))


def _head_rope_kernel(x_ref, g_ref, cos_ref, sin_ref, o_ref, *, norm, half):
    x = x_ref[...]
    if norm:
        x = _rms(x) * g_ref[0]
    lane = lax.broadcasted_iota(jnp.int32, x.shape, 1)
    first = (lane % (2 * half)) < half
    rot = jnp.where(first, -pltpu.roll(x, LANE - half, 1), pltpu.roll(x, half, 1))
    o_ref[...] = (x * cos_ref[...] + rot * sin_ref[...]).astype(o_ref.dtype)


def _head_rope(x, gains, cos, sin, *, col_block0, n_heads, norm, half, out_dtype, n_rows=None, tm=1024):
    M = x.shape[0] if n_rows is None else n_rows
    tm = _tile(M, tm)
    return pl.pallas_call(
        functools.partial(_head_rope_kernel, norm=norm, half=half),
        out_shape=SDS((M, n_heads * LANE), out_dtype),
        grid=(M // tm, n_heads),
        in_specs=[
            pl.BlockSpec((tm, LANE), lambda i, h: (i, col_block0 + h)),
            pl.BlockSpec((1, 1, LANE), lambda i, h: (h, 0, 0)),
            pl.BlockSpec((tm, LANE), lambda i, h: (i, 0)),
            pl.BlockSpec((tm, LANE), lambda i, h: (i, 0)),
        ],
        out_specs=pl.BlockSpec((tm, LANE), lambda i, h: (i, h)),
        compiler_params=_cparams("parallel", "parallel"),
        name="head_rope",
    )(x, gains, cos, sin)


def _rope_tables(n_prompt, dec_batch, t_sample, half):
    tok = jnp.arange(t_sample, dtype=jnp.int32)
    row, col = tok // GRID_W, tok % GRID_W
    inv = ROPE_THETA ** (-jnp.arange(half, dtype=F32) / half)

    def cs(pos):
        ang = pos.astype(F32)[:, None] * inv[None, :]
        return (jnp.concatenate([jnp.cos(ang), jnp.cos(ang)], -1),
                jnp.concatenate([jnp.sin(ang), jnp.sin(ang)], -1))

    cr, sr = cs(row)
    cc, sc = cs(col)
    pad = LANE - 4 * half
    cos = jnp.concatenate([cr, cc, jnp.ones((t_sample, pad), F32)], -1)
    sin = jnp.concatenate([sr, sc, jnp.zeros((t_sample, pad), F32)], -1)
    cos = jnp.concatenate([jnp.ones((n_prompt, LANE), F32), jnp.tile(cos, (dec_batch, 1))], 0)
    sin = jnp.concatenate([jnp.zeros((n_prompt, LANE), F32), jnp.tile(sin, (dec_batch, 1))], 0)
    return cos, sin


def _hgrn_constants(C, reverse):
    L = int(np.log2(C))
    t = np.arange(C)[:, None]
    u = np.arange(C)[None, :]
    Ws, Ms = [], []
    for lvl in range(L):
        b = 1 << lvl
        grp = t // (2 * b)
        if not reverse:
            bnd = grp * 2 * b + b - 1
            qside = (t % (2 * b)) >= b
            W = np.where(qside, (u > bnd) & (u <= t), (u > t) & (u <= bnd))
            kside_s = (u % (2 * b)) < b
        else:
            bnd = grp * 2 * b + b
            qside = (t % (2 * b)) < b
            W = np.where(qside, (u >= t) & (u < bnd), (u >= bnd) & (u < t))
            kside_s = (u % (2 * b)) >= b
        Ws.append(W)
        Ms.append(qside & kside_s & (grp == u // (2 * b)))
    if not reverse:
        Ws += [u <= t, u > t]
    else:
        Ws += [u >= t, u < t]
    Ms.append(t == u)
    W_all = jnp.asarray(np.concatenate(Ws, 0).astype(np.float32), BF16)
    M_all = jnp.asarray(np.stack(Ms).astype(np.float32), F32)
    return W_all, M_all


def _hgrn_chain(q, v, z, lb, W_all, m_ref, st, *, C, L, edge_row):
    f = lb + (1.0 - lb) * jax.nn.sigmoid(z)
    k = 1.0 - f
    lf = jnp.log(f)
    hi = lf.astype(BF16)
    r1 = lf - hi.astype(F32)
    mid = r1.astype(BF16)
    lo = (r1 - mid.astype(F32)).astype(BF16)
    d3 = _dot(W_all, jnp.concatenate([hi, mid, lo], axis=1))
    dn = d3[:, :LANE] + d3[:, LANE:2 * LANE] + d3[:, 2 * LANE:]
    e_all = jnp.exp(dn)
    qb = q.astype(BF16)
    a = m_ref[L] * _dot_nt(qb, k.astype(BF16))
    for lvl in range(L):
        e = e_all[lvl * C:(lvl + 1) * C]
        a = a + m_ref[lvl] * _dot_nt((q * e).astype(BF16), (k * e).astype(BF16))
    eq = e_all[L * C:(L + 1) * C]
    ek = e_all[(L + 1) * C:(L + 2) * C]
    vb = v.astype(BF16)
    o = _dot(a.astype(BF16), vb) + _dot_nt((q * eq).astype(BF16), st.astype(BF16))
    g_edge = dn[L * C + edge_row:L * C + edge_row + 1]
    st_new = st * jnp.exp(g_edge) + _dot_tn(vb, (k * ek).astype(BF16))
    return o, st_new


def _hgrn_kernel(*refs, C, L, nc, has_init):
    (qf_ref, vf_ref, zf_ref, qb_ref, vb_ref, zb_ref, lbf_ref, lbb_ref,
     wf_ref, mf_ref, wb_ref, mb_ref) = refs[:12]
    rest = refs[12:]
    if has_init:
        s0_ref, rest = rest[0], rest[1:]
    of_ref, ob_ref, sfin_ref, stf, stb = rest
    c = pl.program_id(2)

    @pl.when(c == 0)
    def _():
        if has_init:
            stf[...] = s0_ref[0, 0, 0]
            stb[...] = s0_ref[0, 1, 0]
        else:
            stf[...] = jnp.zeros_like(stf)
            stb[...] = jnp.zeros_like(stb)

    o, s_new = _hgrn_chain(qf_ref[...], vf_ref[...], zf_ref[...], lbf_ref[0], wf_ref[...], mf_ref, stf[...],
                           C=C, L=L, edge_row=C - 1)
    of_ref[...] = o
    stf[...] = s_new
    o, s_new = _hgrn_chain(qb_ref[...], vb_ref[...], zb_ref[...], lbb_ref[0], wb_ref[...], mb_ref, stb[...],
                           C=C, L=L, edge_row=0)
    ob_ref[...] = o
    stb[...] = s_new

    @pl.when(c == nc - 1)
    def _():
        sfin_ref[0, 0, 0] = stf[...].T
        sfin_ref[0, 1, 0] = stb[...].T


def _hgrn_scan(proj, lb, s0t, *, B, T, row0, C=128):
    D = proj.shape[1] // 5
    H = D // LANE
    C = _tile(T, C)
    L = int(np.log2(C))
    assert 1 << L == C
    nc = T // C
    rb0 = row0 // C
    wf, mf = _hgrn_constants(C, False)
    wb, mb = _hgrn_constants(C, True)

    def fwd(col0):
        return pl.BlockSpec((C, LANE), lambda b, h, c: (rb0 + b * nc + c, col0 + h))

    def bwd(col0):
        return pl.BlockSpec((C, LANE), lambda b, h, c: (rb0 + b * nc + nc - 1 - c, col0 + h))

    def const(a):
        return pl.BlockSpec(a.shape, lambda b, h, c: (0,) * a.ndim)

    in_specs = [fwd(0), fwd(H), fwd(3 * H), bwd(0), bwd(H), bwd(4 * H),
                pl.BlockSpec((1, 1, LANE), lambda b, h, c: (0, 0, h)),
                pl.BlockSpec((1, 1, LANE), lambda b, h, c: (1, 0, h)),
                const(wf), const(mf), const(wb), const(mb)]
    args = [proj, proj, proj, proj, proj, proj, lb, lb, wf, mf, wb, mb]
    if s0t is not None:
        in_specs.append(pl.BlockSpec((1, 2, 1, LANE, LANE), lambda b, h, c: (b, 0, h, 0, 0)))
        args.append(s0t)
    return pl.pallas_call(
        functools.partial(_hgrn_kernel, C=C, L=L, nc=nc, has_init=s0t is not None),
        out_shape=(SDS((B * T, D), F32), SDS((B * T, D), F32), SDS((B, 2, H, LANE, LANE), F32)),
        grid=(B, H, nc),
        in_specs=in_specs,
        out_specs=(
            pl.BlockSpec((C, LANE), lambda b, h, c: (b * nc + c, h)),
            pl.BlockSpec((C, LANE), lambda b, h, c: (b * nc + nc - 1 - c, h)),
            pl.BlockSpec((1, 2, 1, LANE, LANE), lambda b, h, c: (b, 0, h, 0, 0)),
        ),
        scratch_shapes=[pltpu.VMEM((LANE, LANE), F32), pltpu.VMEM((LANE, LANE), F32)],
        compiler_params=_cparams("parallel", "parallel", "arbitrary"),
        name="hgrn_scan",
    )(*args)


def _hgrn_gate_kernel(of_ref, ob_ref, g_ref, w_ref, o_ref):
    o = _rms(of_ref[...] + ob_ref[...]) * w_ref[0]
    o_ref[...] = (o * _silu(g_ref[...])).astype(o_ref.dtype)


def _hgrn_gate(o_f, o_b, proj, o_norm, *, tm=1024):
    M, D = o_f.shape
    H = D // LANE
    tm = _tile(M, tm)
    blk = pl.BlockSpec((tm, LANE), lambda i, h: (i, h))
    return pl.pallas_call(
        _hgrn_gate_kernel,
        out_shape=SDS((M, D), BF16),
        grid=(M // tm, H),
        in_specs=[blk, blk,
                  pl.BlockSpec((tm, LANE), lambda i, h: (i, 2 * H + h)),
                  pl.BlockSpec((1, 1, LANE), lambda i, h: (h, 0, 0))],
        out_specs=blk,
        compiler_params=_cparams("parallel", "parallel"),
        name="hgrn_gate",
    )(o_f, o_b, proj, o_norm.reshape(H, 1, LANE))


def kernel(x_prompt, x_sample, c, cache_mla_ckv, cache_mla_kpe, state_hgrn, cache_gqa_k, cache_gqa_v, c_ctx, ada_w, ada_b, norm_w, mla_wq_a, mla_q_norm, mla_wq_b, mla_wkv_a, mla_kv_norm, mla_wkv_b, mla_wo, hgrn_w_in, hgrn_lb_logits, hgrn_o_norm, hgrn_wo, gqa_w_qkv, gqa_q_norm, gqa_k_norm, gqa_wo, ffn_w1, ffn_w3, ffn_w2, moe_router, moe_w1, moe_w3, moe_w2):
    Bp, Tp, D = x_prompt.shape
    Bs, Ts, _ = x_sample.shape
    P = cache_mla_ckv.shape[2]
    depth = ada_w.shape[0]
    NP, NS = Bp * Tp, Bs * Ts
    M = NP + NS
    G = 1 + Bs
    assert D == HG_HEADS * LANE
    rows = dict(n_prompt=NP, t_sample=Ts)

    y = jnp.concatenate([x_prompt.reshape(NP, D), x_sample.reshape(NS, D)], axis=0)
    cond8 = jnp.zeros((8, D), F32).at[0].set(c_ctx).at[1:G].set(c)
    mod = _modulation(cond8, ada_w, ada_b).reshape(depth, 8, 6, D)
    mod = jnp.transpose(mod, (0, 2, 1, 3))[:, :, :G, None, :]

    tm_tok = _row_tile(NP, Ts, 512)
    cos_mla, sin_mla = _rope_tables(NP, Bs, Ts, MLA_ROPE // 4)
    cos_gqa, sin_gqa = _rope_tables(NP, Bs, Ts, GQA_HD // 4)
    ones_g = jnp.ones((MLA_HEADS, 1, LANE), F32)

    ckv_list, kpe_list, hg_list, gk_list, gv_list = [], [], [], [], []
    for layer in range(depth):
        kind, j = layer % 3, layer // 3
        nw = norm_w[layer]
        sh1, sc1, g1, sh2, sc2, g2 = (mod[layer, k] for k in range(6))
        h = _adaln_in(y, nw[0], sh1, sc1, tm=tm_tok, **rows)

        if kind == 0:
            Hm = MLA_HEADS
            w_a = jnp.concatenate([mla_wq_a[j], mla_wkv_a[j],
                                   jnp.zeros((D, LANE - MLA_ROPE), F32)], axis=1)
            a = _mm(h, w_a, out_dtype=F32)
            ql, kvl = mla_wq_a.shape[2], mla_kv_norm.shape[1]
            qn = _rmsnorm_cols(a, mla_q_norm[j], col_block=0, width=ql, out_dtype=BF16)
            ckv = _rmsnorm_cols(a, mla_kv_norm[j], col_block=ql // kvl, width=kvl, out_dtype=F32)
            wqb = mla_wq_b[j].reshape(ql, Hm, MLA_NOPE + MLA_ROPE)
            wqb = jnp.concatenate([
                wqb[:, :, :MLA_NOPE].reshape(ql, Hm * MLA_NOPE),
                jnp.pad(wqb[:, :, MLA_NOPE:], ((0, 0), (0, 0), (0, LANE - MLA_ROPE))).reshape(ql, Hm * LANE),
            ], axis=1)
            q = _mm(qn, wqb, out_dtype=F32)
            pe_blk = (ql + kvl) // LANE
            q_pe = _head_rope(q, ones_g, cos_mla, sin_mla, col_block0=Hm, n_heads=Hm, norm=False,
                              half=MLA_ROPE // 4, out_dtype=BF16)
            k_pe = _head_rope(a, ones_g, cos_mla, sin_mla, col_block0=pe_blk, n_heads=1, norm=False,
                              half=MLA_ROPE // 4, out_dtype=BF16)
            wkvb = mla_wkv_b[j].reshape(kvl, Hm, MLA_NOPE + MLA_V)
            wkvb = jnp.concatenate([wkvb[:, :, :MLA_NOPE].reshape(kvl, Hm * MLA_NOPE),
                                    wkvb[:, :, MLA_NOPE:].reshape(kvl, Hm * MLA_V)], axis=1)
            ckv_s = jnp.concatenate([ckv[NP:].reshape(Bs, Ts, kvl), cache_mla_ckv[:, j]], axis=1)
            c_all = jnp.concatenate([ckv[:NP], ckv_s.reshape(Bs * (Ts + P), kvl)], axis=0)
            kv = _mm(c_all, wkvb, out_dtype=BF16)
            kpe_ctx = jnp.pad(cache_mla_kpe[:, j], ((0, 0), (0, 0), (0, LANE - MLA_ROPE))).astype(BF16)
            kpe_s = jnp.concatenate([k_pe[NP:].reshape(Bs, Ts, LANE), kpe_ctx], axis=1)
            kpe_all = jnp.concatenate([k_pe[:NP], kpe_s.reshape(Bs * (Ts + P), LANE)], axis=0)
            scale = (MLA_NOPE + MLA_ROPE) ** -0.5
            qp = [(q, 0, True), (q_pe, 0, True)]
            kp = [(kv, 0, True), (kpe_all, 0, False)]
            vp = (kv, Hm, True)
            o_p = _attention(qp, kp, vp, B=Bp, T=Tp, S=Tp, H=Hm, group=1, q_row0=0, k_row0=0,
                             tq=256, tk=256, scale=scale)
            o_s = _attention(qp, kp, vp, B=Bs, T=Ts, S=Ts + P, H=Hm, group=1, q_row0=NP, k_row0=NP,
                             tq=1024, tk=512, scale=scale)
            mix = jnp.concatenate([o_p, o_s], axis=0)
            w_o = mla_wo[j]
            ckv_list.append(ckv[:NP].reshape(Bp, Tp, kvl))
            kpe_list.append(a[:NP, ql + kvl:ql + kvl + MLA_ROPE].reshape(Bp, Tp, MLA_ROPE))
        elif kind == 1:
            H = HG_HEADS
            p = jax.nn.softmax(hgrn_lb_logits.astype(F32), axis=1)
            cum = jnp.cumsum(p, axis=1)
            lb = (cum - cum[:, :1])[:, layer].reshape(2, 1, D)
            proj = _mm(h, hgrn_w_in[j], out_dtype=F32)
            of_p, ob_p, st_p = _hgrn_scan(proj, lb, None, B=Bp, T=Tp, row0=0)
            s0t = jnp.swapaxes(state_hgrn[:, j].astype(F32), -1, -2)
            of_s, ob_s, _ = _hgrn_scan(proj, lb, s0t, B=Bs, T=Ts, row0=NP)
            o_f = jnp.concatenate([of_p, of_s], axis=0)
            o_b = jnp.concatenate([ob_p, ob_s], axis=0)
            mix = _hgrn_gate(o_f, o_b, proj, hgrn_o_norm[j])
            w_o = hgrn_wo[j]
            hg_list.append(st_p)
        else:
            Hq, Hk = GQA_HEADS, GQA_KV_HEADS
            qkv = _mm(h, gqa_w_qkv[j], out_dtype=F32)
            gains = jnp.concatenate([jnp.tile(gqa_q_norm[j][None], (Hq, 1)),
                                     jnp.tile(gqa_k_norm[j][None], (Hk, 1))], axis=0).reshape(Hq + Hk, 1, LANE)
            k_plain = _head_rope(qkv, gains[Hq:], cos_gqa, sin_gqa, col_block0=Hq, n_heads=Hk, norm=True,
                                 half=GQA_HD // 4, out_dtype=F32, n_rows=NP)
            qk = _head_rope(qkv, gains, cos_gqa, sin_gqa, col_block0=0, n_heads=Hq + Hk, norm=True,
                            half=GQA_HD // 4, out_dtype=BF16)
            kw = Hk * LANE
            k_new = qk[:, Hq * LANE:]
            v_new = qkv[:, (Hq + Hk) * LANE:].astype(BF16)
            k_s = jnp.concatenate([k_new[NP:].reshape(Bs, Ts, kw),
                                   cache_gqa_k[:, j].reshape(Bs, P, kw).astype(BF16)], axis=1)
            v_s = jnp.concatenate([v_new[NP:].reshape(Bs, Ts, kw),
                                   cache_gqa_v[:, j].reshape(Bs, P, kw).astype(BF16)], axis=1)
            k_all = jnp.concatenate([k_new[:NP], k_s.reshape(Bs * (Ts + P), kw)], axis=0)
            v_all = jnp.concatenate([v_new[:NP], v_s.reshape(Bs * (Ts + P), kw)], axis=0)
            scale = GQA_HD ** -0.5
            qp = [(qk, 0, True)]
            kp = [(k_all, 0, True)]
            vp = (v_all, 0, True)
            o_p = _attention(qp, kp, vp, B=Bp, T=Tp, S=Tp, H=Hq, group=Hq // Hk, q_row0=0, k_row0=0,
                             tq=256, tk=256, scale=scale)
            o_s = _attention(qp, kp, vp, B=Bs, T=Ts, S=Ts + P, H=Hq, group=Hq // Hk, q_row0=NP,
                             k_row0=NP, tq=1024, tk=512, scale=scale)
            mix = jnp.concatenate([o_p, o_s], axis=0)
            w_o = gqa_wo[j]
            gk_list.append(k_plain[:NP].reshape(Bp, Tp, Hk, GQA_HD))
            gv_list.append(qkv[:NP, (Hq + Hk) * LANE:].reshape(Bp, Tp, Hk, GQA_HD))

        y = _mm_resid(mix, w_o, y, nw[1], g1, **rows)

        fi = layer // 2
        if layer % 2 == 0:
            y = _ffn(y, nw[2], sh2, sc2, ffn_w1[fi], ffn_w3[fi], ffn_w2[fi], nw[3], g2, **rows)
        else:
            h2, sel, gw = _adaln_in(y, nw[2], sh2, sc2, tm=tm_tok, router=moe_router[fi], **rows)
            tm_e = _tile(M, 512)
            src_tok, gate_rows, tile_e, tile_v, slot_rows = _dispatch_tables(sel[:, :2], gw[:, :2], tm_e)
            x_sorted = _gather_rows(h2, src_tok, D // LANE)
            y_sorted = _moe_ffn(x_sorted, gate_rows, tile_e, tile_v, moe_w1[fi], moe_w3[fi], moe_w2[fi],
                                tm=tm_e)
            slots = _gather_rows(y_sorted, slot_rows, D // LANE)
            y = _moe_combine(slots, y, nw[3], g2, **rows)

    return (y[:NP].reshape(Bp, Tp, D), y[NP:].reshape(Bs, Ts, D),
            jnp.stack(ckv_list, axis=1), jnp.stack(kpe_list, axis=1), jnp.stack(hg_list, axis=1),
            jnp.stack(gk_list, axis=1), jnp.stack(gv_list, axis=1))
```

```python
import functools
import math

import numpy as np
import jax
import jax.numpy as jnp
from jax import lax
from jax.experimental import pallas as pl
from jax.experimental.pallas import tpu as pltpu

F32 = jnp.float32
BF16 = jnp.bfloat16
EPS = 1e-6
LANE = 128

GRID_W = 64
ROPE_THETA = 10000.0
MLA_HEADS, MLA_NOPE, MLA_ROPE, MLA_V = 16, 128, 64, 128
HG_HEADS = 16
GQA_HEADS, GQA_KV_HEADS, GQA_HD = 16, 4, 128
N_EXPERTS = 8

VMEM_LIMIT_BYTES = 56 * 1024 * 1024

SDS = jax.ShapeDtypeStruct


def _cparams(*sem):
    return pltpu.CompilerParams(dimension_semantics=sem, vmem_limit_bytes=VMEM_LIMIT_BYTES)


def _tile(n, pref, step=8):
    t = min(pref, n)
    while n % t:
        t -= step
    return t


def _dot(a, b):
    return jnp.dot(a, b, preferred_element_type=F32)


def _dot_nt(a, b):
    return lax.dot_general(a, b, (((1,), (1,)), ((), ())), preferred_element_type=F32)


def _dot_tn(a, b):
    return lax.dot_general(a, b, (((0,), (0,)), ((), ())), preferred_element_type=F32)


def _rms(x):
    return x * lax.rsqrt(jnp.mean(x * x, axis=-1, keepdims=True) + EPS)


def _silu(x):
    return x * jax.nn.sigmoid(x)


def _row_tile(n_prompt, t_sample, pref):
    return _tile(math.gcd(n_prompt, t_sample), pref)


def _group_of(i, tm, n_prompt, t_sample):
    r = i * tm
    return jnp.where(r < n_prompt, 0, 1 + (r - n_prompt) // t_sample)


def _mod_kernel(c_ref, w_ref, b_ref, o_ref):
    s = _silu(c_ref[...]).astype(BF16)
    o_ref[0] = _dot(s, w_ref[0].astype(BF16)) + b_ref[0]


def _modulation(cond8, ada_w, ada_b):
    L, D, N6 = ada_w.shape
    tn = _tile(N6, 1024)
    return pl.pallas_call(
        _mod_kernel,
        out_shape=SDS((L, 8, N6), F32),
        grid=(L, N6 // tn),
        in_specs=[
            pl.BlockSpec((8, D), lambda l, j: (0, 0)),
            pl.BlockSpec((1, D, tn), lambda l, j: (l, 0, j)),
            pl.BlockSpec((1, 1, tn), lambda l, j: (l, 0, j)),
        ],
        out_specs=pl.BlockSpec((1, 8, tn), lambda l, j: (l, 0, j)),
        compiler_params=_cparams("parallel", "parallel"),
        name="modulation",
    )(cond8, ada_w, ada_b.reshape(L, 1, N6))


def _adaln_in_kernel(y_ref, g_ref, sh_ref, sc_ref, h_ref):
    h = (_rms(y_ref[...]) * g_ref[...]) * (1.0 + sc_ref[0]) + sh_ref[0]
    h_ref[...] = h.astype(h_ref.dtype)


def _to_slab(ref, x):
    rows, D = x.shape
    S = D // LANE
    for s in range(S):
        ref[pl.ds(s, rows, stride=S), :] = x[:, s * LANE:(s + 1) * LANE]


def _from_slab(ref, rows, s, S):
    return ref[pl.ds(s, rows, stride=S), :]


def _adaln_route_kernel(y_ref, g_ref, sh_ref, sc_ref, r_ref, h_ref, sel_ref, gw_ref):
    h = (_rms(y_ref[...]) * g_ref[...]) * (1.0 + sc_ref[0]) + sh_ref[0]
    _to_slab(h_ref, h)
    logits = lax.dot_general(h, r_ref[...], (((1,), (0,)), ((), ())),
                             precision=lax.Precision.HIGHEST, preferred_element_type=F32)
    lane = lax.broadcasted_iota(jnp.int32, logits.shape, 1)
    neg = jnp.float32(-jnp.inf)
    logits = jnp.where(lane < N_EXPERTS, logits, neg)
    lane_f = lane.astype(F32)
    m1 = jnp.max(logits, axis=-1, keepdims=True)
    i1 = jnp.min(jnp.where(logits == m1, lane_f, float(LANE)), axis=-1, keepdims=True)
    rest = jnp.where(lane_f == i1, neg, logits)
    m2 = jnp.max(rest, axis=-1, keepdims=True)
    i2 = jnp.min(jnp.where(rest == m2, lane_f, float(LANE)), axis=-1, keepdims=True)
    e = jnp.exp(m2 - m1)
    w1 = 1.0 / (1.0 + e)
    w2 = e / (1.0 + e)
    i1, i2 = i1.astype(jnp.int32), i2.astype(jnp.int32)
    sel_ref[...] = jnp.where(lane == 0, i1, jnp.where(lane == 1, i2, 0))
    gw_ref[...] = jnp.where(lane == 0, w1, jnp.where(lane == 1, w2, 0.0))


def _adaln_in(y, gain, shift, scale, *, n_prompt, t_sample, tm, router=None):
    M, D = y.shape
    grp = functools.partial(_group_of, tm=tm, n_prompt=n_prompt, t_sample=t_sample)
    in_specs = [
        pl.BlockSpec((tm, D), lambda i: (i, 0)),
        pl.BlockSpec((1, D), lambda i: (0, 0)),
        pl.BlockSpec((1, 1, D), lambda i: (grp(i), 0, 0)),
        pl.BlockSpec((1, 1, D), lambda i: (grp(i), 0, 0)),
    ]
    args = [y, gain.reshape(1, D), shift, scale]
    if router is None:
        return pl.pallas_call(
            _adaln_in_kernel,
            out_shape=SDS((M, D), BF16),
            grid=(M // tm,),
            in_specs=in_specs,
            out_specs=pl.BlockSpec((tm, D), lambda i: (i, 0)),
            compiler_params=_cparams("parallel"),
            name="adaln_in",
        )(*args)
    r_pad = jnp.pad(router, ((0, 0), (0, LANE - router.shape[1])))
    return pl.pallas_call(
        _adaln_route_kernel,
        out_shape=(SDS((M * (D // LANE), LANE), F32), SDS((M, LANE), jnp.int32), SDS((M, LANE), F32)),
        grid=(M // tm,),
        in_specs=in_specs + [pl.BlockSpec((D, LANE), lambda i: (0, 0))],
        out_specs=(
            pl.BlockSpec((tm * (D // LANE), LANE), lambda i: (i, 0)),
            pl.BlockSpec((tm, LANE), lambda i: (i, 0)),
            pl.BlockSpec((tm, LANE), lambda i: (i, 0)),
        ),
        compiler_params=_cparams("parallel"),
        name="adaln_route",
    )(*args, r_pad)


def _mm_kernel(x_ref, w_ref, o_ref, wbf):
    @pl.when(pl.program_id(1) == 0)
    def _():
        wbf[...] = w_ref[...].astype(BF16)

    o_ref[...] = _dot(x_ref[...].astype(BF16), wbf[...]).astype(o_ref.dtype)


def _mm(x, w, *, out_dtype, tm=1024, tn=1024):
    M, K = x.shape
    N = w.shape[1]
    tm = _tile(M, tm)
    tn = _tile(N, tn, LANE)
    return pl.pallas_call(
        _mm_kernel,
        out_shape=SDS((M, N), out_dtype),
        grid=(N // tn, M // tm),
        in_specs=[
            pl.BlockSpec((tm, K), lambda j, i: (i, 0)),
            pl.BlockSpec((K, tn), lambda j, i: (0, j)),
        ],
        out_specs=pl.BlockSpec((tm, tn), lambda j, i: (i, j)),
        scratch_shapes=[pltpu.VMEM((K, tn), BF16)],
        compiler_params=_cparams("parallel", "arbitrary"),
        name="mm",
    )(x, w)


def _mm_resid_kernel(xp_ref, xs_ref, w_ref, y_ref, g_ref, gate_ref, o_ref, wbf, *, n_ptiles):
    i = pl.program_id(0)

    @pl.when(i == 0)
    def _():
        wbf[...] = w_ref[...].astype(BF16)

    x = jnp.where(i < n_ptiles, xp_ref[...], xs_ref[...])
    a = _dot(x.astype(BF16), wbf[...])
    o_ref[...] = y_ref[...] + gate_ref[0] * (_rms(a) * g_ref[...])


def _mm_resid(x_p, x_s, w, y, gain, gate, *, n_prompt, t_sample, tm=256):
    M, D = y.shape
    K = w.shape[0]
    tm = _row_tile(n_prompt, t_sample, tm)
    n_ptiles = n_prompt // tm
    grp = functools.partial(_group_of, tm=tm, n_prompt=n_prompt, t_sample=t_sample)
    return pl.pallas_call(
        functools.partial(_mm_resid_kernel, n_ptiles=n_ptiles),
        out_shape=SDS((M, D), F32),
        grid=(M // tm,),
        in_specs=[
            pl.BlockSpec((tm, K), lambda i: (jnp.minimum(i, n_ptiles - 1), 0)),
            pl.BlockSpec((tm, K), lambda i: (jnp.maximum(i - n_ptiles, 0), 0)),
            pl.BlockSpec((K, D), lambda i: (0, 0), pipeline_mode=pl.Buffered(1)),
            pl.BlockSpec((tm, D), lambda i: (i, 0)),
            pl.BlockSpec((1, D), lambda i: (0, 0)),
            pl.BlockSpec((1, 1, D), lambda i: (grp(i), 0, 0)),
        ],
        out_specs=pl.BlockSpec((tm, D), lambda i: (i, 0)),
        scratch_shapes=[pltpu.VMEM((K, D), BF16)],
        compiler_params=_cparams("arbitrary"),
        name="mm_resid",
    )(x_p, x_s, w, y, gain.reshape(1, D), gate)


def _swiglu_chunk(h, wb1, wb3, wb2):
    a = _dot(h, wb1[...])
    b = _dot(h, wb3[...])
    return _dot((_silu(a) * b).astype(BF16), wb2[...])


def _ffn_kernel(y_ref, gin_ref, sh_ref, sc_ref, w1_ref, w3_ref, w2_ref, gout_ref, gate_ref,
                o_ref, hbf, wb1, wb3, wb2, *, nf, tm, sub):
    f = pl.program_id(1)
    blocks = [pl.ds(r, sub) for r in range(0, tm, sub)]

    @pl.when(f == 0)
    def _():
        for rows in blocks:
            h = (_rms(y_ref[rows, :]) * gin_ref[...]) * (1.0 + sc_ref[0]) + sh_ref[0]
            hbf[rows, :] = h.astype(BF16)

    wb1[...] = w1_ref[...].astype(BF16)
    wb3[...] = w3_ref[...].astype(BF16)
    wb2[...] = w2_ref[...].astype(BF16)
    for rows in blocks:
        p = _swiglu_chunk(hbf[rows, :], wb1, wb3, wb2)

        @pl.when(f == 0)
        def _():
            o_ref[rows, :] = p

        @pl.when(f > 0)
        def _():
            o_ref[rows, :] += p

    @pl.when(f == nf - 1)
    def _():
        for rows in blocks:
            o_ref[rows, :] = y_ref[rows, :] + gate_ref[0] * (_rms(o_ref[rows, :]) * gout_ref[...])


def _ffn(y, gin, shift, scale, w1, w3, w2, gout, gate, *, n_prompt, t_sample, tm=1024, tf=256, sub=256):
    M, D = y.shape
    F = w1.shape[1]
    tm = _row_tile(n_prompt, t_sample, tm)
    sub = _tile(tm, sub)
    tf = _tile(F, tf, LANE)
    nf = F // tf
    grp = functools.partial(_group_of, tm=tm, n_prompt=n_prompt, t_sample=t_sample)
    vec = pl.BlockSpec((1, D), lambda i, f: (0, 0))
    gvec = pl.BlockSpec((1, 1, D), lambda i, f: (grp(i), 0, 0))
    return pl.pallas_call(
        functools.partial(_ffn_kernel, nf=nf, tm=tm, sub=sub),
        out_shape=SDS((M, D), F32),
        grid=(M // tm, nf),
        in_specs=[
            pl.BlockSpec((tm, D), lambda i, f: (i, 0), pipeline_mode=pl.Buffered(1)),
            vec, gvec, gvec,
            pl.BlockSpec((D, tf), lambda i, f: (0, f)),
            pl.BlockSpec((D, tf), lambda i, f: (0, f)),
            pl.BlockSpec((tf, D), lambda i, f: (f, 0)),
            vec, gvec,
        ],
        out_specs=pl.BlockSpec((tm, D), lambda i, f: (i, 0)),
        scratch_shapes=[pltpu.VMEM((tm, D), BF16), pltpu.VMEM((D, tf), BF16), pltpu.VMEM((D, tf), BF16),
                        pltpu.VMEM((tf, D), BF16)],
        compiler_params=_cparams("parallel", "arbitrary"),
        name="ffn_dense",
    )(y, gin.reshape(1, D), shift, scale, w1, w3, w2, gout.reshape(1, D), gate)


def _moe_kernel(te_ref, tr_ref, src_ref, dst_ref, h_hbm, gate_ref, w1_ref, w3_ref, w2_ref, out_hbm,
                xs, xbf, acc, wb1, wb3, wb2, gsem, ssem, *, nf, tm, sub, S):
    i = pl.program_id(0)
    f = pl.program_id(1)
    nrows = tr_ref[i]
    base = i * tm
    blocks = [(r, pl.ds(r, sub)) for r in range(0, tm, sub)]

    @pl.when(nrows > 0)
    def _():
        @pl.when(f == 0)
        def _():
            def issue(r, c):
                s0 = pl.multiple_of(src_ref[base + r] * S, S)
                pltpu.make_async_copy(h_hbm.at[pl.ds(s0, S)], xs.at[pl.ds(pl.multiple_of(r * S, S), S)],
                                      gsem).start()
                return c

            lax.fori_loop(0, tm, issue, 0)
            pltpu.make_async_copy(h_hbm.at[pl.ds(0, tm * S)], xs, gsem).wait()
            for s in range(S):
                xbf[:, s * LANE:(s + 1) * LANE] = _from_slab(xs, tm, s, S).astype(BF16)

        wb1[...] = w1_ref[0].astype(BF16)
        wb3[...] = w3_ref[0].astype(BF16)
        wb2[...] = w2_ref[0].astype(BF16)
        for r0, rows in blocks:
            @pl.when(nrows > r0)
            def _():
                p = _swiglu_chunk(xbf[rows, :], wb1, wb3, wb2)

                @pl.when(f == 0)
                def _():
                    acc[rows, :] = p

                @pl.when(f > 0)
                def _():
                    acc[rows, :] += p

            @pl.when(jnp.logical_and(nrows <= r0, f == 0))
            def _():
                acc[rows, :] = jnp.zeros((sub, acc.shape[1]), F32)

        @pl.when(f == nf - 1)
        def _():
            _to_slab(xs, acc[...] * gate_ref[...])

            def row_copy(r, d0):
                return pltpu.make_async_copy(xs.at[pl.ds(pl.multiple_of(r * S, S), S)],
                                             out_hbm.at[pl.ds(d0, S)], ssem)

            def issue(r, c):
                row_copy(r, pl.multiple_of(dst_ref[base + r] * S, S)).start()
                return c

            def drain(r, c):
                row_copy(r, 0).wait()
                return c

            lax.fori_loop(0, nrows, issue, 0)
            lax.fori_loop(0, nrows, drain, 0)


def _moe_ffn(h_slab, tables, w1, w3, w2, *, n_tok, tm, tf=256, sub=256):
    tile_e, tile_rows, src_tok, dst_row, gate_rows = tables
    D = w1.shape[1]
    S = D // LANE
    F = w1.shape[2]
    tf = _tile(F, tf, LANE)
    nf = F // tf
    nt = tile_e.shape[0]

    def fidx(i, f, tr):
        return jnp.where(tr[i] > 0, f, nf - 1)

    grid_spec = pltpu.PrefetchScalarGridSpec(
        num_scalar_prefetch=4,
        grid=(nt, nf),
        in_specs=[
            pl.BlockSpec(memory_space=pl.ANY),
            pl.BlockSpec((tm, 1), lambda i, f, te, tr, sr, ds: (i, 0)),
            pl.BlockSpec((1, D, tf), lambda i, f, te, tr, sr, ds: (te[i], 0, fidx(i, f, tr))),
            pl.BlockSpec((1, D, tf), lambda i, f, te, tr, sr, ds: (te[i], 0, fidx(i, f, tr))),
            pl.BlockSpec((1, tf, D), lambda i, f, te, tr, sr, ds: (te[i], fidx(i, f, tr), 0)),
        ],
        out_specs=pl.BlockSpec(memory_space=pl.ANY),
        scratch_shapes=[pltpu.VMEM((tm * S, LANE), F32), pltpu.VMEM((tm, D), BF16), pltpu.VMEM((tm, D), F32),
                        pltpu.VMEM((D, tf), BF16), pltpu.VMEM((D, tf), BF16), pltpu.VMEM((tf, D), BF16),
                        pltpu.SemaphoreType.DMA(()), pltpu.SemaphoreType.DMA(())],
    )
    return pl.pallas_call(
        functools.partial(_moe_kernel, nf=nf, tm=tm, sub=_tile(tm, sub), S=S),
        out_shape=SDS((2 * n_tok * S, LANE), F32),
        grid_spec=grid_spec,
        compiler_params=pltpu.CompilerParams(dimension_semantics=("arbitrary", "arbitrary"),
                                             vmem_limit_bytes=VMEM_LIMIT_BYTES, has_side_effects=True),
        name="moe_ffn",
    )(tile_e, tile_rows, src_tok, dst_row, h_slab, gate_rows, w1, w3, w2)


def _combine_kernel(a_ref, b_ref, y_ref, g_ref, gate_ref, o_ref, *, tm, S):
    parts = [_from_slab(a_ref, tm, s, S) + _from_slab(b_ref, tm, s, S) for s in range(S)]
    ss = parts[0] * parts[0]
    for t in parts[1:]:
        ss = ss + t * t
    r = lax.rsqrt(jnp.sum(ss, axis=-1, keepdims=True) / (S * LANE) + EPS)
    for s, t in enumerate(parts):
        c = slice(s * LANE, (s + 1) * LANE)
        o_ref[:, c] = y_ref[:, c] + gate_ref[0][:, c] * (t * r * g_ref[:, c])


def _moe_combine(slots, y, gain, gate, *, n_prompt, t_sample, tm=256):
    M, D = y.shape
    S = D // LANE
    tm = _row_tile(n_prompt, t_sample, tm)
    nb = M // tm
    grp = functools.partial(_group_of, tm=tm, n_prompt=n_prompt, t_sample=t_sample)
    return pl.pallas_call(
        functools.partial(_combine_kernel, tm=tm, S=S),
        out_shape=SDS((M, D), F32),
        grid=(nb,),
        in_specs=[
            pl.BlockSpec((tm * S, LANE), lambda i: (i, 0)),
            pl.BlockSpec((tm * S, LANE), lambda i: (nb + i, 0)),
            pl.BlockSpec((tm, D), lambda i: (i, 0)),
            pl.BlockSpec((1, D), lambda i: (0, 0)),
            pl.BlockSpec((1, 1, D), lambda i: (grp(i), 0, 0)),
        ],
        out_specs=pl.BlockSpec((tm, D), lambda i: (i, 0)),
        compiler_params=_cparams("parallel"),
        name="moe_combine",
    )(slots, slots, y, gain.reshape(1, D), gate)


def _dispatch_tables(sel, gw, tm):
    M = sel.shape[0]
    E = N_EXPERTS
    i32 = jnp.int32
    e_flat = sel.reshape(-1)
    pair = jnp.arange(2 * M, dtype=i32)
    onehot = (e_flat[:, None] == jnp.arange(E, dtype=i32)[None, :]).astype(i32)
    running = jnp.cumsum(onehot, axis=0)
    rank = jnp.sum(running * onehot, axis=1) - 1
    counts = running[-1]
    padded = ((counts + tm - 1) // tm) * tm
    ends_p = jnp.cumsum(padded)
    starts_p = ends_p - padded
    pos = starts_p[e_flat] + rank
    R = 2 * M + E * tm
    src_tok = jnp.zeros((R,), i32).at[pos].set(pair // 2)
    dst_row = jnp.zeros((R,), i32).at[pos].set((pair % 2) * M + pair // 2)
    gate_rows = jnp.zeros((R,), F32).at[pos].set(gw.reshape(-1))
    tile_start = jnp.arange(R // tm, dtype=i32) * tm
    tile_e = jnp.minimum(jnp.sum(ends_p[None, :] <= tile_start[:, None], axis=1), E - 1).astype(i32)
    tile_rows = jnp.clip(starts_p[tile_e] + counts[tile_e] - tile_start, 0, tm).astype(i32)
    return tile_e, tile_rows, src_tok, dst_row, gate_rows.reshape(R, 1)


def _attn_kernel(*refs, n_parts, k_modes, v_mode, hb, scale, nk):
    q_refs = refs[:n_parts]
    k_refs = refs[n_parts:2 * n_parts]
    v_ref = refs[2 * n_parts]
    o_ref = refs[2 * n_parts + 1]
    m_sc, l_sc, acc_sc = refs[2 * n_parts + 2:]
    j = pl.program_id(3)

    @pl.when(j == 0)
    def _():
        m_sc[...] = jnp.full_like(m_sc, -jnp.inf)
        l_sc[...] = jnp.zeros_like(l_sc)
        acc_sc[...] = jnp.zeros_like(acc_sc)

    def block(ref, mode, hh):
        x = ref[:, hh * LANE:(hh + 1) * LANE] if mode == "head" else ref[...]
        return x.astype(BF16)

    for hh in range(hb):
        q = [block(q_refs[p], "head", hh) for p in range(n_parts)]
        k = [block(k_refs[p], k_modes[p], hh) for p in range(n_parts)]
        q = q[0] if n_parts == 1 else jnp.concatenate(q, axis=1)
        k = k[0] if n_parts == 1 else jnp.concatenate(k, axis=1)
        s = _dot_nt(q, k) * scale
        m_prev = m_sc[hh]
        m_new = jnp.maximum(m_prev, jnp.max(s, axis=-1, keepdims=True))
        alpha = jnp.exp(m_prev - m_new)
        p = jnp.exp(s - m_new)
        l_sc[hh] = alpha * l_sc[hh] + jnp.sum(p, axis=-1, keepdims=True)
        acc_sc[hh] = alpha * acc_sc[hh] + _dot(p.astype(BF16), block(v_ref, v_mode, hh))
        m_sc[hh] = m_new

    @pl.when(j == nk - 1)
    def _():
        for hh in range(hb):
            o_ref[:, hh * LANE:(hh + 1) * LANE] = (acc_sc[hh] / l_sc[hh]).astype(o_ref.dtype)


def _attention(q_parts, k_parts, v_part, *, B, T, S, H, hb, q_row0, k_row0, tq, tk, scale):
    tq = _tile(T, tq)
    tk = _tile(S, tk)
    nq, nk = T // tq, S // tk
    qb0, kb0 = q_row0 // tq, k_row0 // tk

    def q_spec(off):
        return pl.BlockSpec((tq, hb * LANE), lambda b, g, i, j: (qb0 + b * nq + i, off // hb + g))

    def k_spec(off, mode):
        if mode == "head":
            return pl.BlockSpec((tk, hb * LANE), lambda b, g, i, j: (kb0 + b * nk + j, off // hb + g))
        return pl.BlockSpec((tk, LANE), lambda b, g, i, j: (kb0 + b * nk + j, off + (g if mode == "group" else 0)))

    in_specs = [q_spec(off) for (_, off) in q_parts]
    in_specs += [k_spec(off, mode) for (_, off, mode) in k_parts]
    in_specs += [k_spec(v_part[1], v_part[2])]
    args = [a for (a, _) in q_parts] + [a for (a, _, _) in k_parts] + [v_part[0]]
    return pl.pallas_call(
        functools.partial(_attn_kernel, n_parts=len(q_parts), k_modes=[m for (_, _, m) in k_parts],
                          v_mode=v_part[2], hb=hb, scale=scale, nk=nk),
        out_shape=SDS((B * T, H * LANE), BF16),
        grid=(B, H // hb, nq, nk),
        in_specs=in_specs,
        out_specs=pl.BlockSpec((tq, hb * LANE), lambda b, g, i, j: (b * nq + i, g)),
        scratch_shapes=[pltpu.VMEM((hb, tq, 1), F32), pltpu.VMEM((hb, tq, 1), F32),
                        pltpu.VMEM((hb, tq, LANE), F32)],
        compiler_params=_cparams("parallel", "parallel", "parallel", "arbitrary"),
        name="attention",
    )(*args)


def _rmsnorm_cols_kernel(x_ref, g_ref, o_ref):
    o_ref[...] = (_rms(x_ref[...]) * g_ref[...]).astype(o_ref.dtype)


def _rmsnorm_cols(x, gain, *, col_block, width, out_dtype, tm=1024):
    M = x.shape[0]
    tm = _tile(M, tm)
    return pl.pallas_call(
        _rmsnorm_cols_kernel,
        out_shape=SDS((M, width), out_dtype),
        grid=(M // tm,),
        in_specs=[pl.BlockSpec((tm, width), lambda i: (i, col_block)),
                  pl.BlockSpec((1, width), lambda i: (0, 0))],
        out_specs=pl.BlockSpec((tm, width), lambda i: (i, 0)),
        compiler_params=_cparams("parallel"),
        name="rmsnorm_cols",
    )(x, gain.reshape(1, width))


def _head_rope_kernel(x_ref, g_ref, cos_ref, sin_ref, o_ref, *, norm, half):
    x = x_ref[...]
    if norm:
        x = _rms(x) * g_ref[0]
    lane = lax.broadcasted_iota(jnp.int32, x.shape, 1)
    first = (lane % (2 * half)) < half
    rot = jnp.where(first, -pltpu.roll(x, LANE - half, 1), pltpu.roll(x, half, 1))
    o_ref[...] = (x * cos_ref[...] + rot * sin_ref[...]).astype(o_ref.dtype)


def _head_rope(x, gains, cos, sin, *, col_block0, n_heads, norm, half, out_dtype, n_rows=None, tm=1024):
    M = x.shape[0] if n_rows is None else n_rows
    tm = _tile(M, tm)
    return pl.pallas_call(
        functools.partial(_head_rope_kernel, norm=norm, half=half),
        out_shape=SDS((M, n_heads * LANE), out_dtype),
        grid=(M // tm, n_heads),
        in_specs=[
            pl.BlockSpec((tm, LANE), lambda i, h: (i, col_block0 + h)),
            pl.BlockSpec((1, 1, LANE), lambda i, h: (h, 0, 0)),
            pl.BlockSpec((tm, LANE), lambda i, h: (i, 0)),
            pl.BlockSpec((tm, LANE), lambda i, h: (i, 0)),
        ],
        out_specs=pl.BlockSpec((tm, LANE), lambda i, h: (i, h)),
        compiler_params=_cparams("parallel", "parallel"),
        name="head_rope",
    )(x, gains, cos, sin)


def _rope_tables(n_prompt, dec_batch, t_sample, half):
    tok = jnp.arange(t_sample, dtype=jnp.int32)
    row, col = tok // GRID_W, tok % GRID_W
    inv = ROPE_THETA ** (-jnp.arange(half, dtype=F32) / half)

    def cs(pos):
        ang = pos.astype(F32)[:, None] * inv[None, :]
        return (jnp.concatenate([jnp.cos(ang), jnp.cos(ang)], -1),
                jnp.concatenate([jnp.sin(ang), jnp.sin(ang)], -1))

    cr, sr = cs(row)
    cc, sc = cs(col)
    pad = LANE - 4 * half
    cos = jnp.concatenate([cr, cc, jnp.ones((t_sample, pad), F32)], -1)
    sin = jnp.concatenate([sr, sc, jnp.zeros((t_sample, pad), F32)], -1)
    cos = jnp.concatenate([jnp.ones((n_prompt, LANE), F32), jnp.tile(cos, (dec_batch, 1))], 0)
    sin = jnp.concatenate([jnp.zeros((n_prompt, LANE), F32), jnp.tile(sin, (dec_batch, 1))], 0)
    return cos, sin


def _hgrn_constants(C, reverse):
    L = int(np.log2(C))
    t = np.arange(C)[:, None]
    u = np.arange(C)[None, :]
    Ws, Ms = [], []
    for lvl in range(L):
        b = 1 << lvl
        grp = t // (2 * b)
        if not reverse:
            bnd = grp * 2 * b + b - 1
            qside = (t % (2 * b)) >= b
            W = np.where(qside, (u > bnd) & (u <= t), (u > t) & (u <= bnd))
            kside_s = (u % (2 * b)) < b
        else:
            bnd = grp * 2 * b + b
            qside = (t % (2 * b)) < b
            W = np.where(qside, (u >= t) & (u < bnd), (u >= bnd) & (u < t))
            kside_s = (u % (2 * b)) >= b
        Ws.append(W)
        Ms.append(qside & kside_s & (grp == u // (2 * b)))
    if not reverse:
        Ws += [u <= t, u > t]
    else:
        Ws += [u >= t, u < t]
    Ms.append(t == u)
    W_all = jnp.asarray(np.concatenate(Ws, 0).astype(np.float32), BF16)
    M_all = jnp.asarray(np.stack(Ms).astype(np.float32), F32)
    return W_all, M_all


def _hgrn_chain(q, v, z, lb, W_all, m_ref, st, *, C, L, edge_row):
    f = lb + (1.0 - lb) * jax.nn.sigmoid(z)
    k = 1.0 - f
    lf = jnp.log(f)
    hi = lf.astype(BF16)
    r1 = lf - hi.astype(F32)
    mid = r1.astype(BF16)
    lo = (r1 - mid.astype(F32)).astype(BF16)
    d3 = _dot(W_all, jnp.concatenate([hi, mid, lo], axis=1))
    dn = d3[:, :LANE] + d3[:, LANE:2 * LANE] + d3[:, 2 * LANE:]
    e_all = jnp.exp(dn)
    qb = q.astype(BF16)
    a = m_ref[L] * _dot_nt(qb, k.astype(BF16))
    for lvl in range(L):
        e = e_all[lvl * C:(lvl + 1) * C]
        a = a + m_ref[lvl] * _dot_nt((q * e).astype(BF16), (k * e).astype(BF16))
    eq = e_all[L * C:(L + 1) * C]
    ek = e_all[(L + 1) * C:(L + 2) * C]
    vb = v.astype(BF16)
    o = _dot(a.astype(BF16), vb) + _dot_nt((q * eq).astype(BF16), st.astype(BF16))
    g_edge = dn[L * C + edge_row:L * C + edge_row + 1]
    st_new = st * jnp.exp(g_edge) + _dot_tn(vb, (k * ek).astype(BF16))
    return o, st_new


def _hgrn_kernel(*refs, C, L, nc, hb, has_init):
    (qf_ref, vf_ref, zf_ref, qb_ref, vb_ref, zb_ref, lbf_ref, lbb_ref,
     wf_ref, mf_ref, wb_ref, mb_ref) = refs[:12]
    rest = refs[12:]
    if has_init:
        s0_ref, rest = rest[0], rest[1:]
    of_ref, ob_ref, sfin_ref, stf, stb = rest
    c = pl.program_id(2)

    @pl.when(c == 0)
    def _():
        for hh in range(hb):
            if has_init:
                stf[hh] = s0_ref[0, 0, hh]
                stb[hh] = s0_ref[0, 1, hh]
            else:
                stf[hh] = jnp.zeros((LANE, LANE), F32)
                stb[hh] = jnp.zeros((LANE, LANE), F32)

    for hh in range(hb):
        cols = slice(hh * LANE, (hh + 1) * LANE)
        o, s_new = _hgrn_chain(qf_ref[:, cols], vf_ref[:, cols], zf_ref[:, cols], lbf_ref[0][:, cols],
                               wf_ref[...], mf_ref, stf[hh], C=C, L=L, edge_row=C - 1)
        of_ref[:, cols] = o
        stf[hh] = s_new
        o, s_new = _hgrn_chain(qb_ref[:, cols], vb_ref[:, cols], zb_ref[:, cols], lbb_ref[0][:, cols],
                               wb_ref[...], mb_ref, stb[hh], C=C, L=L, edge_row=0)
        ob_ref[:, cols] = o
        stb[hh] = s_new

    @pl.when(c == nc - 1)
    def _():
        for hh in range(hb):
            sfin_ref[0, 0, hh] = stf[hh].T
            sfin_ref[0, 1, hh] = stb[hh].T


def _hgrn_scan(proj, lb, s0t, *, B, T, row0, C=128, hb=2):
    D = proj.shape[1] // 5
    H = D // LANE
    C = _tile(T, C)
    L = int(np.log2(C))
    assert 1 << L == C
    nc = T // C
    rb0 = row0 // C
    HB = H // hb
    W = hb * LANE
    wf, mf = _hgrn_constants(C, False)
    wb, mb = _hgrn_constants(C, True)

    def fwd(col0):
        return pl.BlockSpec((C, W), lambda b, g, c: (rb0 + b * nc + c, col0 // hb + g))

    def bwd(col0):
        return pl.BlockSpec((C, W), lambda b, g, c: (rb0 + b * nc + nc - 1 - c, col0 // hb + g))

    def const(a):
        return pl.BlockSpec(a.shape, lambda b, g, c: (0,) * a.ndim)

    in_specs = [fwd(0), fwd(H), fwd(3 * H), bwd(0), bwd(H), bwd(4 * H),
                pl.BlockSpec((1, 1, W), lambda b, g, c: (0, 0, g)),
                pl.BlockSpec((1, 1, W), lambda b, g, c: (1, 0, g)),
                const(wf), const(mf), const(wb), const(mb)]
    args = [proj, proj, proj, proj, proj, proj, lb, lb, wf, mf, wb, mb]
    if s0t is not None:
        in_specs.append(pl.BlockSpec((1, 2, hb, LANE, LANE), lambda b, g, c: (b, 0, g, 0, 0)))
        args.append(s0t)
    return pl.pallas_call(
        functools.partial(_hgrn_kernel, C=C, L=L, nc=nc, hb=hb, has_init=s0t is not None),
        out_shape=(SDS((B * T, D), F32), SDS((B * T, D), F32), SDS((B, 2, H, LANE, LANE), F32)),
        grid=(B, HB, nc),
        in_specs=in_specs,
        out_specs=(
            pl.BlockSpec((C, W), lambda b, g, c: (b * nc + c, g)),
            pl.BlockSpec((C, W), lambda b, g, c: (b * nc + nc - 1 - c, g)),
            pl.BlockSpec((1, 2, hb, LANE, LANE), lambda b, g, c: (b, 0, g, 0, 0)),
        ),
        scratch_shapes=[pltpu.VMEM((hb, LANE, LANE), F32), pltpu.VMEM((hb, LANE, LANE), F32)],
        compiler_params=_cparams("parallel", "parallel", "arbitrary"),
        name="hgrn_scan",
    )(*args)


def _hgrn_gate_kernel(of_ref, ob_ref, g_ref, w_ref, o_ref):
    o = _rms(of_ref[...] + ob_ref[...]) * w_ref[0]
    o_ref[...] = (o * _silu(g_ref[...])).astype(o_ref.dtype)


def _hgrn_gate(o_f, o_b, proj, o_norm, *, row0, tm=1024):
    M, D = o_f.shape
    H = D // LANE
    tm = _tile(math.gcd(M, row0) if row0 else M, tm)
    rb0 = row0 // tm
    blk = pl.BlockSpec((tm, LANE), lambda i, h: (i, h))
    return pl.pallas_call(
        _hgrn_gate_kernel,
        out_shape=SDS((M, D), BF16),
        grid=(M // tm, H),
        in_specs=[blk, blk,
                  pl.BlockSpec((tm, LANE), lambda i, h: (rb0 + i, 2 * H + h)),
                  pl.BlockSpec((1, 1, LANE), lambda i, h: (h, 0, 0))],
        out_specs=blk,
        compiler_params=_cparams("parallel", "parallel"),
        name="hgrn_gate",
    )(o_f, o_b, proj, o_norm.reshape(H, 1, LANE))


def kernel(x_prompt, x_sample, c, cache_mla_ckv, cache_mla_kpe, state_hgrn, cache_gqa_k, cache_gqa_v, c_ctx, ada_w, ada_b, norm_w, mla_wq_a, mla_q_norm, mla_wq_b, mla_wkv_a, mla_kv_norm, mla_wkv_b, mla_wo, hgrn_w_in, hgrn_lb_logits, hgrn_o_norm, hgrn_wo, gqa_w_qkv, gqa_q_norm, gqa_k_norm, gqa_wo, ffn_w1, ffn_w3, ffn_w2, moe_router, moe_w1, moe_w3, moe_w2):
    Bp, Tp, D = x_prompt.shape
    Bs, Ts, _ = x_sample.shape
    P = cache_mla_ckv.shape[2]
    depth = ada_w.shape[0]
    NP, NS = Bp * Tp, Bs * Ts
    M = NP + NS
    G = 1 + Bs
    assert D == HG_HEADS * LANE
    rows = dict(n_prompt=NP, t_sample=Ts)

    y = jnp.concatenate([x_prompt.reshape(NP, D), x_sample.reshape(NS, D)], axis=0)
    cond8 = jnp.zeros((8, D), F32).at[0].set(c_ctx).at[1:G].set(c)
    mod = _modulation(cond8, ada_w, ada_b).reshape(depth, 8, 6, D)
    mod = jnp.transpose(mod, (0, 2, 1, 3))[:, :, :G, None, :]

    tm_tok = _row_tile(NP, Ts, 512)
    cos_mla, sin_mla = _rope_tables(NP, Bs, Ts, MLA_ROPE // 4)
    cos_gqa, sin_gqa = _rope_tables(NP, Bs, Ts, GQA_HD // 4)
    ones_g = jnp.ones((MLA_HEADS, 1, LANE), F32)

    ckv_list, kpe_list, hg_list, gk_list, gv_list = [], [], [], [], []
    for layer in range(depth):
        kind, j = layer % 3, layer // 3
        nw = norm_w[layer]
        sh1, sc1, g1, sh2, sc2, g2 = (mod[layer, k] for k in range(6))
        h = _adaln_in(y, nw[0], sh1, sc1, tm=tm_tok, **rows)

        if kind == 0:
            Hm = MLA_HEADS
            w_a = jnp.concatenate([mla_wq_a[j], mla_wkv_a[j],
                                   jnp.zeros((D, LANE - MLA_ROPE), F32)], axis=1)
            a = _mm(h, w_a, out_dtype=F32)
            ql, kvl = mla_wq_a.shape[2], mla_kv_norm.shape[1]
            qn = _rmsnorm_cols(a, mla_q_norm[j], col_block=0, width=ql, out_dtype=BF16)
            ckv = _rmsnorm_cols(a, mla_kv_norm[j], col_block=ql // kvl, width=kvl, out_dtype=F32)
            wqb = mla_wq_b[j].reshape(ql, Hm, MLA_NOPE + MLA_ROPE)
            wqb = jnp.concatenate([
                wqb[:, :, :MLA_NOPE].reshape(ql, Hm * MLA_NOPE),
                jnp.pad(wqb[:, :, MLA_NOPE:], ((0, 0), (0, 0), (0, LANE - MLA_ROPE))).reshape(ql, Hm * LANE),
            ], axis=1)
            q = _mm(qn, wqb, out_dtype=F32)
            pe_blk = (ql + kvl) // LANE
            q_pe = _head_rope(q, ones_g, cos_mla, sin_mla, col_block0=Hm, n_heads=Hm, norm=False,
                              half=MLA_ROPE // 4, out_dtype=BF16)
            k_pe = _head_rope(a, ones_g, cos_mla, sin_mla, col_block0=pe_blk, n_heads=1, norm=False,
                              half=MLA_ROPE // 4, out_dtype=BF16)
            wkvb = mla_wkv_b[j].reshape(kvl, Hm, MLA_NOPE + MLA_V)
            wkvb = jnp.concatenate([wkvb[:, :, :MLA_NOPE].reshape(kvl, Hm * MLA_NOPE),
                                    wkvb[:, :, MLA_NOPE:].reshape(kvl, Hm * MLA_V)], axis=1)
            ckv_s = jnp.concatenate([ckv[NP:].reshape(Bs, Ts, kvl), cache_mla_ckv[:, j]], axis=1)
            c_all = jnp.concatenate([ckv[:NP], ckv_s.reshape(Bs * (Ts + P), kvl)], axis=0)
            kv = _mm(c_all, wkvb, out_dtype=BF16)
            kpe_ctx = jnp.pad(cache_mla_kpe[:, j], ((0, 0), (0, 0), (0, LANE - MLA_ROPE))).astype(BF16)
            kpe_s = jnp.concatenate([k_pe[NP:].reshape(Bs, Ts, LANE), kpe_ctx], axis=1)
            kpe_all = jnp.concatenate([k_pe[:NP], kpe_s.reshape(Bs * (Ts + P), LANE)], axis=0)
            scale = (MLA_NOPE + MLA_ROPE) ** -0.5
            qp = [(q, 0), (q_pe, 0)]
            kp = [(kv, 0, "head"), (kpe_all, 0, "shared")]
            vp = (kv, Hm, "head")
            mix_p = _attention(qp, kp, vp, B=Bp, T=Tp, S=Tp, H=Hm, hb=4, q_row0=0, k_row0=0,
                               tq=256, tk=256, scale=scale)
            mix_s = _attention(qp, kp, vp, B=Bs, T=Ts, S=Ts + P, H=Hm, hb=4, q_row0=NP, k_row0=NP,
                               tq=512, tk=512, scale=scale)
            w_o = mla_wo[j]
            ckv_list.append(ckv[:NP].reshape(Bp, Tp, kvl))
            kpe_list.append(a[:NP, ql + kvl:ql + kvl + MLA_ROPE].reshape(Bp, Tp, MLA_ROPE))
        elif kind == 1:
            H = HG_HEADS
            p = jax.nn.softmax(hgrn_lb_logits.astype(F32), axis=1)
            cum = jnp.cumsum(p, axis=1)
            lb = (cum - cum[:, :1])[:, layer].reshape(2, 1, D)
            proj = _mm(h, hgrn_w_in[j], out_dtype=F32)
            of_p, ob_p, st_p = _hgrn_scan(proj, lb, None, B=Bp, T=Tp, row0=0)
            s0t = jnp.swapaxes(state_hgrn[:, j].astype(F32), -1, -2)
            of_s, ob_s, _ = _hgrn_scan(proj, lb, s0t, B=Bs, T=Ts, row0=NP)
            mix_p = _hgrn_gate(of_p, ob_p, proj, hgrn_o_norm[j], row0=0)
            mix_s = _hgrn_gate(of_s, ob_s, proj, hgrn_o_norm[j], row0=NP)
            w_o = hgrn_wo[j]
            hg_list.append(st_p)
        else:
            Hq, Hk = GQA_HEADS, GQA_KV_HEADS
            qkv = _mm(h, gqa_w_qkv[j], out_dtype=F32)
            gains = jnp.concatenate([jnp.tile(gqa_q_norm[j][None], (Hq, 1)),
                                     jnp.tile(gqa_k_norm[j][None], (Hk, 1))], axis=0).reshape(Hq + Hk, 1, LANE)
            k_plain = _head_rope(qkv, gains[Hq:], cos_gqa, sin_gqa, col_block0=Hq, n_heads=Hk, norm=True,
                                 half=GQA_HD // 4, out_dtype=F32, n_rows=NP)
            qk = _head_rope(qkv, gains, cos_gqa, sin_gqa, col_block0=0, n_heads=Hq + Hk, norm=True,
                            half=GQA_HD // 4, out_dtype=BF16)
            kw = Hk * LANE
            k_new = qk[:, Hq * LANE:]
            v_new = qkv[:, (Hq + Hk) * LANE:].astype(BF16)
            k_s = jnp.concatenate([k_new[NP:].reshape(Bs, Ts, kw),
                                   cache_gqa_k[:, j].reshape(Bs, P, kw).astype(BF16)], axis=1)
            v_s = jnp.concatenate([v_new[NP:].reshape(Bs, Ts, kw),
                                   cache_gqa_v[:, j].reshape(Bs, P, kw).astype(BF16)], axis=1)
            k_all = jnp.concatenate([k_new[:NP], k_s.reshape(Bs * (Ts + P), kw)], axis=0)
            v_all = jnp.concatenate([v_new[:NP], v_s.reshape(Bs * (Ts + P), kw)], axis=0)
            scale = GQA_HD ** -0.5
            qp = [(qk, 0)]
            kp = [(k_all, 0, "group")]
            vp = (v_all, 0, "group")
            mix_p = _attention(qp, kp, vp, B=Bp, T=Tp, S=Tp, H=Hq, hb=Hq // Hk, q_row0=0, k_row0=0,
                               tq=256, tk=256, scale=scale)
            mix_s = _attention(qp, kp, vp, B=Bs, T=Ts, S=Ts + P, H=Hq, hb=Hq // Hk, q_row0=NP, k_row0=NP,
                               tq=512, tk=512, scale=scale)
            w_o = gqa_wo[j]
            gk_list.append(k_plain[:NP].reshape(Bp, Tp, Hk, GQA_HD))
            gv_list.append(qkv[:NP, (Hq + Hk) * LANE:].reshape(Bp, Tp, Hk, GQA_HD))

        y = _mm_resid(mix_p, mix_s, w_o, y, nw[1], g1, **rows)

        fi = layer // 2
        if layer % 2 == 0:
            y = _ffn(y, nw[2], sh2, sc2, ffn_w1[fi], ffn_w3[fi], ffn_w2[fi], nw[3], g2, **rows)
        else:
            h2, sel, gw = _adaln_in(y, nw[2], sh2, sc2, tm=tm_tok, router=moe_router[fi], **rows)
            tm_e = _tile(M, 1024)
            tables = _dispatch_tables(sel[:, :2], gw[:, :2], tm_e)
            slots = _moe_ffn(h2, tables, moe_w1[fi], moe_w3[fi], moe_w2[fi], n_tok=M, tm=tm_e)
            y = _moe_combine(slots, y, nw[3], g2, **rows)

    return (y[:NP].reshape(Bp, Tp, D), y[NP:].reshape(Bs, Ts, D),
            jnp.stack(ckv_list, axis=1), jnp.stack(kpe_list, axis=1), jnp.stack(hg_list, axis=1),
            jnp.stack(gk_list, axis=1), jnp.stack(gv_list, axis=1))
```

```python
import functools
import math

import numpy as np
import jax
import jax.numpy as jnp
from jax import lax
from jax.experimental import pallas as pl
from jax.experimental.pallas import tpu as pltpu

F32 = jnp.float32
BF16 = jnp.bfloat16
EPS = 1e-6
LANE = 128

GRID_W = 64
ROPE_THETA = 10000.0
MLA_HEADS, MLA_NOPE, MLA_ROPE, MLA_V = 16, 128, 64, 128
HG_HEADS = 16
GQA_HEADS, GQA_KV_HEADS, GQA_HD = 16, 4, 128
N_EXPERTS = 8

VMEM_LIMIT_BYTES = 56 * 1024 * 1024

SDS = jax.ShapeDtypeStruct


def _cparams(*sem):
    return pltpu.CompilerParams(dimension_semantics=sem, vmem_limit_bytes=VMEM_LIMIT_BYTES)


def _tile(n, pref, step=8):
    t = min(pref, n)
    while n % t:
        t -= step
    return t


def _dot(a, b):
    return jnp.dot(a, b, preferred_element_type=F32)


def _dot_nt(a, b):
    return lax.dot_general(a, b, (((1,), (1,)), ((), ())), preferred_element_type=F32)


def _dot_tn(a, b):
    return lax.dot_general(a, b, (((0,), (0,)), ((), ())), preferred_element_type=F32)


def _rms(x):
    return x * lax.rsqrt(jnp.mean(x * x, axis=-1, keepdims=True) + EPS)


def _silu(x):
    return x * jax.nn.sigmoid(x)


def _row_tile(n_prompt, t_sample, pref):
    return _tile(math.gcd(n_prompt, t_sample), pref)


def _group_of(i, tm, n_prompt, t_sample):
    r = i * tm
    return jnp.where(r < n_prompt, 0, 1 + (r - n_prompt) // t_sample)


def _mod_kernel(c_ref, w_ref, b_ref, o_ref):
    s = _silu(c_ref[...]).astype(BF16)
    o_ref[0] = _dot(s, w_ref[0].astype(BF16)) + b_ref[0]


def _modulation(cond8, ada_w, ada_b):
    L, D, N6 = ada_w.shape
    tn = _tile(N6, 1024)
    return pl.pallas_call(
        _mod_kernel,
        out_shape=SDS((L, 8, N6), F32),
        grid=(L, N6 // tn),
        in_specs=[
            pl.BlockSpec((8, D), lambda l, j: (0, 0)),
            pl.BlockSpec((1, D, tn), lambda l, j: (l, 0, j)),
            pl.BlockSpec((1, 1, tn), lambda l, j: (l, 0, j)),
        ],
        out_specs=pl.BlockSpec((1, 8, tn), lambda l, j: (l, 0, j)),
        compiler_params=_cparams("parallel", "parallel"),
        name="modulation",
    )(cond8, ada_w, ada_b.reshape(L, 1, N6))


def _adaln_in_kernel(y_ref, g_ref, sh_ref, sc_ref, h_ref):
    h = (_rms(y_ref[...]) * g_ref[...]) * (1.0 + sc_ref[0]) + sh_ref[0]
    h_ref[...] = h.astype(h_ref.dtype)


def _to_slab(ref, x):
    rows, D = x.shape
    S = D // LANE
    for s in range(S):
        ref[pl.ds(s, rows, stride=S), :] = x[:, s * LANE:(s + 1) * LANE]


def _from_slab(ref, rows, s, S):
    return ref[pl.ds(s, rows, stride=S), :]


def _adaln_route_kernel(y_ref, g_ref, sh_ref, sc_ref, r_ref, h_ref, sel_ref, gw_ref):
    h = (_rms(y_ref[...]) * g_ref[...]) * (1.0 + sc_ref[0]) + sh_ref[0]
    _to_slab(h_ref, h)
    logits = lax.dot_general(h, r_ref[...], (((1,), (0,)), ((), ())),
                             precision=lax.Precision.HIGHEST, preferred_element_type=F32)
    lane = lax.broadcasted_iota(jnp.int32, logits.shape, 1)
    neg = jnp.float32(-jnp.inf)
    logits = jnp.where(lane < N_EXPERTS, logits, neg)
    lane_f = lane.astype(F32)
    m1 = jnp.max(logits, axis=-1, keepdims=True)
    i1 = jnp.min(jnp.where(logits == m1, lane_f, float(LANE)), axis=-1, keepdims=True)
    rest = jnp.where(lane_f == i1, neg, logits)
    m2 = jnp.max(rest, axis=-1, keepdims=True)
    i2 = jnp.min(jnp.where(rest == m2, lane_f, float(LANE)), axis=-1, keepdims=True)
    e = jnp.exp(m2 - m1)
    w1 = 1.0 / (1.0 + e)
    w2 = e / (1.0 + e)
    i1, i2 = i1.astype(jnp.int32), i2.astype(jnp.int32)
    sel_ref[...] = jnp.where(lane == 0, i1, jnp.where(lane == 1, i2, 0))
    gw_ref[...] = jnp.where(lane == 0, w1, jnp.where(lane == 1, w2, 0.0))


def _adaln_in(y, gain, shift, scale, *, n_prompt, t_sample, tm, router=None):
    M, D = y.shape
    grp = functools.partial(_group_of, tm=tm, n_prompt=n_prompt, t_sample=t_sample)
    in_specs = [
        pl.BlockSpec((tm, D), lambda i: (i, 0)),
        pl.BlockSpec((1, D), lambda i: (0, 0)),
        pl.BlockSpec((1, 1, D), lambda i: (grp(i), 0, 0)),
        pl.BlockSpec((1, 1, D), lambda i: (grp(i), 0, 0)),
    ]
    args = [y, gain.reshape(1, D), shift, scale]
    if router is None:
        return pl.pallas_call(
            _adaln_in_kernel,
            out_shape=SDS((M, D), BF16),
            grid=(M // tm,),
            in_specs=in_specs,
            out_specs=pl.BlockSpec((tm, D), lambda i: (i, 0)),
            compiler_params=_cparams("parallel"),
            name="adaln_in",
        )(*args)
    r_pad = jnp.pad(router, ((0, 0), (0, LANE - router.shape[1])))
    return pl.pallas_call(
        _adaln_route_kernel,
        out_shape=(SDS((M * (D // LANE), LANE), F32), SDS((M, LANE), jnp.int32), SDS((M, LANE), F32)),
        grid=(M // tm,),
        in_specs=in_specs + [pl.BlockSpec((D, LANE), lambda i: (0, 0))],
        out_specs=(
            pl.BlockSpec((tm * (D // LANE), LANE), lambda i: (i, 0)),
            pl.BlockSpec((tm, LANE), lambda i: (i, 0)),
            pl.BlockSpec((tm, LANE), lambda i: (i, 0)),
        ),
        compiler_params=_cparams("parallel"),
        name="adaln_route",
    )(*args, r_pad)


def _mm_kernel(x_ref, w_ref, o_ref, wbf):
    @pl.when(pl.program_id(1) == 0)
    def _():
        wbf[...] = w_ref[...].astype(BF16)

    o_ref[...] = _dot(x_ref[...].astype(BF16), wbf[...]).astype(o_ref.dtype)


def _layer_spec(w, layer, block, index_map):
    if layer is None:
        return pl.BlockSpec(block, index_map)
    lead = w.ndim - len(block) - 1
    return pl.BlockSpec((None,) * (lead + 1) + tuple(block),
                        lambda *a: (layer,) + (0,) * lead + tuple(index_map(*a)))


def _mm(x, w, *, out_dtype, layer=None, tm=1024, tn=1024):
    M, K = x.shape
    N = w.shape[-1]
    tm = _tile(M, tm)
    tn = _tile(N, tn, LANE)
    return pl.pallas_call(
        _mm_kernel,
        out_shape=SDS((M, N), out_dtype),
        grid=(N // tn, M // tm),
        in_specs=[
            pl.BlockSpec((tm, K), lambda j, i: (i, 0)),
            _layer_spec(w, layer, (K, tn), lambda j, i: (0, j)),
        ],
        out_specs=pl.BlockSpec((tm, tn), lambda j, i: (i, j)),
        scratch_shapes=[pltpu.VMEM((K, tn), BF16)],
        compiler_params=_cparams("parallel", "arbitrary"),
        name="mm",
    )(x, w)


def _mm_resid_kernel(xp_ref, xs_ref, w_ref, y_ref, g_ref, gate_ref, o_ref, wbf, *, n_ptiles):
    i = pl.program_id(0)

    @pl.when(i == 0)
    def _():
        wbf[...] = w_ref[...].astype(BF16)

    x = jnp.where(i < n_ptiles, xp_ref[...], xs_ref[...])
    a = _dot(x.astype(BF16), wbf[...])
    o_ref[...] = y_ref[...] + gate_ref[0] * (_rms(a) * g_ref[...])


def _mm_resid(x_p, x_s, w, y, gain, gate, *, layer, n_prompt, t_sample, tm=256):
    M, D = y.shape
    K = w.shape[-2]
    tm = _row_tile(n_prompt, t_sample, tm)
    n_ptiles = n_prompt // tm
    grp = functools.partial(_group_of, tm=tm, n_prompt=n_prompt, t_sample=t_sample)
    return pl.pallas_call(
        functools.partial(_mm_resid_kernel, n_ptiles=n_ptiles),
        out_shape=SDS((M, D), F32),
        grid=(M // tm,),
        in_specs=[
            pl.BlockSpec((tm, K), lambda i: (jnp.minimum(i, n_ptiles - 1), 0)),
            pl.BlockSpec((tm, K), lambda i: (jnp.maximum(i - n_ptiles, 0), 0)),
            pl.BlockSpec((None, K, D), lambda i: (layer, 0, 0), pipeline_mode=pl.Buffered(1)),
            pl.BlockSpec((tm, D), lambda i: (i, 0)),
            pl.BlockSpec((1, D), lambda i: (0, 0)),
            pl.BlockSpec((1, 1, D), lambda i: (grp(i), 0, 0)),
        ],
        out_specs=pl.BlockSpec((tm, D), lambda i: (i, 0)),
        scratch_shapes=[pltpu.VMEM((K, D), BF16)],
        compiler_params=_cparams("arbitrary"),
        name="mm_resid",
    )(x_p, x_s, w, y, gain.reshape(1, D), gate)


def _cast_kernel(x_ref, o_ref):
    o_ref[...] = x_ref[...].astype(o_ref.dtype)


def _cast_bf16(w, layer, *, tr=1024):
    shape = w.shape[1:]
    R = math.prod(shape[:-1])
    N = shape[-1]
    w2 = w.reshape(w.shape[0] * R, N)
    tr = _tile(R, tr)
    nb = R // tr
    out = pl.pallas_call(
        _cast_kernel,
        out_shape=SDS((R, N), BF16),
        grid=(nb,),
        in_specs=[pl.BlockSpec((tr, N), lambda i: (layer * nb + i, 0))],
        out_specs=pl.BlockSpec((tr, N), lambda i: (i, 0)),
        compiler_params=_cparams("parallel"),
        name="cast_bf16",
    )(w2)
    return out.reshape(shape)


def _gate_up_kernel(x_ref, w1_ref, w3_ref, u_ref, wb1, wb3):
    @pl.when(pl.program_id(1) == 0)
    def _():
        wb1[...] = w1_ref[...].astype(BF16)
        wb3[...] = w3_ref[...].astype(BF16)

    x = x_ref[...]
    u_ref[...] = (_silu(_dot(x, wb1[...])) * _dot(x, wb3[...])).astype(u_ref.dtype)


def _gate_up(x, w1, w3, layer, *, tm=1024, tn=512):
    M, D = x.shape
    F = w1.shape[-1]
    tm = _tile(M, tm)
    tn = _tile(F, tn, LANE)
    wspec = _layer_spec(w1, layer, (D, tn), lambda j, i: (0, j))
    return pl.pallas_call(
        _gate_up_kernel,
        out_shape=SDS((M, F), BF16),
        grid=(F // tn, M // tm),
        in_specs=[pl.BlockSpec((tm, D), lambda j, i: (i, 0)), wspec, wspec],
        out_specs=pl.BlockSpec((tm, tn), lambda j, i: (i, j)),
        scratch_shapes=[pltpu.VMEM((D, tn), BF16), pltpu.VMEM((D, tn), BF16)],
        compiler_params=_cparams("parallel", "arbitrary"),
        name="gate_up",
    )(x, w1, w3)


def _down_resid_kernel(u_ref, w_ref, y_ref, g_ref, gate_ref, o_ref, *, nk):
    k = pl.program_id(1)

    @pl.when(k == 0)
    def _():
        o_ref[...] = jnp.zeros_like(o_ref)

    o_ref[...] += _dot(u_ref[...], w_ref[...])

    @pl.when(k == nk - 1)
    def _():
        o_ref[...] = y_ref[...] + gate_ref[0] * (_rms(o_ref[...]) * g_ref[...])


def _down_resid(u, w2b, y, gain, gate, *, n_prompt, t_sample, tm=1024, tk=512):
    M, D = y.shape
    F = u.shape[1]
    tm = _row_tile(n_prompt, t_sample, tm)
    tk = _tile(F, tk, LANE)
    nk = F // tk
    grp = functools.partial(_group_of, tm=tm, n_prompt=n_prompt, t_sample=t_sample)
    return pl.pallas_call(
        functools.partial(_down_resid_kernel, nk=nk),
        out_shape=SDS((M, D), F32),
        grid=(M // tm, nk),
        in_specs=[
            pl.BlockSpec((tm, tk), lambda i, k: (i, k)),
            pl.BlockSpec((tk, D), lambda i, k: (k, 0)),
            pl.BlockSpec((tm, D), lambda i, k: (i, 0), pipeline_mode=pl.Buffered(1)),
            pl.BlockSpec((1, D), lambda i, k: (0, 0)),
            pl.BlockSpec((1, 1, D), lambda i, k: (grp(i), 0, 0)),
        ],
        out_specs=pl.BlockSpec((tm, D), lambda i, k: (i, 0)),
        compiler_params=_cparams("parallel", "arbitrary"),
        name="down_resid",
    )(u, w2b, y, gain.reshape(1, D), gate)


def _gather_sorted_kernel(tr_ref, src_ref, h_hbm, o_ref, xs, sem, *, tm, S):
    i = pl.program_id(0)
    base = i * tm

    @pl.when(tr_ref[i] == 0)
    def _():
        o_ref[...] = jnp.zeros_like(o_ref)

    @pl.when(tr_ref[i] > 0)
    def _():
        def issue(r, c):
            s0 = pl.multiple_of(src_ref[base + r] * S, S)
            pltpu.make_async_copy(h_hbm.at[pl.ds(s0, S)], xs.at[pl.ds(pl.multiple_of(r * S, S), S)], sem).start()
            return c

        lax.fori_loop(0, tm, issue, 0)
        pltpu.make_async_copy(h_hbm.at[pl.ds(0, tm * S)], xs, sem).wait()
        for s in range(S):
            o_ref[:, s * LANE:(s + 1) * LANE] = _from_slab(xs, tm, s, S).astype(o_ref.dtype)


def _gather_sorted(h_slab, tile_rows, src_tok, *, tm, D):
    S = D // LANE
    nt = tile_rows.shape[0]
    grid_spec = pltpu.PrefetchScalarGridSpec(
        num_scalar_prefetch=2,
        grid=(nt,),
        in_specs=[pl.BlockSpec(memory_space=pl.ANY)],
        out_specs=pl.BlockSpec((tm, D), lambda i, tr, sr: (i, 0)),
        scratch_shapes=[pltpu.VMEM((tm * S, LANE), F32), pltpu.SemaphoreType.DMA(())],
    )
    return pl.pallas_call(
        functools.partial(_gather_sorted_kernel, tm=tm, S=S),
        out_shape=SDS((nt * tm, D), BF16),
        grid_spec=grid_spec,
        compiler_params=_cparams("arbitrary"),
        name="gather_sorted",
    )(tile_rows, src_tok, h_slab)


def _moe_gate_up_kernel(te_ref, tr_ref, tf_ref, x_ref, w1_ref, w3_ref, u_ref, wb1, wb3):
    i = pl.program_id(1)

    @pl.when(tf_ref[i] > 0)
    def _():
        wb1[...] = w1_ref[...].astype(BF16)
        wb3[...] = w3_ref[...].astype(BF16)

    @pl.when(tr_ref[i] > 0)
    def _():
        x = x_ref[...]
        u_ref[...] = (_silu(_dot(x, wb1[...])) * _dot(x, wb3[...])).astype(u_ref.dtype)

    @pl.when(tr_ref[i] == 0)
    def _():
        u_ref[...] = jnp.zeros_like(u_ref)


def _moe_gate_up(x_sorted, tile_e, tile_rows, tile_first, w1, w3, layer, *, tm, tn=512):
    R, D = x_sorted.shape
    F = w1.shape[-1]
    tn = _tile(F, tn, LANE)
    wspec = pl.BlockSpec((None, None, D, tn), lambda j, i, te, tr, tf: (layer, te[i], 0, j))
    grid_spec = pltpu.PrefetchScalarGridSpec(
        num_scalar_prefetch=3,
        grid=(F // tn, R // tm),
        in_specs=[pl.BlockSpec((tm, D), lambda j, i, te, tr, tf: (i, 0)), wspec, wspec],
        out_specs=pl.BlockSpec((tm, tn), lambda j, i, te, tr, tf: (i, j)),
        scratch_shapes=[pltpu.VMEM((D, tn), BF16), pltpu.VMEM((D, tn), BF16)],
    )
    return pl.pallas_call(
        _moe_gate_up_kernel,
        out_shape=SDS((R, F), BF16),
        grid_spec=grid_spec,
        compiler_params=_cparams("parallel", "arbitrary"),
        name="moe_gate_up",
    )(tile_e, tile_rows, tile_first, x_sorted, w1, w3)


def _moe_down_kernel(te_ref, tr_ref, dst_ref, u_ref, w_ref, gate_ref, out_hbm, acc, stage, sem, *, nk, tm, S):
    i = pl.program_id(0)
    k = pl.program_id(1)
    nrows = tr_ref[i]
    base = i * tm

    @pl.when(nrows > 0)
    def _():
        @pl.when(k == 0)
        def _():
            acc[...] = jnp.zeros_like(acc)

        acc[...] += _dot(u_ref[...], w_ref[0])

        @pl.when(k == nk - 1)
        def _():
            _to_slab(stage, acc[...] * gate_ref[...])

            def row_copy(r, d0):
                return pltpu.make_async_copy(stage.at[pl.ds(pl.multiple_of(r * S, S), S)],
                                             out_hbm.at[pl.ds(d0, S)], sem)

            def issue(r, c):
                row_copy(r, pl.multiple_of(dst_ref[base + r] * S, S)).start()
                return c

            def drain(r, c):
                row_copy(r, 0).wait()
                return c

            lax.fori_loop(0, nrows, issue, 0)
            lax.fori_loop(0, nrows, drain, 0)


def _moe_down(u_sorted, tile_e, tile_rows, dst_row, gate_rows, w2b, *, n_tok, tm, tk=512):
    R, F = u_sorted.shape
    D = w2b.shape[2]
    S = D // LANE
    tk = _tile(F, tk, LANE)
    nk = F // tk

    def kidx(i, k, tr):
        return jnp.where(tr[i] > 0, k, nk - 1)

    grid_spec = pltpu.PrefetchScalarGridSpec(
        num_scalar_prefetch=3,
        grid=(R // tm, nk),
        in_specs=[
            pl.BlockSpec((tm, tk), lambda i, k, te, tr, ds: (i, kidx(i, k, tr))),
            pl.BlockSpec((1, tk, D), lambda i, k, te, tr, ds: (te[i], kidx(i, k, tr), 0)),
            pl.BlockSpec((tm, 1), lambda i, k, te, tr, ds: (i, 0)),
        ],
        out_specs=pl.BlockSpec(memory_space=pl.ANY),
        scratch_shapes=[pltpu.VMEM((tm, D), F32), pltpu.VMEM((tm * S, LANE), F32), pltpu.SemaphoreType.DMA(())],
    )
    return pl.pallas_call(
        functools.partial(_moe_down_kernel, nk=nk, tm=tm, S=S),
        out_shape=SDS((2 * n_tok * S, LANE), F32),
        grid_spec=grid_spec,
        compiler_params=pltpu.CompilerParams(dimension_semantics=("arbitrary", "arbitrary"),
                                             vmem_limit_bytes=VMEM_LIMIT_BYTES, has_side_effects=True),
        name="moe_down",
    )(tile_e, tile_rows, dst_row, u_sorted, w2b, gate_rows)


def _combine_kernel(a_ref, b_ref, y_ref, g_ref, gate_ref, o_ref, *, tm, S):
    parts = [_from_slab(a_ref, tm, s, S) + _from_slab(b_ref, tm, s, S) for s in range(S)]
    ss = parts[0] * parts[0]
    for t in parts[1:]:
        ss = ss + t * t
    r = lax.rsqrt(jnp.sum(ss, axis=-1, keepdims=True) / (S * LANE) + EPS)
    for s, t in enumerate(parts):
        c = slice(s * LANE, (s + 1) * LANE)
        o_ref[:, c] = y_ref[:, c] + gate_ref[0][:, c] * (t * r * g_ref[:, c])


def _moe_combine(slots, y, gain, gate, *, n_prompt, t_sample, tm=256):
    M, D = y.shape
    S = D // LANE
    tm = _row_tile(n_prompt, t_sample, tm)
    nb = M // tm
    grp = functools.partial(_group_of, tm=tm, n_prompt=n_prompt, t_sample=t_sample)
    return pl.pallas_call(
        functools.partial(_combine_kernel, tm=tm, S=S),
        out_shape=SDS((M, D), F32),
        grid=(nb,),
        in_specs=[
            pl.BlockSpec((tm * S, LANE), lambda i: (i, 0)),
            pl.BlockSpec((tm * S, LANE), lambda i: (nb + i, 0)),
            pl.BlockSpec((tm, D), lambda i: (i, 0)),
            pl.BlockSpec((1, D), lambda i: (0, 0)),
            pl.BlockSpec((1, 1, D), lambda i: (grp(i), 0, 0)),
        ],
        out_specs=pl.BlockSpec((tm, D), lambda i: (i, 0)),
        compiler_params=_cparams("parallel"),
        name="moe_combine",
    )(slots, slots, y, gain.reshape(1, D), gate)


def _dispatch_tables(sel, gw, tm):
    M = sel.shape[0]
    E = N_EXPERTS
    i32 = jnp.int32
    e_flat = sel.reshape(-1)
    pair = jnp.arange(2 * M, dtype=i32)
    onehot = (e_flat[:, None] == jnp.arange(E, dtype=i32)[None, :]).astype(i32)
    running = jnp.cumsum(onehot, axis=0)
    rank = jnp.sum(running * onehot, axis=1) - 1
    counts = running[-1]
    padded = ((counts + tm - 1) // tm) * tm
    ends_p = jnp.cumsum(padded)
    starts_p = ends_p - padded
    pos = starts_p[e_flat] + rank
    R = 2 * M + E * tm
    src_tok = jnp.zeros((R,), i32).at[pos].set(pair // 2)
    dst_row = jnp.zeros((R,), i32).at[pos].set((pair % 2) * M + pair // 2)
    gate_rows = jnp.zeros((R,), F32).at[pos].set(gw.reshape(-1))
    tile_start = jnp.arange(R // tm, dtype=i32) * tm
    tile_e = jnp.minimum(jnp.sum(ends_p[None, :] <= tile_start[:, None], axis=1), E - 1).astype(i32)
    tile_rows = jnp.clip(starts_p[tile_e] + counts[tile_e] - tile_start, 0, tm).astype(i32)
    tile_first = jnp.concatenate([jnp.ones((1,), i32), (tile_e[1:] != tile_e[:-1]).astype(i32)])
    return tile_e, tile_rows, tile_first, src_tok, dst_row, gate_rows.reshape(R, 1)


def _attn_kernel(*refs, n_parts, k_modes, v_mode, hb, scale, nk):
    q_refs = refs[:n_parts]
    k_refs = refs[n_parts:2 * n_parts]
    v_ref = refs[2 * n_parts]
    o_ref = refs[2 * n_parts + 1]
    m_sc, l_sc, acc_sc = refs[2 * n_parts + 2:]
    j = pl.program_id(3)

    @pl.when(j == 0)
    def _():
        m_sc[...] = jnp.full_like(m_sc, -jnp.inf)
        l_sc[...] = jnp.zeros_like(l_sc)
        acc_sc[...] = jnp.zeros_like(acc_sc)

    def block(ref, mode, hh):
        x = ref[:, hh * LANE:(hh + 1) * LANE] if mode == "head" else ref[...]
        return x.astype(BF16)

    for hh in range(hb):
        q = [block(q_refs[p], "head", hh) for p in range(n_parts)]
        k = [block(k_refs[p], k_modes[p], hh) for p in range(n_parts)]
        q = q[0] if n_parts == 1 else jnp.concatenate(q, axis=1)
        k = k[0] if n_parts == 1 else jnp.concatenate(k, axis=1)
        s = _dot_nt(q, k) * scale
        m_prev = m_sc[hh]
        m_new = jnp.maximum(m_prev, jnp.max(s, axis=-1, keepdims=True))
        alpha = jnp.exp(m_prev - m_new)
        p = jnp.exp(s - m_new)
        l_sc[hh] = alpha * l_sc[hh] + jnp.sum(p, axis=-1, keepdims=True)
        acc_sc[hh] = alpha * acc_sc[hh] + _dot(p.astype(BF16), block(v_ref, v_mode, hh))
        m_sc[hh] = m_new

    @pl.when(j == nk - 1)
    def _():
        for hh in range(hb):
            o_ref[:, hh * LANE:(hh + 1) * LANE] = (acc_sc[hh] / l_sc[hh]).astype(o_ref.dtype)


def _attention(q_parts, k_parts, v_part, *, B, T, S, H, hb, q_row0, k_row0, tq, tk, scale):
    tq = _tile(T, tq)
    tk = _tile(S, tk)
    nq, nk = T // tq, S // tk
    qb0, kb0 = q_row0 // tq, k_row0 // tk

    def q_spec(off):
        return pl.BlockSpec((tq, hb * LANE), lambda b, g, i, j: (qb0 + b * nq + i, off // hb + g))

    def k_spec(off, mode):
        if mode == "head":
            return pl.BlockSpec((tk, hb * LANE), lambda b, g, i, j: (kb0 + b * nk + j, off // hb + g))
        return pl.BlockSpec((tk, LANE), lambda b, g, i, j: (kb0 + b * nk + j, off + (g if mode == "group" else 0)))

    in_specs = [q_spec(off) for (_, off) in q_parts]
    in_specs += [k_spec(off, mode) for (_, off, mode) in k_parts]
    in_specs += [k_spec(v_part[1], v_part[2])]
    args = [a for (a, _) in q_parts] + [a for (a, _, _) in k_parts] + [v_part[0]]
    return pl.pallas_call(
        functools.partial(_attn_kernel, n_parts=len(q_parts), k_modes=[m for (_, _, m) in k_parts],
                          v_mode=v_part[2], hb=hb, scale=scale, nk=nk),
        out_shape=SDS((B * T, H * LANE), BF16),
        grid=(B, H // hb, nq, nk),
        in_specs=in_specs,
        out_specs=pl.BlockSpec((tq, hb * LANE), lambda b, g, i, j: (b * nq + i, g)),
        scratch_shapes=[pltpu.VMEM((hb, tq, 1), F32), pltpu.VMEM((hb, tq, 1), F32),
                        pltpu.VMEM((hb, tq, LANE), F32)],
        compiler_params=_cparams("parallel", "parallel", "parallel", "arbitrary"),
        name="attention",
    )(*args)


def _rmsnorm_cols_kernel(x_ref, g_ref, o_ref):
    o_ref[...] = (_rms(x_ref[...]) * g_ref[...]).astype(o_ref.dtype)


def _rmsnorm_cols(x, gain, *, col_block, width, out_dtype, tm=1024):
    M = x.shape[0]
    tm = _tile(M, tm)
    return pl.pallas_call(
        _rmsnorm_cols_kernel,
        out_shape=SDS((M, width), out_dtype),
        grid=(M // tm,),
        in_specs=[pl.BlockSpec((tm, width), lambda i: (i, col_block)),
                  pl.BlockSpec((1, width), lambda i: (0, 0))],
        out_specs=pl.BlockSpec((tm, width), lambda i: (i, 0)),
        compiler_params=_cparams("parallel"),
        name="rmsnorm_cols",
    )(x, gain.reshape(1, width))


def _head_rope_kernel(x_ref, g_ref, cos_ref, sin_ref, o_ref, *, norm, half):
    x = x_ref[...]
    if norm:
        x = _rms(x) * g_ref[0]
    lane = lax.broadcasted_iota(jnp.int32, x.shape, 1)
    first = (lane % (2 * half)) < half
    rot = jnp.where(first, -pltpu.roll(x, LANE - half, 1), pltpu.roll(x, half, 1))
    o_ref[...] = (x * cos_ref[...] + rot * sin_ref[...]).astype(o_ref.dtype)


def _head_rope(x, gains, cos, sin, *, col_block0, n_heads, norm, half, out_dtype, n_rows=None, tm=1024):
    M = x.shape[0] if n_rows is None else n_rows
    tm = _tile(M, tm)
    return pl.pallas_call(
        functools.partial(_head_rope_kernel, norm=norm, half=half),
        out_shape=SDS((M, n_heads * LANE), out_dtype),
        grid=(M // tm, n_heads),
        in_specs=[
            pl.BlockSpec((tm, LANE), lambda i, h: (i, col_block0 + h)),
            pl.BlockSpec((1, 1, LANE), lambda i, h: (h, 0, 0)),
            pl.BlockSpec((tm, LANE), lambda i, h: (i, 0)),
            pl.BlockSpec((tm, LANE), lambda i, h: (i, 0)),
        ],
        out_specs=pl.BlockSpec((tm, LANE), lambda i, h: (i, h)),
        compiler_params=_cparams("parallel", "parallel"),
        name="head_rope",
    )(x, gains, cos, sin)


def _rope_tables(n_prompt, dec_batch, t_sample, half):
    tok = jnp.arange(t_sample, dtype=jnp.int32)
    row, col = tok // GRID_W, tok % GRID_W
    inv = ROPE_THETA ** (-jnp.arange(half, dtype=F32) / half)

    def cs(pos):
        ang = pos.astype(F32)[:, None] * inv[None, :]
        return (jnp.concatenate([jnp.cos(ang), jnp.cos(ang)], -1),
                jnp.concatenate([jnp.sin(ang), jnp.sin(ang)], -1))

    cr, sr = cs(row)
    cc, sc = cs(col)
    pad = LANE - 4 * half
    cos = jnp.concatenate([cr, cc, jnp.ones((t_sample, pad), F32)], -1)
    sin = jnp.concatenate([sr, sc, jnp.zeros((t_sample, pad), F32)], -1)
    cos = jnp.concatenate([jnp.ones((n_prompt, LANE), F32), jnp.tile(cos, (dec_batch, 1))], 0)
    sin = jnp.concatenate([jnp.zeros((n_prompt, LANE), F32), jnp.tile(sin, (dec_batch, 1))], 0)
    return cos, sin


def _hgrn_constants(C, reverse):
    L = int(np.log2(C))
    t = np.arange(C)[:, None]
    u = np.arange(C)[None, :]
    Ws, Ms = [], []
    for lvl in range(L):
        b = 1 << lvl
        grp = t // (2 * b)
        if not reverse:
            bnd = grp * 2 * b + b - 1
            qside = (t % (2 * b)) >= b
            W = np.where(qside, (u > bnd) & (u <= t), (u > t) & (u <= bnd))
            kside_s = (u % (2 * b)) < b
        else:
            bnd = grp * 2 * b + b
            qside = (t % (2 * b)) < b
            W = np.where(qside, (u >= t) & (u < bnd), (u >= bnd) & (u < t))
            kside_s = (u % (2 * b)) >= b
        Ws.append(W)
        Ms.append(qside & kside_s & (grp == u // (2 * b)))
    if not reverse:
        Ws += [u <= t, u > t]
    else:
        Ws += [u >= t, u < t]
    Ms.append(t == u)
    W_all = jnp.asarray(np.concatenate(Ws, 0).astype(np.float32), BF16)
    M_all = jnp.asarray(np.stack(Ms).astype(np.float32), F32)
    return W_all, M_all


def _hgrn_chain(q, v, z, lb, W_all, m_ref, st, *, C, L, edge_row):
    f = lb + (1.0 - lb) * jax.nn.sigmoid(z)
    k = 1.0 - f
    lf = jnp.log(f)
    hi = lf.astype(BF16)
    r1 = lf - hi.astype(F32)
    mid = r1.astype(BF16)
    lo = (r1 - mid.astype(F32)).astype(BF16)
    d3 = _dot(W_all, jnp.concatenate([hi, mid, lo], axis=1))
    dn = d3[:, :LANE] + d3[:, LANE:2 * LANE] + d3[:, 2 * LANE:]
    e_all = jnp.exp(dn)
    qb = q.astype(BF16)
    a = m_ref[L] * _dot_nt(qb, k.astype(BF16))
    for lvl in range(L):
        e = e_all[lvl * C:(lvl + 1) * C]
        a = a + m_ref[lvl] * _dot_nt((q * e).astype(BF16), (k * e).astype(BF16))
    eq = e_all[L * C:(L + 1) * C]
    ek = e_all[(L + 1) * C:(L + 2) * C]
    vb = v.astype(BF16)
    o = _dot(a.astype(BF16), vb) + _dot_nt((q * eq).astype(BF16), st.astype(BF16))
    g_edge = dn[L * C + edge_row:L * C + edge_row + 1]
    st_new = st * jnp.exp(g_edge) + _dot_tn(vb, (k * ek).astype(BF16))
    return o, st_new


def _hgrn_kernel(*refs, C, L, nc, hb, has_init):
    (qf_ref, vf_ref, zf_ref, qb_ref, vb_ref, zb_ref, lbf_ref, lbb_ref,
     wf_ref, mf_ref, wb_ref, mb_ref) = refs[:12]
    rest = refs[12:]
    if has_init:
        s0_ref, rest = rest[0], rest[1:]
    of_ref, ob_ref, sfin_ref, stf, stb = rest
    c = pl.program_id(2)

    @pl.when(c == 0)
    def _():
        for hh in range(hb):
            if has_init:
                stf[hh] = s0_ref[0, 0, hh]
                stb[hh] = s0_ref[0, 1, hh]
            else:
                stf[hh] = jnp.zeros((LANE, LANE), F32)
                stb[hh] = jnp.zeros((LANE, LANE), F32)

    for hh in range(hb):
        cols = slice(hh * LANE, (hh + 1) * LANE)
        o, s_new = _hgrn_chain(qf_ref[:, cols], vf_ref[:, cols], zf_ref[:, cols], lbf_ref[0][:, cols],
                               wf_ref[...], mf_ref, stf[hh], C=C, L=L, edge_row=C - 1)
        of_ref[:, cols] = o
        stf[hh] = s_new
        o, s_new = _hgrn_chain(qb_ref[:, cols], vb_ref[:, cols], zb_ref[:, cols], lbb_ref[0][:, cols],
                               wb_ref[...], mb_ref, stb[hh], C=C, L=L, edge_row=0)
        ob_ref[:, cols] = o
        stb[hh] = s_new

    @pl.when(c == nc - 1)
    def _():
        for hh in range(hb):
            sfin_ref[0, 0, hh] = stf[hh].T
            sfin_ref[0, 1, hh] = stb[hh].T


def _hgrn_scan(proj, lb, s0t, *, B, T, row0, C=128, hb=2):
    D = proj.shape[1] // 5
    H = D // LANE
    C = _tile(T, C)
    L = int(np.log2(C))
    assert 1 << L == C
    nc = T // C
    rb0 = row0 // C
    HB = H // hb
    W = hb * LANE
    wf, mf = _hgrn_constants(C, False)
    wb, mb = _hgrn_constants(C, True)

    def fwd(col0):
        return pl.BlockSpec((C, W), lambda b, g, c: (rb0 + b * nc + c, col0 // hb + g))

    def bwd(col0):
        return pl.BlockSpec((C, W), lambda b, g, c: (rb0 + b * nc + nc - 1 - c, col0 // hb + g))

    def const(a):
        return pl.BlockSpec(a.shape, lambda b, g, c: (0,) * a.ndim)

    in_specs = [fwd(0), fwd(H), fwd(3 * H), bwd(0), bwd(H), bwd(4 * H),
                pl.BlockSpec((1, 1, W), lambda b, g, c: (0, 0, g)),
                pl.BlockSpec((1, 1, W), lambda b, g, c: (1, 0, g)),
                const(wf), const(mf), const(wb), const(mb)]
    args = [proj, proj, proj, proj, proj, proj, lb, lb, wf, mf, wb, mb]
    if s0t is not None:
        in_specs.append(pl.BlockSpec((1, 2, hb, LANE, LANE), lambda b, g, c: (b, 0, g, 0, 0)))
        args.append(s0t)
    return pl.pallas_call(
        functools.partial(_hgrn_kernel, C=C, L=L, nc=nc, hb=hb, has_init=s0t is not None),
        out_shape=(SDS((B * T, D), F32), SDS((B * T, D), F32), SDS((B, 2, H, LANE, LANE), F32)),
        grid=(B, HB, nc),
        in_specs=in_specs,
        out_specs=(
            pl.BlockSpec((C, W), lambda b, g, c: (b * nc + c, g)),
            pl.BlockSpec((C, W), lambda b, g, c: (b * nc + nc - 1 - c, g)),
            pl.BlockSpec((1, 2, hb, LANE, LANE), lambda b, g, c: (b, 0, g, 0, 0)),
        ),
        scratch_shapes=[pltpu.VMEM((hb, LANE, LANE), F32), pltpu.VMEM((hb, LANE, LANE), F32)],
        compiler_params=_cparams("parallel", "parallel", "arbitrary"),
        name="hgrn_scan",
    )(*args)


def _hgrn_gate_kernel(of_ref, ob_ref, g_ref, w_ref, o_ref):
    o = _rms(of_ref[...] + ob_ref[...]) * w_ref[0]
    o_ref[...] = (o * _silu(g_ref[...])).astype(o_ref.dtype)


def _hgrn_gate(o_f, o_b, proj, o_norm, *, row0, tm=1024):
    M, D = o_f.shape
    H = D // LANE
    tm = _tile(math.gcd(M, row0) if row0 else M, tm)
    rb0 = row0 // tm
    blk = pl.BlockSpec((tm, LANE), lambda i, h: (i, h))
    return pl.pallas_call(
        _hgrn_gate_kernel,
        out_shape=SDS((M, D), BF16),
        grid=(M // tm, H),
        in_specs=[blk, blk,
                  pl.BlockSpec((tm, LANE), lambda i, h: (rb0 + i, 2 * H + h)),
                  pl.BlockSpec((1, 1, LANE), lambda i, h: (h, 0, 0))],
        out_specs=blk,
        compiler_params=_cparams("parallel", "parallel"),
        name="hgrn_gate",
    )(o_f, o_b, proj, o_norm.reshape(H, 1, LANE))


def kernel(x_prompt, x_sample, c, cache_mla_ckv, cache_mla_kpe, state_hgrn, cache_gqa_k, cache_gqa_v, c_ctx, ada_w, ada_b, norm_w, mla_wq_a, mla_q_norm, mla_wq_b, mla_wkv_a, mla_kv_norm, mla_wkv_b, mla_wo, hgrn_w_in, hgrn_lb_logits, hgrn_o_norm, hgrn_wo, gqa_w_qkv, gqa_q_norm, gqa_k_norm, gqa_wo, ffn_w1, ffn_w3, ffn_w2, moe_router, moe_w1, moe_w3, moe_w2):
    Bp, Tp, D = x_prompt.shape
    Bs, Ts, _ = x_sample.shape
    P = cache_mla_ckv.shape[2]
    depth = ada_w.shape[0]
    NP, NS = Bp * Tp, Bs * Ts
    M = NP + NS
    G = 1 + Bs
    assert D == HG_HEADS * LANE
    rows = dict(n_prompt=NP, t_sample=Ts)

    y = jnp.concatenate([x_prompt.reshape(NP, D), x_sample.reshape(NS, D)], axis=0)
    cond8 = jnp.zeros((8, D), F32).at[0].set(c_ctx).at[1:G].set(c)
    mod = _modulation(cond8, ada_w, ada_b).reshape(depth, 8, 6, D)
    mod = jnp.transpose(mod, (0, 2, 1, 3))[:, :, :G, None, :]

    tm_tok = _row_tile(NP, Ts, 512)
    cos_mla, sin_mla = _rope_tables(NP, Bs, Ts, MLA_ROPE // 4)
    cos_gqa, sin_gqa = _rope_tables(NP, Bs, Ts, GQA_HD // 4)
    ones_g = jnp.ones((MLA_HEADS, 1, LANE), F32)

    ckv_list, kpe_list, hg_list, gk_list, gv_list = [], [], [], [], []
    for layer in range(depth):
        kind, j = layer % 3, layer // 3
        nw = norm_w[layer]
        sh1, sc1, g1, sh2, sc2, g2 = (mod[layer, k] for k in range(6))
        h = _adaln_in(y, nw[0], sh1, sc1, tm=tm_tok, **rows)

        if kind == 0:
            Hm = MLA_HEADS
            w_a = jnp.concatenate([mla_wq_a[j], mla_wkv_a[j],
                                   jnp.zeros((D, LANE - MLA_ROPE), F32)], axis=1)
            a = _mm(h, w_a, out_dtype=F32)
            ql, kvl = mla_wq_a.shape[2], mla_kv_norm.shape[1]
            qn = _rmsnorm_cols(a, mla_q_norm[j], col_block=0, width=ql, out_dtype=BF16)
            ckv = _rmsnorm_cols(a, mla_kv_norm[j], col_block=ql // kvl, width=kvl, out_dtype=F32)
            wqb = mla_wq_b[j].reshape(ql, Hm, MLA_NOPE + MLA_ROPE)
            wqb = jnp.concatenate([
                wqb[:, :, :MLA_NOPE].reshape(ql, Hm * MLA_NOPE),
                jnp.pad(wqb[:, :, MLA_NOPE:], ((0, 0), (0, 0), (0, LANE - MLA_ROPE))).reshape(ql, Hm * LANE),
            ], axis=1)
            q = _mm(qn, wqb, out_dtype=F32)
            pe_blk = (ql + kvl) // LANE
            q_pe = _head_rope(q, ones_g, cos_mla, sin_mla, col_block0=Hm, n_heads=Hm, norm=False,
                              half=MLA_ROPE // 4, out_dtype=BF16)
            k_pe = _head_rope(a, ones_g, cos_mla, sin_mla, col_block0=pe_blk, n_heads=1, norm=False,
                              half=MLA_ROPE // 4, out_dtype=BF16)
            wkvb = mla_wkv_b[j].reshape(kvl, Hm, MLA_NOPE + MLA_V)
            wkvb = jnp.concatenate([wkvb[:, :, :MLA_NOPE].reshape(kvl, Hm * MLA_NOPE),
                                    wkvb[:, :, MLA_NOPE:].reshape(kvl, Hm * MLA_V)], axis=1)
            ckv_s = jnp.concatenate([ckv[NP:].reshape(Bs, Ts, kvl), cache_mla_ckv[:, j]], axis=1)
            c_all = jnp.concatenate([ckv[:NP], ckv_s.reshape(Bs * (Ts + P), kvl)], axis=0)
            kv = _mm(c_all, wkvb, out_dtype=BF16)
            kpe_ctx = jnp.pad(cache_mla_kpe[:, j], ((0, 0), (0, 0), (0, LANE - MLA_ROPE))).astype(BF16)
            kpe_s = jnp.concatenate([k_pe[NP:].reshape(Bs, Ts, LANE), kpe_ctx], axis=1)
            kpe_all = jnp.concatenate([k_pe[:NP], kpe_s.reshape(Bs * (Ts + P), LANE)], axis=0)
            scale = (MLA_NOPE + MLA_ROPE) ** -0.5
            qp = [(q, 0), (q_pe, 0)]
            kp = [(kv, 0, "head"), (kpe_all, 0, "shared")]
            vp = (kv, Hm, "head")
            mix_p = _attention(qp, kp, vp, B=Bp, T=Tp, S=Tp, H=Hm, hb=4, q_row0=0, k_row0=0,
                               tq=256, tk=256, scale=scale)
            mix_s = _attention(qp, kp, vp, B=Bs, T=Ts, S=Ts + P, H=Hm, hb=4, q_row0=NP, k_row0=NP,
                               tq=512, tk=512, scale=scale)
            w_o = mla_wo
            ckv_list.append(ckv[:NP].reshape(Bp, Tp, kvl))
            kpe_list.append(a[:NP, ql + kvl:ql + kvl + MLA_ROPE].reshape(Bp, Tp, MLA_ROPE))
        elif kind == 1:
            H = HG_HEADS
            p = jax.nn.softmax(hgrn_lb_logits.astype(F32), axis=1)
            cum = jnp.cumsum(p, axis=1)
            lb = (cum - cum[:, :1])[:, layer].reshape(2, 1, D)
            proj = _mm(h, hgrn_w_in, layer=j, out_dtype=F32)
            of_p, ob_p, st_p = _hgrn_scan(proj, lb, None, B=Bp, T=Tp, row0=0)
            s0t = jnp.swapaxes(state_hgrn[:, j].astype(F32), -1, -2)
            of_s, ob_s, _ = _hgrn_scan(proj, lb, s0t, B=Bs, T=Ts, row0=NP)
            mix_p = _hgrn_gate(of_p, ob_p, proj, hgrn_o_norm[j], row0=0)
            mix_s = _hgrn_gate(of_s, ob_s, proj, hgrn_o_norm[j], row0=NP)
            w_o = hgrn_wo
            hg_list.append(st_p)
        else:
            Hq, Hk = GQA_HEADS, GQA_KV_HEADS
            qkv = _mm(h, gqa_w_qkv, layer=j, out_dtype=F32)
            gains = jnp.concatenate([jnp.tile(gqa_q_norm[j][None], (Hq, 1)),
                                     jnp.tile(gqa_k_norm[j][None], (Hk, 1))], axis=0).reshape(Hq + Hk, 1, LANE)
            k_plain = _head_rope(qkv, gains[Hq:], cos_gqa, sin_gqa, col_block0=Hq, n_heads=Hk, norm=True,
                                 half=GQA_HD // 4, out_dtype=F32, n_rows=NP)
            qk = _head_rope(qkv, gains, cos_gqa, sin_gqa, col_block0=0, n_heads=Hq + Hk, norm=True,
                            half=GQA_HD // 4, out_dtype=BF16)
            kw = Hk * LANE
            k_new = qk[:, Hq * LANE:]
            v_new = qkv[:, (Hq + Hk) * LANE:].astype(BF16)
            k_s = jnp.concatenate([k_new[NP:].reshape(Bs, Ts, kw),
                                   cache_gqa_k[:, j].reshape(Bs, P, kw).astype(BF16)], axis=1)
            v_s = jnp.concatenate([v_new[NP:].reshape(Bs, Ts, kw),
                                   cache_gqa_v[:, j].reshape(Bs, P, kw).astype(BF16)], axis=1)
            k_all = jnp.concatenate([k_new[:NP], k_s.reshape(Bs * (Ts + P), kw)], axis=0)
            v_all = jnp.concatenate([v_new[:NP], v_s.reshape(Bs * (Ts + P), kw)], axis=0)
            scale = GQA_HD ** -0.5
            qp = [(qk, 0)]
            kp = [(k_all, 0, "group")]
            vp = (v_all, 0, "group")
            mix_p = _attention(qp, kp, vp, B=Bp, T=Tp, S=Tp, H=Hq, hb=Hq // Hk, q_row0=0, k_row0=0,
                               tq=256, tk=256, scale=scale)
            mix_s = _attention(qp, kp, vp, B=Bs, T=Ts, S=Ts + P, H=Hq, hb=Hq // Hk, q_row0=NP, k_row0=NP,
                               tq=512, tk=512, scale=scale)
            w_o = gqa_wo
            gk_list.append(k_plain[:NP].reshape(Bp, Tp, Hk, GQA_HD))
            gv_list.append(qkv[:NP, (Hq + Hk) * LANE:].reshape(Bp, Tp, Hk, GQA_HD))

        y = _mm_resid(mix_p, mix_s, w_o, y, nw[1], g1, layer=j, **rows)

        fi = layer // 2
        if layer % 2 == 0:
            h2 = _adaln_in(y, nw[2], sh2, sc2, tm=tm_tok, **rows)
            u = _gate_up(h2, ffn_w1, ffn_w3, fi)
            y = _down_resid(u, _cast_bf16(ffn_w2, fi), y, nw[3], g2, **rows)
        else:
            h2, sel, gw = _adaln_in(y, nw[2], sh2, sc2, tm=tm_tok, router=moe_router[fi], **rows)
            tm_e = _tile(M, 1024)
            tile_e, tile_rows, tile_first, src_tok, dst_row, gate_rows = _dispatch_tables(
                sel[:, :2], gw[:, :2], tm_e)
            x_sorted = _gather_sorted(h2, tile_rows, src_tok, tm=tm_e, D=D)
            u = _moe_gate_up(x_sorted, tile_e, tile_rows, tile_first, moe_w1, moe_w3, fi, tm=tm_e)
            slots = _moe_down(u, tile_e, tile_rows, dst_row, gate_rows, _cast_bf16(moe_w2, fi),
                              n_tok=M, tm=tm_e)
            y = _moe_combine(slots, y, nw[3], g2, **rows)

    return (y[:NP].reshape(Bp, Tp, D), y[NP:].reshape(Bs, Ts, D),
            jnp.stack(ckv_list, axis=1), jnp.stack(kpe_list, axis=1), jnp.stack(hg_list, axis=1),
            jnp.stack(gk_list, axis=1), jnp.stack(gv_list, axis=1))
```

```python
import functools
import math

import numpy as np
import jax
import jax.numpy as jnp
from jax import lax
from jax.experimental import pallas as pl
from jax.experimental.pallas import tpu as pltpu

F32 = jnp.float32
BF16 = jnp.bfloat16
EPS = 1e-6
LANE = 128

GRID_W = 64
ROPE_THETA = 10000.0
MLA_HEADS, MLA_NOPE, MLA_ROPE, MLA_V = 16, 128, 64, 128
HG_HEADS = 16
GQA_HEADS, GQA_KV_HEADS, GQA_HD = 16, 4, 128
N_EXPERTS = 8

VMEM_LIMIT_BYTES = 56 * 1024 * 1024

SDS = jax.ShapeDtypeStruct


def _cparams(*sem):
    return pltpu.CompilerParams(dimension_semantics=sem, vmem_limit_bytes=VMEM_LIMIT_BYTES)


def _tile(n, pref, step=8):
    t = min(pref, n)
    while n % t:
        t -= step
    return t


def _dot(a, b):
    return jnp.dot(a, b, preferred_element_type=F32)


def _dot_nt(a, b):
    return lax.dot_general(a, b, (((1,), (1,)), ((), ())), preferred_element_type=F32)


def _dot_tn(a, b):
    return lax.dot_general(a, b, (((0,), (0,)), ((), ())), preferred_element_type=F32)


def _rms(x):
    return x * lax.rsqrt(jnp.mean(x * x, axis=-1, keepdims=True) + EPS)


def _silu(x):
    return x * jax.nn.sigmoid(x)


def _row_tile(n_prompt, t_sample, pref):
    return _tile(math.gcd(n_prompt, t_sample), pref)


def _group_of(i, tm, n_prompt, t_sample):
    r = i * tm
    return jnp.where(r < n_prompt, 0, 1 + (r - n_prompt) // t_sample)


def _mod_kernel(c_ref, w_ref, b_ref, o_ref):
    s = _silu(c_ref[...]).astype(BF16)
    o_ref[0] = _dot(s, w_ref[0].astype(BF16)) + b_ref[0]


def _modulation(cond8, ada_w, ada_b):
    L, D, N6 = ada_w.shape
    tn = _tile(N6, 1024)
    return pl.pallas_call(
        _mod_kernel,
        out_shape=SDS((L, 8, N6), F32),
        grid=(L, N6 // tn),
        in_specs=[
            pl.BlockSpec((8, D), lambda l, j: (0, 0)),
            pl.BlockSpec((1, D, tn), lambda l, j: (l, 0, j)),
            pl.BlockSpec((1, 1, tn), lambda l, j: (l, 0, j)),
        ],
        out_specs=pl.BlockSpec((1, 8, tn), lambda l, j: (l, 0, j)),
        compiler_params=_cparams("parallel", "parallel"),
        name="modulation",
    )(cond8, ada_w, ada_b.reshape(L, 1, N6))


def _adaln_in_kernel(y_ref, g_ref, sh_ref, sc_ref, h_ref):
    h = (_rms(y_ref[...]) * g_ref[...]) * (1.0 + sc_ref[0]) + sh_ref[0]
    h_ref[...] = h.astype(h_ref.dtype)


def _to_slab(ref, x):
    rows, D = x.shape
    S = D // LANE
    for s in range(S):
        ref[pl.ds(s, rows, stride=S), :] = x[:, s * LANE:(s + 1) * LANE]


def _from_slab(ref, rows, s, S):
    return ref[pl.ds(s, rows, stride=S), :]


def _adaln_route_kernel(y_ref, g_ref, sh_ref, sc_ref, r_ref, h_ref, sel_ref, gw_ref):
    h = (_rms(y_ref[...]) * g_ref[...]) * (1.0 + sc_ref[0]) + sh_ref[0]
    _to_slab(h_ref, h)
    logits = lax.dot_general(h, r_ref[...], (((1,), (0,)), ((), ())),
                             precision=lax.Precision.HIGHEST, preferred_element_type=F32)
    lane = lax.broadcasted_iota(jnp.int32, logits.shape, 1)
    neg = jnp.float32(-jnp.inf)
    logits = jnp.where(lane < N_EXPERTS, logits, neg)
    lane_f = lane.astype(F32)
    m1 = jnp.max(logits, axis=-1, keepdims=True)
    i1 = jnp.min(jnp.where(logits == m1, lane_f, float(LANE)), axis=-1, keepdims=True)
    rest = jnp.where(lane_f == i1, neg, logits)
    m2 = jnp.max(rest, axis=-1, keepdims=True)
    i2 = jnp.min(jnp.where(rest == m2, lane_f, float(LANE)), axis=-1, keepdims=True)
    e = jnp.exp(m2 - m1)
    w1 = 1.0 / (1.0 + e)
    w2 = e / (1.0 + e)
    i1, i2 = i1.astype(jnp.int32), i2.astype(jnp.int32)
    sel_ref[...] = jnp.where(lane == 0, i1, jnp.where(lane == 1, i2, 0))
    gw_ref[...] = jnp.where(lane == 0, w1, jnp.where(lane == 1, w2, 0.0))


def _adaln_in(y, gain, shift, scale, *, n_prompt, t_sample, tm, router=None):
    M, D = y.shape
    grp = functools.partial(_group_of, tm=tm, n_prompt=n_prompt, t_sample=t_sample)
    in_specs = [
        pl.BlockSpec((tm, D), lambda i: (i, 0)),
        pl.BlockSpec((1, D), lambda i: (0, 0)),
        pl.BlockSpec((1, 1, D), lambda i: (grp(i), 0, 0)),
        pl.BlockSpec((1, 1, D), lambda i: (grp(i), 0, 0)),
    ]
    args = [y, gain.reshape(1, D), shift, scale]
    if router is None:
        return pl.pallas_call(
            _adaln_in_kernel,
            out_shape=SDS((M, D), BF16),
            grid=(M // tm,),
            in_specs=in_specs,
            out_specs=pl.BlockSpec((tm, D), lambda i: (i, 0)),
            compiler_params=_cparams("parallel"),
            name="adaln_in",
        )(*args)
    r_pad = jnp.pad(router, ((0, 0), (0, LANE - router.shape[1])))
    return pl.pallas_call(
        _adaln_route_kernel,
        out_shape=(SDS((M * (D // LANE), LANE), F32), SDS((M, LANE), jnp.int32), SDS((M, LANE), F32)),
        grid=(M // tm,),
        in_specs=in_specs + [pl.BlockSpec((D, LANE), lambda i: (0, 0))],
        out_specs=(
            pl.BlockSpec((tm * (D // LANE), LANE), lambda i: (i, 0)),
            pl.BlockSpec((tm, LANE), lambda i: (i, 0)),
            pl.BlockSpec((tm, LANE), lambda i: (i, 0)),
        ),
        compiler_params=_cparams("parallel"),
        name="adaln_route",
    )(*args, r_pad)


def _mm_kernel(x_ref, w_ref, o_ref, wbf):
    @pl.when(pl.program_id(1) == 0)
    def _():
        wbf[...] = w_ref[...].astype(BF16)

    o_ref[...] = _dot(x_ref[...].astype(BF16), wbf[...]).astype(o_ref.dtype)


def _layer_spec(w, layer, block, index_map):
    if layer is None:
        return pl.BlockSpec(block, index_map)
    lead = w.ndim - len(block) - 1
    return pl.BlockSpec((None,) * (lead + 1) + tuple(block),
                        lambda *a: (layer,) + (0,) * lead + tuple(index_map(*a)))


def _mm(x, w, *, out_dtype, layer=None, tm=1024, tn=1024):
    M, K = x.shape
    N = w.shape[-1]
    tm = _tile(M, tm)
    tn = _tile(N, tn, LANE)
    return pl.pallas_call(
        _mm_kernel,
        out_shape=SDS((M, N), out_dtype),
        grid=(N // tn, M // tm),
        in_specs=[
            pl.BlockSpec((tm, K), lambda j, i: (i, 0)),
            _layer_spec(w, layer, (K, tn), lambda j, i: (0, j)),
        ],
        out_specs=pl.BlockSpec((tm, tn), lambda j, i: (i, j)),
        scratch_shapes=[pltpu.VMEM((K, tn), BF16)],
        compiler_params=_cparams("parallel", "arbitrary"),
        name="mm",
    )(x, w)


def _mm_resid_kernel(xp_ref, xs_ref, w_ref, y_ref, g_ref, gate_ref, o_ref, wbf, *, n_ptiles):
    i = pl.program_id(0)

    @pl.when(i == 0)
    def _():
        wbf[...] = w_ref[...].astype(BF16)

    x = jnp.where(i < n_ptiles, xp_ref[...], xs_ref[...])
    a = _dot(x.astype(BF16), wbf[...])
    o_ref[...] = y_ref[...] + gate_ref[0] * (_rms(a) * g_ref[...])


def _mm_resid(x_p, x_s, w, y, gain, gate, *, layer, n_prompt, t_sample, tm=256):
    M, D = y.shape
    K = w.shape[-2]
    tm = _row_tile(n_prompt, t_sample, tm)
    n_ptiles = n_prompt // tm
    grp = functools.partial(_group_of, tm=tm, n_prompt=n_prompt, t_sample=t_sample)
    return pl.pallas_call(
        functools.partial(_mm_resid_kernel, n_ptiles=n_ptiles),
        out_shape=SDS((M, D), F32),
        grid=(M // tm,),
        in_specs=[
            pl.BlockSpec((tm, K), lambda i: (jnp.minimum(i, n_ptiles - 1), 0)),
            pl.BlockSpec((tm, K), lambda i: (jnp.maximum(i - n_ptiles, 0), 0)),
            pl.BlockSpec((None, K, D), lambda i: (layer, 0, 0), pipeline_mode=pl.Buffered(1)),
            pl.BlockSpec((tm, D), lambda i: (i, 0)),
            pl.BlockSpec((1, D), lambda i: (0, 0)),
            pl.BlockSpec((1, 1, D), lambda i: (grp(i), 0, 0)),
        ],
        out_specs=pl.BlockSpec((tm, D), lambda i: (i, 0)),
        scratch_shapes=[pltpu.VMEM((K, D), BF16)],
        compiler_params=_cparams("arbitrary"),
        name="mm_resid",
    )(x_p, x_s, w, y, gain.reshape(1, D), gate)


def _cast_kernel(x_ref, o_ref):
    o_ref[...] = x_ref[...].astype(o_ref.dtype)


def _cast_bf16(w, layer, *, tr=1024):
    shape = w.shape[1:]
    R = math.prod(shape[:-1])
    N = shape[-1]
    w2 = w.reshape(w.shape[0] * R, N)
    tr = _tile(R, tr)
    nb = R // tr
    out = pl.pallas_call(
        _cast_kernel,
        out_shape=SDS((R, N), BF16),
        grid=(nb,),
        in_specs=[pl.BlockSpec((tr, N), lambda i: (layer * nb + i, 0))],
        out_specs=pl.BlockSpec((tr, N), lambda i: (i, 0)),
        compiler_params=_cparams("parallel"),
        name="cast_bf16",
    )(w2)
    return out.reshape(shape)


def _gate_up_kernel(x_ref, w1_ref, w3_ref, u_ref, wb1, wb3):
    @pl.when(pl.program_id(1) == 0)
    def _():
        wb1[...] = w1_ref[...].astype(BF16)
        wb3[...] = w3_ref[...].astype(BF16)

    x = x_ref[...]
    u_ref[...] = (_silu(_dot(x, wb1[...])) * _dot(x, wb3[...])).astype(u_ref.dtype)


def _gate_up(x, w1, w3, layer, *, tm=1024, tn=512):
    M, D = x.shape
    F = w1.shape[-1]
    tm = _tile(M, tm)
    tn = _tile(F, tn, LANE)
    wspec = _layer_spec(w1, layer, (D, tn), lambda j, i: (0, j))
    return pl.pallas_call(
        _gate_up_kernel,
        out_shape=SDS((M, F), BF16),
        grid=(F // tn, M // tm),
        in_specs=[pl.BlockSpec((tm, D), lambda j, i: (i, 0)), wspec, wspec],
        out_specs=pl.BlockSpec((tm, tn), lambda j, i: (i, j)),
        scratch_shapes=[pltpu.VMEM((D, tn), BF16), pltpu.VMEM((D, tn), BF16)],
        compiler_params=_cparams("parallel", "arbitrary"),
        name="gate_up",
    )(x, w1, w3)


def _down_resid_kernel(u_ref, w_ref, y_ref, g_ref, gate_ref, o_ref, *, nk):
    k = pl.program_id(1)

    @pl.when(k == 0)
    def _():
        o_ref[...] = jnp.zeros_like(o_ref)

    o_ref[...] += _dot(u_ref[...], w_ref[...])

    @pl.when(k == nk - 1)
    def _():
        o_ref[...] = y_ref[...] + gate_ref[0] * (_rms(o_ref[...]) * g_ref[...])


def _down_resid(u, w2b, y, gain, gate, *, n_prompt, t_sample, tm=1024, tk=1408):
    M, D = y.shape
    F = u.shape[1]
    tm = _row_tile(n_prompt, t_sample, tm)
    tk = _tile(F, tk, LANE)
    nk = F // tk
    grp = functools.partial(_group_of, tm=tm, n_prompt=n_prompt, t_sample=t_sample)
    return pl.pallas_call(
        functools.partial(_down_resid_kernel, nk=nk),
        out_shape=SDS((M, D), F32),
        grid=(M // tm, nk),
        in_specs=[
            pl.BlockSpec((tm, tk), lambda i, k: (i, k)),
            pl.BlockSpec((tk, D), lambda i, k: (k, 0)),
            pl.BlockSpec((tm, D), lambda i, k: (i, 0), pipeline_mode=pl.Buffered(1)),
            pl.BlockSpec((1, D), lambda i, k: (0, 0)),
            pl.BlockSpec((1, 1, D), lambda i, k: (grp(i), 0, 0)),
        ],
        out_specs=pl.BlockSpec((tm, D), lambda i, k: (i, 0)),
        compiler_params=_cparams("parallel", "arbitrary"),
        name="down_resid",
    )(u, w2b, y, gain.reshape(1, D), gate)


def _gather_sorted_kernel(tr_ref, src_ref, h_hbm, o_ref, xs, sem, *, tm, S):
    i = pl.program_id(0)
    base = i * tm

    @pl.when(tr_ref[i] == 0)
    def _():
        o_ref[...] = jnp.zeros_like(o_ref)

    @pl.when(tr_ref[i] > 0)
    def _():
        def issue(r, c):
            s0 = pl.multiple_of(src_ref[base + r] * S, S)
            pltpu.make_async_copy(h_hbm.at[pl.ds(s0, S)], xs.at[pl.ds(pl.multiple_of(r * S, S), S)], sem).start()
            return c

        lax.fori_loop(0, tm, issue, 0)
        pltpu.make_async_copy(h_hbm.at[pl.ds(0, tm * S)], xs, sem).wait()
        for s in range(S):
            o_ref[:, s * LANE:(s + 1) * LANE] = _from_slab(xs, tm, s, S).astype(o_ref.dtype)


def _gather_sorted(h_slab, tile_rows, src_tok, *, tm, D):
    S = D // LANE
    nt = tile_rows.shape[0]
    grid_spec = pltpu.PrefetchScalarGridSpec(
        num_scalar_prefetch=2,
        grid=(nt,),
        in_specs=[pl.BlockSpec(memory_space=pl.ANY)],
        out_specs=pl.BlockSpec((tm, D), lambda i, tr, sr: (i, 0)),
        scratch_shapes=[pltpu.VMEM((tm * S, LANE), F32), pltpu.SemaphoreType.DMA(())],
    )
    return pl.pallas_call(
        functools.partial(_gather_sorted_kernel, tm=tm, S=S),
        out_shape=SDS((nt * tm, D), BF16),
        grid_spec=grid_spec,
        compiler_params=_cparams("arbitrary"),
        name="gather_sorted",
    )(tile_rows, src_tok, h_slab)


def _moe_gate_up_kernel(te_ref, tr_ref, tf_ref, x_ref, w1_ref, w3_ref, u_ref, wb1, wb3):
    i = pl.program_id(1)

    @pl.when(tf_ref[i] > 0)
    def _():
        wb1[...] = w1_ref[...].astype(BF16)
        wb3[...] = w3_ref[...].astype(BF16)

    @pl.when(tr_ref[i] > 0)
    def _():
        x = x_ref[...]
        u_ref[...] = (_silu(_dot(x, wb1[...])) * _dot(x, wb3[...])).astype(u_ref.dtype)

    @pl.when(tr_ref[i] == 0)
    def _():
        u_ref[...] = jnp.zeros_like(u_ref)


def _moe_gate_up(x_sorted, tile_e, tile_rows, tile_first, w1, w3, layer, *, tm, tn=512):
    R, D = x_sorted.shape
    F = w1.shape[-1]
    tn = _tile(F, tn, LANE)
    wspec = pl.BlockSpec((None, None, D, tn), lambda j, i, te, tr, tf: (layer, te[i], 0, j))
    grid_spec = pltpu.PrefetchScalarGridSpec(
        num_scalar_prefetch=3,
        grid=(F // tn, R // tm),
        in_specs=[pl.BlockSpec((tm, D), lambda j, i, te, tr, tf: (i, 0)), wspec, wspec],
        out_specs=pl.BlockSpec((tm, tn), lambda j, i, te, tr, tf: (i, j)),
        scratch_shapes=[pltpu.VMEM((D, tn), BF16), pltpu.VMEM((D, tn), BF16)],
    )
    return pl.pallas_call(
        _moe_gate_up_kernel,
        out_shape=SDS((R, F), BF16),
        grid_spec=grid_spec,
        compiler_params=_cparams("parallel", "arbitrary"),
        name="moe_gate_up",
    )(tile_e, tile_rows, tile_first, x_sorted, w1, w3)


def _moe_down_kernel(te_ref, tr_ref, dst_ref, u_ref, w_ref, gate_ref, out_hbm, acc, stage, sem, *, nk, tm, S):
    i = pl.program_id(0)
    k = pl.program_id(1)
    nrows = tr_ref[i]
    base = i * tm

    @pl.when(nrows > 0)
    def _():
        @pl.when(k == 0)
        def _():
            acc[...] = jnp.zeros_like(acc)

        acc[...] += _dot(u_ref[...], w_ref[0])

        @pl.when(k == nk - 1)
        def _():
            _to_slab(stage, acc[...] * gate_ref[...])

            def row_copy(r, d0):
                return pltpu.make_async_copy(stage.at[pl.ds(pl.multiple_of(r * S, S), S)],
                                             out_hbm.at[pl.ds(d0, S)], sem)

            def issue(r, c):
                row_copy(r, pl.multiple_of(dst_ref[base + r] * S, S)).start()
                return c

            def drain(r, c):
                row_copy(r, 0).wait()
                return c

            lax.fori_loop(0, nrows, issue, 0)
            lax.fori_loop(0, nrows, drain, 0)


def _moe_down(u_sorted, tile_e, tile_rows, dst_row, gate_rows, w2b, *, n_tok, tm, tk=1408):
    R, F = u_sorted.shape
    D = w2b.shape[2]
    S = D // LANE
    tk = _tile(F, tk, LANE)
    nk = F // tk

    def kidx(i, k, tr):
        return jnp.where(tr[i] > 0, k, nk - 1)

    grid_spec = pltpu.PrefetchScalarGridSpec(
        num_scalar_prefetch=3,
        grid=(R // tm, nk),
        in_specs=[
            pl.BlockSpec((tm, tk), lambda i, k, te, tr, ds: (i, kidx(i, k, tr))),
            pl.BlockSpec((1, tk, D), lambda i, k, te, tr, ds: (te[i], kidx(i, k, tr), 0)),
            pl.BlockSpec((tm, 1), lambda i, k, te, tr, ds: (i, 0)),
        ],
        out_specs=pl.BlockSpec(memory_space=pl.ANY),
        scratch_shapes=[pltpu.VMEM((tm, D), F32), pltpu.VMEM((tm * S, LANE), F32), pltpu.SemaphoreType.DMA(())],
    )
    return pl.pallas_call(
        functools.partial(_moe_down_kernel, nk=nk, tm=tm, S=S),
        out_shape=SDS((2 * n_tok * S, LANE), F32),
        grid_spec=grid_spec,
        compiler_params=pltpu.CompilerParams(dimension_semantics=("arbitrary", "arbitrary"),
                                             vmem_limit_bytes=VMEM_LIMIT_BYTES, has_side_effects=True),
        name="moe_down",
    )(tile_e, tile_rows, dst_row, u_sorted, w2b, gate_rows)


def _combine_kernel(a_ref, b_ref, y_ref, g_ref, gate_ref, o_ref, *, tm, S):
    parts = [_from_slab(a_ref, tm, s, S) + _from_slab(b_ref, tm, s, S) for s in range(S)]
    ss = parts[0] * parts[0]
    for t in parts[1:]:
        ss = ss + t * t
    r = lax.rsqrt(jnp.sum(ss, axis=-1, keepdims=True) / (S * LANE) + EPS)
    for s, t in enumerate(parts):
        c = slice(s * LANE, (s + 1) * LANE)
        o_ref[:, c] = y_ref[:, c] + gate_ref[0][:, c] * (t * r * g_ref[:, c])


def _moe_combine(slots, y, gain, gate, *, n_prompt, t_sample, tm=256):
    M, D = y.shape
    S = D // LANE
    tm = _row_tile(n_prompt, t_sample, tm)
    nb = M // tm
    grp = functools.partial(_group_of, tm=tm, n_prompt=n_prompt, t_sample=t_sample)
    return pl.pallas_call(
        functools.partial(_combine_kernel, tm=tm, S=S),
        out_shape=SDS((M, D), F32),
        grid=(nb,),
        in_specs=[
            pl.BlockSpec((tm * S, LANE), lambda i: (i, 0)),
            pl.BlockSpec((tm * S, LANE), lambda i: (nb + i, 0)),
            pl.BlockSpec((tm, D), lambda i: (i, 0)),
            pl.BlockSpec((1, D), lambda i: (0, 0)),
            pl.BlockSpec((1, 1, D), lambda i: (grp(i), 0, 0)),
        ],
        out_specs=pl.BlockSpec((tm, D), lambda i: (i, 0)),
        compiler_params=_cparams("parallel"),
        name="moe_combine",
    )(slots, slots, y, gain.reshape(1, D), gate)


def _dispatch_tables(sel, gw, tm):
    M = sel.shape[0]
    E = N_EXPERTS
    i32 = jnp.int32
    e_flat = sel.reshape(-1)
    pair = jnp.arange(2 * M, dtype=i32)
    onehot = (e_flat[:, None] == jnp.arange(E, dtype=i32)[None, :]).astype(i32)
    running = jnp.cumsum(onehot, axis=0)
    rank = jnp.sum(running * onehot, axis=1) - 1
    counts = running[-1]
    padded = ((counts + tm - 1) // tm) * tm
    ends_p = jnp.cumsum(padded)
    starts_p = ends_p - padded
    pos = starts_p[e_flat] + rank
    R = 2 * M + E * tm
    payload = jnp.stack([pair // 2, (pair % 2) * M + pair // 2,
                         lax.bitcast_convert_type(gw.reshape(-1), i32)], axis=1)
    table = jnp.zeros((R, 3), i32).at[pos].set(payload)
    src_tok, dst_row = table[:, 0], table[:, 1]
    gate_rows = lax.bitcast_convert_type(table[:, 2], F32)
    tile_start = jnp.arange(R // tm, dtype=i32) * tm
    tile_e = jnp.minimum(jnp.sum(ends_p[None, :] <= tile_start[:, None], axis=1), E - 1).astype(i32)
    tile_rows = jnp.clip(starts_p[tile_e] + counts[tile_e] - tile_start, 0, tm).astype(i32)
    tile_first = jnp.concatenate([jnp.ones((1,), i32), (tile_e[1:] != tile_e[:-1]).astype(i32)])
    return tile_e, tile_rows, tile_first, src_tok, dst_row, gate_rows.reshape(R, 1)


def _attn_kernel(*refs, n_parts, k_modes, v_mode, hb, scale, nk):
    q_refs = refs[:n_parts]
    k_refs = refs[n_parts:2 * n_parts]
    v_ref = refs[2 * n_parts]
    o_ref = refs[2 * n_parts + 1]
    m_sc, l_sc, acc_sc = refs[2 * n_parts + 2:]
    j = pl.program_id(3)
    c_exp = scale * math.log2(math.e)

    @pl.when(j == 0)
    def _():
        m_sc[...] = jnp.full_like(m_sc, -jnp.inf)
        l_sc[...] = jnp.zeros_like(l_sc)
        acc_sc[...] = jnp.zeros_like(acc_sc)

    def block(ref, mode, hh):
        x = ref[:, hh * LANE:(hh + 1) * LANE] if mode == "head" else ref[...]
        return x.astype(BF16)

    for hh in range(hb):
        q = [block(q_refs[p], "head", hh) for p in range(n_parts)]
        k = [block(k_refs[p], k_modes[p], hh) for p in range(n_parts)]
        q = q[0] if n_parts == 1 else jnp.concatenate(q, axis=1)
        k = k[0] if n_parts == 1 else jnp.concatenate(k, axis=1)
        s = _dot_nt(q, k)
        tiles = [s[:, c * LANE:(c + 1) * LANE] for c in range(s.shape[1] // LANE)]
        m_lane = tiles[0]
        for t in tiles[1:]:
            m_lane = jnp.maximum(m_lane, t)
        m_prev = m_sc[hh]
        m_new = jnp.maximum(m_prev, jnp.max(m_lane, axis=-1, keepdims=True))
        alpha = jnp.exp2((m_prev - m_new) * c_exp)
        p = [jnp.exp2((t - m_new) * c_exp) for t in tiles]
        l_lane = p[0]
        for t in p[1:]:
            l_lane = l_lane + t
        l_sc[hh] = alpha * l_sc[hh] + l_lane
        pb = jnp.concatenate([t.astype(BF16) for t in p], axis=1)
        acc_sc[hh] = alpha * acc_sc[hh] + _dot(pb, block(v_ref, v_mode, hh))
        m_sc[hh] = m_new

    @pl.when(j == nk - 1)
    def _():
        for hh in range(hb):
            l = jnp.sum(l_sc[hh], axis=-1, keepdims=True)
            o_ref[:, hh * LANE:(hh + 1) * LANE] = (acc_sc[hh] / l).astype(o_ref.dtype)


def _attention(q_parts, k_parts, v_part, *, B, T, S, H, hb, q_row0, k_row0, tq, tk, scale):
    tq = _tile(T, tq)
    tk = _tile(S, tk)
    nq, nk = T // tq, S // tk
    qb0, kb0 = q_row0 // tq, k_row0 // tk

    def q_spec(off):
        return pl.BlockSpec((tq, hb * LANE), lambda b, g, i, j: (qb0 + b * nq + i, off // hb + g))

    def k_spec(off, mode):
        if mode == "head":
            return pl.BlockSpec((tk, hb * LANE), lambda b, g, i, j: (kb0 + b * nk + j, off // hb + g))
        return pl.BlockSpec((tk, LANE), lambda b, g, i, j: (kb0 + b * nk + j, off + (g if mode == "group" else 0)))

    in_specs = [q_spec(off) for (_, off) in q_parts]
    in_specs += [k_spec(off, mode) for (_, off, mode) in k_parts]
    in_specs += [k_spec(v_part[1], v_part[2])]
    args = [a for (a, _) in q_parts] + [a for (a, _, _) in k_parts] + [v_part[0]]
    return pl.pallas_call(
        functools.partial(_attn_kernel, n_parts=len(q_parts), k_modes=[m for (_, _, m) in k_parts],
                          v_mode=v_part[2], hb=hb, scale=scale, nk=nk),
        out_shape=SDS((B * T, H * LANE), BF16),
        grid=(B, H // hb, nq, nk),
        in_specs=in_specs,
        out_specs=pl.BlockSpec((tq, hb * LANE), lambda b, g, i, j: (b * nq + i, g)),
        scratch_shapes=[pltpu.VMEM((hb, tq, LANE), F32), pltpu.VMEM((hb, tq, LANE), F32),
                        pltpu.VMEM((hb, tq, LANE), F32)],
        compiler_params=_cparams("parallel", "parallel", "parallel", "arbitrary"),
        name="attention",
    )(*args)


def _rmsnorm_cols_kernel(x_ref, g_ref, o_ref):
    o_ref[...] = (_rms(x_ref[...]) * g_ref[...]).astype(o_ref.dtype)


def _rmsnorm_cols(x, gain, *, col_block, width, out_dtype, tm=1024):
    M = x.shape[0]
    tm = _tile(M, tm)
    return pl.pallas_call(
        _rmsnorm_cols_kernel,
        out_shape=SDS((M, width), out_dtype),
        grid=(M // tm,),
        in_specs=[pl.BlockSpec((tm, width), lambda i: (i, col_block)),
                  pl.BlockSpec((1, width), lambda i: (0, 0))],
        out_specs=pl.BlockSpec((tm, width), lambda i: (i, 0)),
        compiler_params=_cparams("parallel"),
        name="rmsnorm_cols",
    )(x, gain.reshape(1, width))


def _head_rope_kernel(x_ref, g_ref, cos_ref, sin_ref, o_ref, *, norm, half):
    x = x_ref[...]
    if norm:
        x = _rms(x) * g_ref[0]
    lane = lax.broadcasted_iota(jnp.int32, x.shape, 1)
    first = (lane % (2 * half)) < half
    rot = jnp.where(first, -pltpu.roll(x, LANE - half, 1), pltpu.roll(x, half, 1))
    o_ref[...] = (x * cos_ref[...] + rot * sin_ref[...]).astype(o_ref.dtype)


def _head_rope(x, gains, cos, sin, *, col_block0, n_heads, norm, half, out_dtype, n_rows=None, tm=1024):
    M = x.shape[0] if n_rows is None else n_rows
    tm = _tile(M, tm)
    return pl.pallas_call(
        functools.partial(_head_rope_kernel, norm=norm, half=half),
        out_shape=SDS((M, n_heads * LANE), out_dtype),
        grid=(M // tm, n_heads),
        in_specs=[
            pl.BlockSpec((tm, LANE), lambda i, h: (i, col_block0 + h)),
            pl.BlockSpec((1, 1, LANE), lambda i, h: (h, 0, 0)),
            pl.BlockSpec((tm, LANE), lambda i, h: (i, 0)),
            pl.BlockSpec((tm, LANE), lambda i, h: (i, 0)),
        ],
        out_specs=pl.BlockSpec((tm, LANE), lambda i, h: (i, h)),
        compiler_params=_cparams("parallel", "parallel"),
        name="head_rope",
    )(x, gains, cos, sin)


def _rope_tables(n_prompt, dec_batch, t_sample, half):
    tok = jnp.arange(t_sample, dtype=jnp.int32)
    row, col = tok // GRID_W, tok % GRID_W
    inv = ROPE_THETA ** (-jnp.arange(half, dtype=F32) / half)

    def cs(pos):
        ang = pos.astype(F32)[:, None] * inv[None, :]
        return (jnp.concatenate([jnp.cos(ang), jnp.cos(ang)], -1),
                jnp.concatenate([jnp.sin(ang), jnp.sin(ang)], -1))

    cr, sr = cs(row)
    cc, sc = cs(col)
    pad = LANE - 4 * half
    cos = jnp.concatenate([cr, cc, jnp.ones((t_sample, pad), F32)], -1)
    sin = jnp.concatenate([sr, sc, jnp.zeros((t_sample, pad), F32)], -1)
    cos = jnp.concatenate([jnp.ones((n_prompt, LANE), F32), jnp.tile(cos, (dec_batch, 1))], 0)
    sin = jnp.concatenate([jnp.zeros((n_prompt, LANE), F32), jnp.tile(sin, (dec_batch, 1))], 0)
    return cos, sin


def _hgrn_constants(C, reverse):
    L = int(np.log2(C))
    t = np.arange(C)[:, None]
    u = np.arange(C)[None, :]
    Ws, Ms = [], []
    for lvl in range(L):
        b = 1 << lvl
        grp = t // (2 * b)
        if not reverse:
            bnd = grp * 2 * b + b - 1
            qside = (t % (2 * b)) >= b
            W = np.where(qside, (u > bnd) & (u <= t), (u > t) & (u <= bnd))
            kside_s = (u % (2 * b)) < b
        else:
            bnd = grp * 2 * b + b
            qside = (t % (2 * b)) < b
            W = np.where(qside, (u >= t) & (u < bnd), (u >= bnd) & (u < t))
            kside_s = (u % (2 * b)) >= b
        Ws.append(W)
        Ms.append(qside & kside_s & (grp == u // (2 * b)))
    if not reverse:
        Ws += [u <= t, u > t]
    else:
        Ws += [u >= t, u < t]
    Ms.append(t == u)
    W_all = jnp.asarray(np.concatenate(Ws, 0).astype(np.float32), BF16)
    M_all = jnp.asarray(np.stack(Ms).astype(np.float32), F32)
    return W_all, M_all


def _hgrn_chain(q, v, z, lb, W_all, m_ref, st, *, C, L, edge_row):
    f = lb + (1.0 - lb) * jax.nn.sigmoid(z)
    k = 1.0 - f
    lf = jnp.log(f)
    hi = lf.astype(BF16)
    r1 = lf - hi.astype(F32)
    mid = r1.astype(BF16)
    lo = (r1 - mid.astype(F32)).astype(BF16)
    d3 = _dot(W_all, jnp.concatenate([hi, mid, lo], axis=1))
    dn = d3[:, :LANE] + d3[:, LANE:2 * LANE] + d3[:, 2 * LANE:]
    e_all = jnp.exp(dn)
    qb = q.astype(BF16)
    a = m_ref[L] * _dot_nt(qb, k.astype(BF16))
    for lvl in range(L):
        e = e_all[lvl * C:(lvl + 1) * C]
        a = a + m_ref[lvl] * _dot_nt((q * e).astype(BF16), (k * e).astype(BF16))
    eq = e_all[L * C:(L + 1) * C]
    ek = e_all[(L + 1) * C:(L + 2) * C]
    vb = v.astype(BF16)
    o = _dot(a.astype(BF16), vb) + _dot_nt((q * eq).astype(BF16), st.astype(BF16))
    g_edge = dn[L * C + edge_row:L * C + edge_row + 1]
    st_new = st * jnp.exp(g_edge) + _dot_tn(vb, (k * ek).astype(BF16))
    return o, st_new


def _hgrn_kernel(*refs, C, L, nc, hb, has_init):
    (qf_ref, vf_ref, zf_ref, qb_ref, vb_ref, zb_ref, lbf_ref, lbb_ref,
     wf_ref, mf_ref, wb_ref, mb_ref) = refs[:12]
    rest = refs[12:]
    if has_init:
        s0_ref, rest = rest[0], rest[1:]
    of_ref, ob_ref, sfin_ref, stf, stb = rest
    c = pl.program_id(2)

    @pl.when(c == 0)
    def _():
        for hh in range(hb):
            if has_init:
                stf[hh] = s0_ref[0, 0, hh]
                stb[hh] = s0_ref[0, 1, hh]
            else:
                stf[hh] = jnp.zeros((LANE, LANE), F32)
                stb[hh] = jnp.zeros((LANE, LANE), F32)

    for hh in range(hb):
        cols = slice(hh * LANE, (hh + 1) * LANE)
        o, s_new = _hgrn_chain(qf_ref[:, cols], vf_ref[:, cols], zf_ref[:, cols], lbf_ref[0][:, cols],
                               wf_ref[...], mf_ref, stf[hh], C=C, L=L, edge_row=C - 1)
        of_ref[:, cols] = o
        stf[hh] = s_new
        o, s_new = _hgrn_chain(qb_ref[:, cols], vb_ref[:, cols], zb_ref[:, cols], lbb_ref[0][:, cols],
                               wb_ref[...], mb_ref, stb[hh], C=C, L=L, edge_row=0)
        ob_ref[:, cols] = o
        stb[hh] = s_new

    @pl.when(c == nc - 1)
    def _():
        for hh in range(hb):
            sfin_ref[0, 0, hh] = stf[hh].T
            sfin_ref[0, 1, hh] = stb[hh].T


def _hgrn_scan(proj, lb, s0t, *, B, T, row0, C=128, hb=2):
    D = proj.shape[1] // 5
    H = D // LANE
    C = _tile(T, C)
    L = int(np.log2(C))
    assert 1 << L == C
    nc = T // C
    rb0 = row0 // C
    HB = H // hb
    W = hb * LANE
    wf, mf = _hgrn_constants(C, False)
    wb, mb = _hgrn_constants(C, True)

    def fwd(col0):
        return pl.BlockSpec((C, W), lambda b, g, c: (rb0 + b * nc + c, col0 // hb + g))

    def bwd(col0):
        return pl.BlockSpec((C, W), lambda b, g, c: (rb0 + b * nc + nc - 1 - c, col0 // hb + g))

    def const(a):
        return pl.BlockSpec(a.shape, lambda b, g, c: (0,) * a.ndim)

    in_specs = [fwd(0), fwd(H), fwd(3 * H), bwd(0), bwd(H), bwd(4 * H),
                pl.BlockSpec((1, 1, W), lambda b, g, c: (0, 0, g)),
                pl.BlockSpec((1, 1, W), lambda b, g, c: (1, 0, g)),
                const(wf), const(mf), const(wb), const(mb)]
    args = [proj, proj, proj, proj, proj, proj, lb, lb, wf, mf, wb, mb]
    if s0t is not None:
        in_specs.append(pl.BlockSpec((1, 2, hb, LANE, LANE), lambda b, g, c: (b, 0, g, 0, 0)))
        args.append(s0t)
    return pl.pallas_call(
        functools.partial(_hgrn_kernel, C=C, L=L, nc=nc, hb=hb, has_init=s0t is not None),
        out_shape=(SDS((B * T, D), F32), SDS((B * T, D), F32), SDS((B, 2, H, LANE, LANE), F32)),
        grid=(B, HB, nc),
        in_specs=in_specs,
        out_specs=(
            pl.BlockSpec((C, W), lambda b, g, c: (b * nc + c, g)),
            pl.BlockSpec((C, W), lambda b, g, c: (b * nc + nc - 1 - c, g)),
            pl.BlockSpec((1, 2, hb, LANE, LANE), lambda b, g, c: (b, 0, g, 0, 0)),
        ),
        scratch_shapes=[pltpu.VMEM((hb, LANE, LANE), F32), pltpu.VMEM((hb, LANE, LANE), F32)],
        compiler_params=_cparams("parallel", "parallel", "arbitrary"),
        name="hgrn_scan",
    )(*args)


def _hgrn_gate_kernel(of_ref, ob_ref, g_ref, w_ref, o_ref):
    o = _rms(of_ref[...] + ob_ref[...]) * w_ref[0]
    o_ref[...] = (o * _silu(g_ref[...])).astype(o_ref.dtype)


def _hgrn_gate(o_f, o_b, proj, o_norm, *, row0, tm=1024):
    M, D = o_f.shape
    H = D // LANE
    tm = _tile(math.gcd(M, row0) if row0 else M, tm)
    rb0 = row0 // tm
    blk = pl.BlockSpec((tm, LANE), lambda i, h: (i, h))
    return pl.pallas_call(
        _hgrn_gate_kernel,
        out_shape=SDS((M, D), BF16),
        grid=(M // tm, H),
        in_specs=[blk, blk,
                  pl.BlockSpec((tm, LANE), lambda i, h: (rb0 + i, 2 * H + h)),
                  pl.BlockSpec((1, 1, LANE), lambda i, h: (h, 0, 0))],
        out_specs=blk,
        compiler_params=_cparams("parallel", "parallel"),
        name="hgrn_gate",
    )(o_f, o_b, proj, o_norm.reshape(H, 1, LANE))


def kernel(x_prompt, x_sample, c, cache_mla_ckv, cache_mla_kpe, state_hgrn, cache_gqa_k, cache_gqa_v, c_ctx, ada_w, ada_b, norm_w, mla_wq_a, mla_q_norm, mla_wq_b, mla_wkv_a, mla_kv_norm, mla_wkv_b, mla_wo, hgrn_w_in, hgrn_lb_logits, hgrn_o_norm, hgrn_wo, gqa_w_qkv, gqa_q_norm, gqa_k_norm, gqa_wo, ffn_w1, ffn_w3, ffn_w2, moe_router, moe_w1, moe_w3, moe_w2):
    Bp, Tp, D = x_prompt.shape
    Bs, Ts, _ = x_sample.shape
    P = cache_mla_ckv.shape[2]
    depth = ada_w.shape[0]
    NP, NS = Bp * Tp, Bs * Ts
    M = NP + NS
    G = 1 + Bs
    assert D == HG_HEADS * LANE
    rows = dict(n_prompt=NP, t_sample=Ts)

    y = jnp.concatenate([x_prompt.reshape(NP, D), x_sample.reshape(NS, D)], axis=0)
    cond8 = jnp.zeros((8, D), F32).at[0].set(c_ctx).at[1:G].set(c)
    mod = _modulation(cond8, ada_w, ada_b).reshape(depth, 8, 6, D)
    mod = jnp.transpose(mod, (0, 2, 1, 3))[:, :, :G, None, :]

    tm_tok = _row_tile(NP, Ts, 512)
    cos_mla, sin_mla = _rope_tables(NP, Bs, Ts, MLA_ROPE // 4)
    cos_gqa, sin_gqa = _rope_tables(NP, Bs, Ts, GQA_HD // 4)
    ones_g = jnp.ones((MLA_HEADS, 1, LANE), F32)

    ckv_list, kpe_list, hg_list, gk_list, gv_list = [], [], [], [], []
    for layer in range(depth):
        kind, j = layer % 3, layer // 3
        nw = norm_w[layer]
        sh1, sc1, g1, sh2, sc2, g2 = (mod[layer, k] for k in range(6))
        h = _adaln_in(y, nw[0], sh1, sc1, tm=tm_tok, **rows)

        if kind == 0:
            Hm = MLA_HEADS
            w_a = jnp.concatenate([mla_wq_a[j], mla_wkv_a[j],
                                   jnp.zeros((D, LANE - MLA_ROPE), F32)], axis=1)
            a = _mm(h, w_a, out_dtype=F32)
            ql, kvl = mla_wq_a.shape[2], mla_kv_norm.shape[1]
            qn = _rmsnorm_cols(a, mla_q_norm[j], col_block=0, width=ql, out_dtype=BF16)
            ckv = _rmsnorm_cols(a, mla_kv_norm[j], col_block=ql // kvl, width=kvl, out_dtype=F32)
            wqb = mla_wq_b[j].reshape(ql, Hm, MLA_NOPE + MLA_ROPE)
            wqb = jnp.concatenate([
                wqb[:, :, :MLA_NOPE].reshape(ql, Hm * MLA_NOPE),
                jnp.pad(wqb[:, :, MLA_NOPE:], ((0, 0), (0, 0), (0, LANE - MLA_ROPE))).reshape(ql, Hm * LANE),
            ], axis=1)
            q = _mm(qn, wqb, out_dtype=F32)
            pe_blk = (ql + kvl) // LANE
            q_pe = _head_rope(q, ones_g, cos_mla, sin_mla, col_block0=Hm, n_heads=Hm, norm=False,
                              half=MLA_ROPE // 4, out_dtype=BF16)
            k_pe = _head_rope(a, ones_g, cos_mla, sin_mla, col_block0=pe_blk, n_heads=1, norm=False,
                              half=MLA_ROPE // 4, out_dtype=BF16)
            wkvb = mla_wkv_b[j].reshape(kvl, Hm, MLA_NOPE + MLA_V)
            wkvb = jnp.concatenate([wkvb[:, :, :MLA_NOPE].reshape(kvl, Hm * MLA_NOPE),
                                    wkvb[:, :, MLA_NOPE:].reshape(kvl, Hm * MLA_V)], axis=1)
            ckv_s = jnp.concatenate([ckv[NP:].reshape(Bs, Ts, kvl), cache_mla_ckv[:, j]], axis=1)
            c_all = jnp.concatenate([ckv[:NP], ckv_s.reshape(Bs * (Ts + P), kvl)], axis=0)
            kv = _mm(c_all, wkvb, out_dtype=BF16)
            kpe_ctx = jnp.pad(cache_mla_kpe[:, j], ((0, 0), (0, 0), (0, LANE - MLA_ROPE))).astype(BF16)
            kpe_s = jnp.concatenate([k_pe[NP:].reshape(Bs, Ts, LANE), kpe_ctx], axis=1)
            kpe_all = jnp.concatenate([k_pe[:NP], kpe_s.reshape(Bs * (Ts + P), LANE)], axis=0)
            scale = (MLA_NOPE + MLA_ROPE) ** -0.5
            qp = [(q, 0), (q_pe, 0)]
            kp = [(kv, 0, "head"), (kpe_all, 0, "shared")]
            vp = (kv, Hm, "head")
            mix_p = _attention(qp, kp, vp, B=Bp, T=Tp, S=Tp, H=Hm, hb=4, q_row0=0, k_row0=0,
                               tq=256, tk=256, scale=scale)
            mix_s = _attention(qp, kp, vp, B=Bs, T=Ts, S=Ts + P, H=Hm, hb=4, q_row0=NP, k_row0=NP,
                               tq=512, tk=512, scale=scale)
            w_o = mla_wo
            ckv_list.append(ckv[:NP].reshape(Bp, Tp, kvl))
            kpe_list.append(a[:NP, ql + kvl:ql + kvl + MLA_ROPE].reshape(Bp, Tp, MLA_ROPE))
        elif kind == 1:
            H = HG_HEADS
            p = jax.nn.softmax(hgrn_lb_logits.astype(F32), axis=1)
            cum = jnp.cumsum(p, axis=1)
            lb = (cum - cum[:, :1])[:, layer].reshape(2, 1, D)
            proj = _mm(h, hgrn_w_in, layer=j, out_dtype=F32)
            of_p, ob_p, st_p = _hgrn_scan(proj, lb, None, B=Bp, T=Tp, row0=0)
            s0t = jnp.swapaxes(state_hgrn[:, j].astype(F32), -1, -2)
            of_s, ob_s, _ = _hgrn_scan(proj, lb, s0t, B=Bs, T=Ts, row0=NP)
            mix_p = _hgrn_gate(of_p, ob_p, proj, hgrn_o_norm[j], row0=0)
            mix_s = _hgrn_gate(of_s, ob_s, proj, hgrn_o_norm[j], row0=NP)
            w_o = hgrn_wo
            hg_list.append(st_p)
        else:
            Hq, Hk = GQA_HEADS, GQA_KV_HEADS
            qkv = _mm(h, gqa_w_qkv, layer=j, out_dtype=F32)
            gains = jnp.concatenate([jnp.tile(gqa_q_norm[j][None], (Hq, 1)),
                                     jnp.tile(gqa_k_norm[j][None], (Hk, 1))], axis=0).reshape(Hq + Hk, 1, LANE)
            k_plain = _head_rope(qkv, gains[Hq:], cos_gqa, sin_gqa, col_block0=Hq, n_heads=Hk, norm=True,
                                 half=GQA_HD // 4, out_dtype=F32, n_rows=NP)
            qk = _head_rope(qkv, gains, cos_gqa, sin_gqa, col_block0=0, n_heads=Hq + Hk, norm=True,
                            half=GQA_HD // 4, out_dtype=BF16)
            kw = Hk * LANE
            k_new = qk[:, Hq * LANE:]
            v_new = qkv[:, (Hq + Hk) * LANE:].astype(BF16)
            k_s = jnp.concatenate([k_new[NP:].reshape(Bs, Ts, kw),
                                   cache_gqa_k[:, j].reshape(Bs, P, kw).astype(BF16)], axis=1)
            v_s = jnp.concatenate([v_new[NP:].reshape(Bs, Ts, kw),
                                   cache_gqa_v[:, j].reshape(Bs, P, kw).astype(BF16)], axis=1)
            k_all = jnp.concatenate([k_new[:NP], k_s.reshape(Bs * (Ts + P), kw)], axis=0)
            v_all = jnp.concatenate([v_new[:NP], v_s.reshape(Bs * (Ts + P), kw)], axis=0)
            scale = GQA_HD ** -0.5
            qp = [(qk, 0)]
            kp = [(k_all, 0, "group")]
            vp = (v_all, 0, "group")
            mix_p = _attention(qp, kp, vp, B=Bp, T=Tp, S=Tp, H=Hq, hb=Hq // Hk, q_row0=0, k_row0=0,
                               tq=256, tk=256, scale=scale)
            mix_s = _attention(qp, kp, vp, B=Bs, T=Ts, S=Ts + P, H=Hq, hb=Hq // Hk, q_row0=NP, k_row0=NP,
                               tq=512, tk=512, scale=scale)
            w_o = gqa_wo
            gk_list.append(k_plain[:NP].reshape(Bp, Tp, Hk, GQA_HD))
            gv_list.append(qkv[:NP, (Hq + Hk) * LANE:].reshape(Bp, Tp, Hk, GQA_HD))

        y = _mm_resid(mix_p, mix_s, w_o, y, nw[1], g1, layer=j, **rows)

        fi = layer // 2
        if layer % 2 == 0:
            h2 = _adaln_in(y, nw[2], sh2, sc2, tm=tm_tok, **rows)
            u = _gate_up(h2, ffn_w1, ffn_w3, fi)
            y = _down_resid(u, _cast_bf16(ffn_w2, fi), y, nw[3], g2, **rows)
        else:
            h2, sel, gw = _adaln_in(y, nw[2], sh2, sc2, tm=tm_tok, router=moe_router[fi], **rows)
            tm_e = _tile(M, 1024)
            tile_e, tile_rows, tile_first, src_tok, dst_row, gate_rows = _dispatch_tables(
                sel[:, :2], gw[:, :2], tm_e)
            x_sorted = _gather_sorted(h2, tile_rows, src_tok, tm=tm_e, D=D)
            u = _moe_gate_up(x_sorted, tile_e, tile_rows, tile_first, moe_w1, moe_w3, fi, tm=tm_e)
            slots = _moe_down(u, tile_e, tile_rows, dst_row, gate_rows, _cast_bf16(moe_w2, fi),
                              n_tok=M, tm=tm_e)
            y = _moe_combine(slots, y, nw[3], g2, **rows)

    return (y[:NP].reshape(Bp, Tp, D), y[NP:].reshape(Bs, Ts, D),
            jnp.stack(ckv_list, axis=1), jnp.stack(kpe_list, axis=1), jnp.stack(hg_list, axis=1),
            jnp.stack(gk_list, axis=1), jnp.stack(gv_list, axis=1))
```

```python
import functools
import math

import numpy as np
import jax
import jax.numpy as jnp
from jax import lax
from jax.experimental import pallas as pl
from jax.experimental.pallas import tpu as pltpu

F32 = jnp.float32
BF16 = jnp.bfloat16
EPS = 1e-6
LANE = 128

GRID_W = 64
ROPE_THETA = 10000.0
MLA_HEADS, MLA_NOPE, MLA_ROPE, MLA_V = 16, 128, 64, 128
HG_HEADS = 16
GQA_HEADS, GQA_KV_HEADS, GQA_HD = 16, 4, 128
N_EXPERTS = 8

VMEM_LIMIT_BYTES = 56 * 1024 * 1024

SDS = jax.ShapeDtypeStruct


def _cparams(*sem):
    return pltpu.CompilerParams(dimension_semantics=sem, vmem_limit_bytes=VMEM_LIMIT_BYTES)


def _tile(n, pref, step=8):
    t = min(pref, n)
    while n % t:
        t -= step
    return t


def _dot(a, b):
    return jnp.dot(a, b, preferred_element_type=F32)


def _dot_nt(a, b):
    return lax.dot_general(a, b, (((1,), (1,)), ((), ())), preferred_element_type=F32)


def _dot_tn(a, b):
    return lax.dot_general(a, b, (((0,), (0,)), ((), ())), preferred_element_type=F32)


def _rms(x):
    return x * lax.rsqrt(jnp.mean(x * x, axis=-1, keepdims=True) + EPS)


def _silu(x):
    return x * jax.nn.sigmoid(x)


def _row_tile(n_prompt, t_sample, pref):
    return _tile(math.gcd(n_prompt, t_sample), pref)


def _group_of(i, tm, n_prompt, t_sample):
    r = i * tm
    return jnp.where(r < n_prompt, 0, 1 + (r - n_prompt) // t_sample)


def _mod_kernel(c_ref, w_ref, b_ref, o_ref):
    s = _silu(c_ref[...]).astype(BF16)
    o_ref[0] = _dot(s, w_ref[0].astype(BF16)) + b_ref[0]


def _modulation(cond8, ada_w, ada_b):
    L, D, N6 = ada_w.shape
    tn = _tile(N6, 1024)
    return pl.pallas_call(
        _mod_kernel,
        out_shape=SDS((L, 8, N6), F32),
        grid=(L, N6 // tn),
        in_specs=[
            pl.BlockSpec((8, D), lambda l, j: (0, 0)),
            pl.BlockSpec((1, D, tn), lambda l, j: (l, 0, j)),
            pl.BlockSpec((1, 1, tn), lambda l, j: (l, 0, j)),
        ],
        out_specs=pl.BlockSpec((1, 8, tn), lambda l, j: (l, 0, j)),
        compiler_params=_cparams("parallel", "parallel"),
        name="modulation",
    )(cond8, ada_w, ada_b.reshape(L, 1, N6))


def _adaln_in_kernel(y_ref, g_ref, sh_ref, sc_ref, h_ref):
    h = (_rms(y_ref[...]) * g_ref[...]) * (1.0 + sc_ref[0]) + sh_ref[0]
    h_ref[...] = h.astype(h_ref.dtype)


def _to_slab(ref, x):
    rows, D = x.shape
    S = D // LANE
    for s in range(S):
        ref[pl.ds(s, rows, stride=S), :] = x[:, s * LANE:(s + 1) * LANE]


def _from_slab(ref, rows, s, S):
    return ref[pl.ds(s, rows, stride=S), :]


def _adaln_route_kernel(y_ref, g_ref, sh_ref, sc_ref, r_ref, h_ref, sel_ref, gw_ref):
    h = (_rms(y_ref[...]) * g_ref[...]) * (1.0 + sc_ref[0]) + sh_ref[0]
    _to_slab(h_ref, h)
    logits = lax.dot_general(h, r_ref[...], (((1,), (0,)), ((), ())),
                             precision=lax.Precision.HIGHEST, preferred_element_type=F32)
    lane = lax.broadcasted_iota(jnp.int32, logits.shape, 1)
    neg = jnp.float32(-jnp.inf)
    logits = jnp.where(lane < N_EXPERTS, logits, neg)
    lane_f = lane.astype(F32)
    m1 = jnp.max(logits, axis=-1, keepdims=True)
    i1 = jnp.min(jnp.where(logits == m1, lane_f, float(LANE)), axis=-1, keepdims=True)
    rest = jnp.where(lane_f == i1, neg, logits)
    m2 = jnp.max(rest, axis=-1, keepdims=True)
    i2 = jnp.min(jnp.where(rest == m2, lane_f, float(LANE)), axis=-1, keepdims=True)
    e = jnp.exp(m2 - m1)
    w1 = 1.0 / (1.0 + e)
    w2 = e / (1.0 + e)
    i1, i2 = i1.astype(jnp.int32), i2.astype(jnp.int32)
    sel_ref[...] = jnp.where(lane == 0, i1, jnp.where(lane == 1, i2, 0))
    gw_ref[...] = jnp.where(lane == 0, w1, jnp.where(lane == 1, w2, 0.0))


def _adaln_in(y, gain, shift, scale, *, n_prompt, t_sample, tm, router=None):
    M, D = y.shape
    grp = functools.partial(_group_of, tm=tm, n_prompt=n_prompt, t_sample=t_sample)
    in_specs = [
        pl.BlockSpec((tm, D), lambda i: (i, 0)),
        pl.BlockSpec((1, D), lambda i: (0, 0)),
        pl.BlockSpec((1, 1, D), lambda i: (grp(i), 0, 0)),
        pl.BlockSpec((1, 1, D), lambda i: (grp(i), 0, 0)),
    ]
    args = [y, gain.reshape(1, D), shift, scale]
    if router is None:
        return pl.pallas_call(
            _adaln_in_kernel,
            out_shape=SDS((M, D), BF16),
            grid=(M // tm,),
            in_specs=in_specs,
            out_specs=pl.BlockSpec((tm, D), lambda i: (i, 0)),
            compiler_params=_cparams("parallel"),
            name="adaln_in",
        )(*args)
    r_pad = jnp.pad(router, ((0, 0), (0, LANE - router.shape[1])))
    return pl.pallas_call(
        _adaln_route_kernel,
        out_shape=(SDS((M * (D // LANE), LANE), F32), SDS((M, LANE), jnp.int32), SDS((M, LANE), F32)),
        grid=(M // tm,),
        in_specs=in_specs + [pl.BlockSpec((D, LANE), lambda i: (0, 0))],
        out_specs=(
            pl.BlockSpec((tm * (D // LANE), LANE), lambda i: (i, 0)),
            pl.BlockSpec((tm, LANE), lambda i: (i, 0)),
            pl.BlockSpec((tm, LANE), lambda i: (i, 0)),
        ),
        compiler_params=_cparams("parallel"),
        name="adaln_route",
    )(*args, r_pad)


def _mm_kernel(x_ref, w_ref, o_ref, wbf):
    @pl.when(pl.program_id(1) == 0)
    def _():
        wbf[...] = w_ref[...].astype(BF16)

    o_ref[...] = _dot(x_ref[...].astype(BF16), wbf[...]).astype(o_ref.dtype)


def _layer_spec(w, layer, block, index_map):
    if layer is None:
        return pl.BlockSpec(block, index_map)
    lead = w.ndim - len(block) - 1
    return pl.BlockSpec((None,) * (lead + 1) + tuple(block),
                        lambda *a: (layer,) + (0,) * lead + tuple(index_map(*a)))


def _mm(x, w, *, out_dtype, layer=None, tm=1024, tn=1024):
    M, K = x.shape
    N = w.shape[-1]
    tm = _tile(M, tm)
    tn = _tile(N, tn, LANE)
    return pl.pallas_call(
        _mm_kernel,
        out_shape=SDS((M, N), out_dtype),
        grid=(N // tn, M // tm),
        in_specs=[
            pl.BlockSpec((tm, K), lambda j, i: (i, 0)),
            _layer_spec(w, layer, (K, tn), lambda j, i: (0, j)),
        ],
        out_specs=pl.BlockSpec((tm, tn), lambda j, i: (i, j)),
        scratch_shapes=[pltpu.VMEM((K, tn), BF16)],
        compiler_params=_cparams("parallel", "arbitrary"),
        name="mm",
    )(x, w)


def _mm_resid_kernel(xp_ref, xs_ref, w_ref, y_ref, g_ref, gate_ref, o_ref, wbf, *, n_ptiles):
    i = pl.program_id(0)

    @pl.when(i == 0)
    def _():
        wbf[...] = w_ref[...].astype(BF16)

    x = jnp.where(i < n_ptiles, xp_ref[...], xs_ref[...])
    a = _dot(x.astype(BF16), wbf[...])
    o_ref[...] = y_ref[...] + gate_ref[0] * (_rms(a) * g_ref[...])


def _mm_resid(x_p, x_s, w, y, gain, gate, *, layer, n_prompt, t_sample, tm=256):
    M, D = y.shape
    K = w.shape[-2]
    tm = _row_tile(n_prompt, t_sample, tm)
    n_ptiles = n_prompt // tm
    grp = functools.partial(_group_of, tm=tm, n_prompt=n_prompt, t_sample=t_sample)
    return pl.pallas_call(
        functools.partial(_mm_resid_kernel, n_ptiles=n_ptiles),
        out_shape=SDS((M, D), F32),
        grid=(M // tm,),
        in_specs=[
            pl.BlockSpec((tm, K), lambda i: (jnp.minimum(i, n_ptiles - 1), 0)),
            pl.BlockSpec((tm, K), lambda i: (jnp.maximum(i - n_ptiles, 0), 0)),
            pl.BlockSpec((None, K, D), lambda i: (layer, 0, 0), pipeline_mode=pl.Buffered(1)),
            pl.BlockSpec((tm, D), lambda i: (i, 0)),
            pl.BlockSpec((1, D), lambda i: (0, 0)),
            pl.BlockSpec((1, 1, D), lambda i: (grp(i), 0, 0)),
        ],
        out_specs=pl.BlockSpec((tm, D), lambda i: (i, 0)),
        scratch_shapes=[pltpu.VMEM((K, D), BF16)],
        compiler_params=_cparams("arbitrary"),
        name="mm_resid",
    )(x_p, x_s, w, y, gain.reshape(1, D), gate)


def _cast_kernel(x_ref, o_ref):
    o_ref[...] = x_ref[...].astype(o_ref.dtype)


def _cast_bf16(w, layer, *, tr=1024):
    shape = w.shape[1:]
    R = math.prod(shape[:-1])
    N = shape[-1]
    w2 = w.reshape(w.shape[0] * R, N)
    tr = _tile(R, tr)
    nb = R // tr
    out = pl.pallas_call(
        _cast_kernel,
        out_shape=SDS((R, N), BF16),
        grid=(nb,),
        in_specs=[pl.BlockSpec((tr, N), lambda i: (layer * nb + i, 0))],
        out_specs=pl.BlockSpec((tr, N), lambda i: (i, 0)),
        compiler_params=_cparams("parallel"),
        name="cast_bf16",
    )(w2)
    return out.reshape(shape)


def _gate_up_kernel(x_ref, w1_ref, w3_ref, u_ref, wb1, wb3):
    @pl.when(pl.program_id(1) == 0)
    def _():
        wb1[...] = w1_ref[...].astype(BF16)
        wb3[...] = w3_ref[...].astype(BF16)

    x = x_ref[...]
    u_ref[...] = (_silu(_dot(x, wb1[...])) * _dot(x, wb3[...])).astype(u_ref.dtype)


def _gate_up(x, w1, w3, layer, *, tm=1024, tn=512):
    M, D = x.shape
    F = w1.shape[-1]
    tm = _tile(M, tm)
    tn = _tile(F, tn, LANE)
    wspec = _layer_spec(w1, layer, (D, tn), lambda j, i: (0, j))
    return pl.pallas_call(
        _gate_up_kernel,
        out_shape=SDS((M, F), BF16),
        grid=(F // tn, M // tm),
        in_specs=[pl.BlockSpec((tm, D), lambda j, i: (i, 0)), wspec, wspec],
        out_specs=pl.BlockSpec((tm, tn), lambda j, i: (i, j)),
        scratch_shapes=[pltpu.VMEM((D, tn), BF16), pltpu.VMEM((D, tn), BF16)],
        compiler_params=_cparams("parallel", "arbitrary"),
        name="gate_up",
    )(x, w1, w3)


def _down_resid_kernel(u_ref, w_ref, y_ref, g_ref, gate_ref, o_ref, *, nk):
    k = pl.program_id(1)

    @pl.when(k == 0)
    def _():
        o_ref[...] = jnp.zeros_like(o_ref)

    o_ref[...] += _dot(u_ref[...], w_ref[...])

    @pl.when(k == nk - 1)
    def _():
        o_ref[...] = y_ref[...] + gate_ref[0] * (_rms(o_ref[...]) * g_ref[...])


def _down_resid(u, w2b, y, gain, gate, *, n_prompt, t_sample, tm=1024, tk=1408):
    M, D = y.shape
    F = u.shape[1]
    tm = _row_tile(n_prompt, t_sample, tm)
    tk = _tile(F, tk, LANE)
    nk = F // tk
    grp = functools.partial(_group_of, tm=tm, n_prompt=n_prompt, t_sample=t_sample)
    return pl.pallas_call(
        functools.partial(_down_resid_kernel, nk=nk),
        out_shape=SDS((M, D), F32),
        grid=(M // tm, nk),
        in_specs=[
            pl.BlockSpec((tm, tk), lambda i, k: (i, k)),
            pl.BlockSpec((tk, D), lambda i, k: (k, 0)),
            pl.BlockSpec((tm, D), lambda i, k: (i, 0), pipeline_mode=pl.Buffered(1)),
            pl.BlockSpec((1, D), lambda i, k: (0, 0)),
            pl.BlockSpec((1, 1, D), lambda i, k: (grp(i), 0, 0)),
        ],
        out_specs=pl.BlockSpec((tm, D), lambda i, k: (i, 0)),
        compiler_params=_cparams("parallel", "arbitrary"),
        name="down_resid",
    )(u, w2b, y, gain.reshape(1, D), gate)


def _gather_sorted_kernel(tr_ref, src_ref, h_hbm, o_ref, xs, sem, *, tm, S, nt):
    i = pl.program_id(0)

    def start_tile(t, slot):
        def issue(r, c):
            s0 = pl.multiple_of(src_ref[t * tm + r] * S, S)
            pltpu.make_async_copy(h_hbm.at[pl.ds(s0, S)], xs.at[slot, pl.ds(pl.multiple_of(r * S, S), S)],
                                  sem.at[slot]).start()
            return c

        lax.fori_loop(0, tm, issue, 0)

    def finish_tile(slot):
        pltpu.make_async_copy(h_hbm.at[pl.ds(0, tm * S)], xs.at[slot], sem.at[slot]).wait()
        for s in range(S):
            o_ref[:, s * LANE:(s + 1) * LANE] = _from_slab(xs.at[slot], tm, s, S).astype(o_ref.dtype)

    @pl.when(jnp.logical_and(i == 0, tr_ref[0] > 0))
    def _():
        start_tile(0, 0)

    nxt = jnp.minimum(i + 1, nt - 1)
    for slot in range(2):
        @pl.when(jnp.logical_and(jnp.logical_and(i + 1 < nt, tr_ref[nxt] > 0), i % 2 == slot))
        def _():
            start_tile(i + 1, 1 - slot)

    @pl.when(tr_ref[i] == 0)
    def _():
        o_ref[...] = jnp.zeros_like(o_ref)

    for slot in range(2):
        @pl.when(jnp.logical_and(tr_ref[i] > 0, i % 2 == slot))
        def _():
            finish_tile(slot)


def _gather_sorted(h_slab, tile_rows, src_tok, *, tm, D):
    S = D // LANE
    nt = tile_rows.shape[0]
    grid_spec = pltpu.PrefetchScalarGridSpec(
        num_scalar_prefetch=2,
        grid=(nt,),
        in_specs=[pl.BlockSpec(memory_space=pl.ANY)],
        out_specs=pl.BlockSpec((tm, D), lambda i, tr, sr: (i, 0)),
        scratch_shapes=[pltpu.VMEM((2, tm * S, LANE), F32), pltpu.SemaphoreType.DMA((2,))],
    )
    return pl.pallas_call(
        functools.partial(_gather_sorted_kernel, tm=tm, S=S, nt=nt),
        out_shape=SDS((nt * tm, D), BF16),
        grid_spec=grid_spec,
        compiler_params=_cparams("arbitrary"),
        name="gather_sorted",
    )(tile_rows, src_tok, h_slab)


def _moe_gate_up_kernel(te_ref, tr_ref, tf_ref, x_ref, w1_ref, w3_ref, u_ref, wb1, wb3):
    i = pl.program_id(1)

    @pl.when(tf_ref[i] > 0)
    def _():
        wb1[...] = w1_ref[...].astype(BF16)
        wb3[...] = w3_ref[...].astype(BF16)

    @pl.when(tr_ref[i] > 0)
    def _():
        x = x_ref[...]
        u_ref[...] = (_silu(_dot(x, wb1[...])) * _dot(x, wb3[...])).astype(u_ref.dtype)

    @pl.when(tr_ref[i] == 0)
    def _():
        u_ref[...] = jnp.zeros_like(u_ref)


def _moe_gate_up(x_sorted, tile_e, tile_rows, tile_first, w1, w3, layer, *, tm, tn=512):
    R, D = x_sorted.shape
    F = w1.shape[-1]
    tn = _tile(F, tn, LANE)
    wspec = pl.BlockSpec((None, None, D, tn), lambda j, i, te, tr, tf: (layer, te[i], 0, j))
    grid_spec = pltpu.PrefetchScalarGridSpec(
        num_scalar_prefetch=3,
        grid=(F // tn, R // tm),
        in_specs=[pl.BlockSpec((tm, D), lambda j, i, te, tr, tf: (i, 0)), wspec, wspec],
        out_specs=pl.BlockSpec((tm, tn), lambda j, i, te, tr, tf: (i, j)),
        scratch_shapes=[pltpu.VMEM((D, tn), BF16), pltpu.VMEM((D, tn), BF16)],
    )
    return pl.pallas_call(
        _moe_gate_up_kernel,
        out_shape=SDS((R, F), BF16),
        grid_spec=grid_spec,
        compiler_params=_cparams("parallel", "arbitrary"),
        name="moe_gate_up",
    )(tile_e, tile_rows, tile_first, x_sorted, w1, w3)


def _moe_down_kernel(te_ref, tr_ref, dst_ref, u_ref, w_ref, gate_ref, out_hbm, acc, stage, sem,
                     *, nk, nt, tm, S, chunk):
    i = pl.program_id(0)
    k = pl.program_id(1)
    nrows = tr_ref[i]
    base = i * tm

    def row_copy(slot, r, d0):
        return pltpu.make_async_copy(stage.at[slot, pl.ds(pl.multiple_of(r * S, S), S)],
                                     out_hbm.at[pl.ds(d0, S)], sem.at[slot])

    def drain(slot, n):
        for c in range(tm // chunk):
            @pl.when(n >= (c + 1) * chunk)
            def _():
                pltpu.make_async_copy(stage.at[slot, pl.ds(c * chunk * S, chunk * S)],
                                      out_hbm.at[pl.ds(0, chunk * S)], sem.at[slot]).wait()

        def one(r, c):
            row_copy(slot, r, 0).wait()
            return c

        lax.fori_loop((n // chunk) * chunk, n, one, 0)

    @pl.when(nrows > 0)
    def _():
        @pl.when(k == 0)
        def _():
            acc[...] = jnp.zeros_like(acc)

        acc[...] += _dot(u_ref[...], w_ref[0])

        for slot in range(2):
            @pl.when(jnp.logical_and(k == nk - 1, i % 2 == slot))
            def _():
                _to_slab(stage.at[slot], acc[...] * gate_ref[...])

                def issue(r, c):
                    row_copy(slot, r, pl.multiple_of(dst_ref[base + r] * S, S)).start()
                    return c

                lax.fori_loop(0, nrows, issue, 0)

                @pl.when(i > 0)
                def _():
                    drain(1 - slot, tr_ref[jnp.maximum(i - 1, 0)])

                @pl.when(jnp.logical_or(i == nt - 1, tr_ref[jnp.minimum(i + 1, nt - 1)] == 0))
                def _():
                    drain(slot, nrows)


def _moe_down(u_sorted, tile_e, tile_rows, dst_row, gate_rows, w2b, *, n_tok, tm, tk=1408):
    R, F = u_sorted.shape
    D = w2b.shape[2]
    S = D // LANE
    tk = _tile(F, tk, LANE)
    nk = F // tk

    def kidx(i, k, tr):
        return jnp.where(tr[i] > 0, k, nk - 1)

    grid_spec = pltpu.PrefetchScalarGridSpec(
        num_scalar_prefetch=3,
        grid=(R // tm, nk),
        in_specs=[
            pl.BlockSpec((tm, tk), lambda i, k, te, tr, ds: (i, kidx(i, k, tr))),
            pl.BlockSpec((1, tk, D), lambda i, k, te, tr, ds: (te[i], kidx(i, k, tr), 0)),
            pl.BlockSpec((tm, 1), lambda i, k, te, tr, ds: (i, 0)),
        ],
        out_specs=pl.BlockSpec(memory_space=pl.ANY),
        scratch_shapes=[pltpu.VMEM((tm, D), F32), pltpu.VMEM((2, tm * S, LANE), F32),
                        pltpu.SemaphoreType.DMA((2,))],
    )
    return pl.pallas_call(
        functools.partial(_moe_down_kernel, nk=nk, nt=R // tm, tm=tm, S=S, chunk=_tile(tm, 64)),
        out_shape=SDS((2 * n_tok * S, LANE), F32),
        grid_spec=grid_spec,
        compiler_params=pltpu.CompilerParams(dimension_semantics=("arbitrary", "arbitrary"),
                                             vmem_limit_bytes=VMEM_LIMIT_BYTES, has_side_effects=True),
        name="moe_down",
    )(tile_e, tile_rows, dst_row, u_sorted, w2b, gate_rows)


def _combine_kernel(a_ref, b_ref, y_ref, g_ref, gate_ref, o_ref, *, tm, S):
    parts = [_from_slab(a_ref, tm, s, S) + _from_slab(b_ref, tm, s, S) for s in range(S)]
    ss = parts[0] * parts[0]
    for t in parts[1:]:
        ss = ss + t * t
    r = lax.rsqrt(jnp.sum(ss, axis=-1, keepdims=True) / (S * LANE) + EPS)
    for s, t in enumerate(parts):
        c = slice(s * LANE, (s + 1) * LANE)
        o_ref[:, c] = y_ref[:, c] + gate_ref[0][:, c] * (t * r * g_ref[:, c])


def _moe_combine(slots, y, gain, gate, *, n_prompt, t_sample, tm=256):
    M, D = y.shape
    S = D // LANE
    tm = _row_tile(n_prompt, t_sample, tm)
    nb = M // tm
    grp = functools.partial(_group_of, tm=tm, n_prompt=n_prompt, t_sample=t_sample)
    return pl.pallas_call(
        functools.partial(_combine_kernel, tm=tm, S=S),
        out_shape=SDS((M, D), F32),
        grid=(nb,),
        in_specs=[
            pl.BlockSpec((tm * S, LANE), lambda i: (i, 0)),
            pl.BlockSpec((tm * S, LANE), lambda i: (nb + i, 0)),
            pl.BlockSpec((tm, D), lambda i: (i, 0)),
            pl.BlockSpec((1, D), lambda i: (0, 0)),
            pl.BlockSpec((1, 1, D), lambda i: (grp(i), 0, 0)),
        ],
        out_specs=pl.BlockSpec((tm, D), lambda i: (i, 0)),
        compiler_params=_cparams("parallel"),
        name="moe_combine",
    )(slots, slots, y, gain.reshape(1, D), gate)


def _dispatch_tables(sel, gw, tm):
    M = sel.shape[0]
    E = N_EXPERTS
    i32 = jnp.int32
    e_flat = sel.reshape(-1)
    pair = jnp.arange(2 * M, dtype=i32)
    onehot = (e_flat[:, None] == jnp.arange(E, dtype=i32)[None, :]).astype(i32)
    running = jnp.cumsum(onehot, axis=0)
    rank = jnp.sum(running * onehot, axis=1) - 1
    counts = running[-1]
    padded = ((counts + tm - 1) // tm) * tm
    ends_p = jnp.cumsum(padded)
    starts_p = ends_p - padded
    pos = starts_p[e_flat] + rank
    R = 2 * M + E * tm
    payload = jnp.stack([pair // 2, (pair % 2) * M + pair // 2,
                         lax.bitcast_convert_type(gw.reshape(-1), i32)], axis=1)
    table = jnp.zeros((R, 3), i32).at[pos].set(payload)
    src_tok, dst_row = table[:, 0], table[:, 1]
    gate_rows = lax.bitcast_convert_type(table[:, 2], F32)
    tile_start = jnp.arange(R // tm, dtype=i32) * tm
    tile_e = jnp.minimum(jnp.sum(ends_p[None, :] <= tile_start[:, None], axis=1), E - 1).astype(i32)
    tile_rows = jnp.clip(starts_p[tile_e] + counts[tile_e] - tile_start, 0, tm).astype(i32)
    tile_first = jnp.concatenate([jnp.ones((1,), i32), (tile_e[1:] != tile_e[:-1]).astype(i32)])
    return tile_e, tile_rows, tile_first, src_tok, dst_row, gate_rows.reshape(R, 1)


def _attn_kernel(*refs, n_parts, k_modes, v_mode, hb, scale, nk):
    q_refs = refs[:n_parts]
    k_refs = refs[n_parts:2 * n_parts]
    v_ref = refs[2 * n_parts]
    o_ref = refs[2 * n_parts + 1]
    m_sc, l_sc, acc_sc = refs[2 * n_parts + 2:]
    j = pl.program_id(3)
    c_exp = scale * math.log2(math.e)

    @pl.when(j == 0)
    def _():
        m_sc[...] = jnp.full_like(m_sc, -jnp.inf)
        l_sc[...] = jnp.zeros_like(l_sc)
        acc_sc[...] = jnp.zeros_like(acc_sc)

    def block(ref, mode, hh):
        x = ref[:, hh * LANE:(hh + 1) * LANE] if mode == "head" else ref[...]
        return x.astype(BF16)

    for hh in range(hb):
        q = [block(q_refs[p], "head", hh) for p in range(n_parts)]
        k = [block(k_refs[p], k_modes[p], hh) for p in range(n_parts)]
        q = q[0] if n_parts == 1 else jnp.concatenate(q, axis=1)
        k = k[0] if n_parts == 1 else jnp.concatenate(k, axis=1)
        s = _dot_nt(q, k)
        tiles = [s[:, c * LANE:(c + 1) * LANE] for c in range(s.shape[1] // LANE)]
        m_lane = tiles[0]
        for t in tiles[1:]:
            m_lane = jnp.maximum(m_lane, t)
        m_prev = m_sc[hh]
        m_new = jnp.maximum(m_prev, jnp.max(m_lane, axis=-1, keepdims=True))
        alpha = jnp.exp2((m_prev - m_new) * c_exp)
        p = [jnp.exp2((t - m_new) * c_exp) for t in tiles]
        l_lane = p[0]
        for t in p[1:]:
            l_lane = l_lane + t
        l_sc[hh] = alpha * l_sc[hh] + l_lane
        pb = jnp.concatenate([t.astype(BF16) for t in p], axis=1)
        acc_sc[hh] = alpha * acc_sc[hh] + _dot(pb, block(v_ref, v_mode, hh))
        m_sc[hh] = m_new

    @pl.when(j == nk - 1)
    def _():
        for hh in range(hb):
            l = jnp.sum(l_sc[hh], axis=-1, keepdims=True)
            o_ref[:, hh * LANE:(hh + 1) * LANE] = (acc_sc[hh] / l).astype(o_ref.dtype)


def _attention(q_parts, k_parts, v_part, *, B, T, S, H, hb, q_row0, k_row0, tq, tk, scale):
    tq = _tile(T, tq)
    tk = _tile(S, tk)
    nq, nk = T // tq, S // tk
    qb0, kb0 = q_row0 // tq, k_row0 // tk

    def q_spec(off):
        return pl.BlockSpec((tq, hb * LANE), lambda b, g, i, j: (qb0 + b * nq + i, off // hb + g))

    def k_spec(off, mode):
        if mode == "head":
            return pl.BlockSpec((tk, hb * LANE), lambda b, g, i, j: (kb0 + b * nk + j, off // hb + g))
        return pl.BlockSpec((tk, LANE), lambda b, g, i, j: (kb0 + b * nk + j, off + (g if mode == "group" else 0)))

    in_specs = [q_spec(off) for (_, off) in q_parts]
    in_specs += [k_spec(off, mode) for (_, off, mode) in k_parts]
    in_specs += [k_spec(v_part[1], v_part[2])]
    args = [a for (a, _) in q_parts] + [a for (a, _, _) in k_parts] + [v_part[0]]
    return pl.pallas_call(
        functools.partial(_attn_kernel, n_parts=len(q_parts), k_modes=[m for (_, _, m) in k_parts],
                          v_mode=v_part[2], hb=hb, scale=scale, nk=nk),
        out_shape=SDS((B * T, H * LANE), BF16),
        grid=(B, H // hb, nq, nk),
        in_specs=in_specs,
        out_specs=pl.BlockSpec((tq, hb * LANE), lambda b, g, i, j: (b * nq + i, g)),
        scratch_shapes=[pltpu.VMEM((hb, tq, LANE), F32), pltpu.VMEM((hb, tq, LANE), F32),
                        pltpu.VMEM((hb, tq, LANE), F32)],
        compiler_params=_cparams("parallel", "parallel", "parallel", "arbitrary"),
        name="attention",
    )(*args)


def _rmsnorm_cols_kernel(x_ref, g_ref, o_ref):
    o_ref[...] = (_rms(x_ref[...]) * g_ref[...]).astype(o_ref.dtype)


def _rmsnorm_cols(x, gain, *, col_block, width, out_dtype, tm=1024):
    M = x.shape[0]
    tm = _tile(M, tm)
    return pl.pallas_call(
        _rmsnorm_cols_kernel,
        out_shape=SDS((M, width), out_dtype),
        grid=(M // tm,),
        in_specs=[pl.BlockSpec((tm, width), lambda i: (i, col_block)),
                  pl.BlockSpec((1, width), lambda i: (0, 0))],
        out_specs=pl.BlockSpec((tm, width), lambda i: (i, 0)),
        compiler_params=_cparams("parallel"),
        name="rmsnorm_cols",
    )(x, gain.reshape(1, width))


def _head_rope_kernel(x_ref, g_ref, cos_ref, sin_ref, o_ref, *, norm, half):
    x = x_ref[...]
    if norm:
        x = _rms(x) * g_ref[0]
    lane = lax.broadcasted_iota(jnp.int32, x.shape, 1)
    first = (lane % (2 * half)) < half
    rot = jnp.where(first, -pltpu.roll(x, LANE - half, 1), pltpu.roll(x, half, 1))
    o_ref[...] = (x * cos_ref[...] + rot * sin_ref[...]).astype(o_ref.dtype)


def _head_rope(x, gains, cos, sin, *, col_block0, n_heads, norm, half, out_dtype, n_rows=None, tm=1024):
    M = x.shape[0] if n_rows is None else n_rows
    tm = _tile(M, tm)
    return pl.pallas_call(
        functools.partial(_head_rope_kernel, norm=norm, half=half),
        out_shape=SDS((M, n_heads * LANE), out_dtype),
        grid=(M // tm, n_heads),
        in_specs=[
            pl.BlockSpec((tm, LANE), lambda i, h: (i, col_block0 + h)),
            pl.BlockSpec((1, 1, LANE), lambda i, h: (h, 0, 0)),
            pl.BlockSpec((tm, LANE), lambda i, h: (i, 0)),
            pl.BlockSpec((tm, LANE), lambda i, h: (i, 0)),
        ],
        out_specs=pl.BlockSpec((tm, LANE), lambda i, h: (i, h)),
        compiler_params=_cparams("parallel", "parallel"),
        name="head_rope",
    )(x, gains, cos, sin)


def _rope_tables(n_prompt, dec_batch, t_sample, half):
    tok = jnp.arange(t_sample, dtype=jnp.int32)
    row, col = tok // GRID_W, tok % GRID_W
    inv = ROPE_THETA ** (-jnp.arange(half, dtype=F32) / half)

    def cs(pos):
        ang = pos.astype(F32)[:, None] * inv[None, :]
        return (jnp.concatenate([jnp.cos(ang), jnp.cos(ang)], -1),
                jnp.concatenate([jnp.sin(ang), jnp.sin(ang)], -1))

    cr, sr = cs(row)
    cc, sc = cs(col)
    pad = LANE - 4 * half
    cos = jnp.concatenate([cr, cc, jnp.ones((t_sample, pad), F32)], -1)
    sin = jnp.concatenate([sr, sc, jnp.zeros((t_sample, pad), F32)], -1)
    cos = jnp.concatenate([jnp.ones((n_prompt, LANE), F32), jnp.tile(cos, (dec_batch, 1))], 0)
    sin = jnp.concatenate([jnp.zeros((n_prompt, LANE), F32), jnp.tile(sin, (dec_batch, 1))], 0)
    return cos, sin


def _hgrn_constants(C, reverse):
    L = int(np.log2(C))
    t = np.arange(C)[:, None]
    u = np.arange(C)[None, :]
    Ws, Ms = [], []
    for lvl in range(L):
        b = 1 << lvl
        grp = t // (2 * b)
        if not reverse:
            bnd = grp * 2 * b + b - 1
            qside = (t % (2 * b)) >= b
            W = np.where(qside, (u > bnd) & (u <= t), (u > t) & (u <= bnd))
            kside_s = (u % (2 * b)) < b
        else:
            bnd = grp * 2 * b + b
            qside = (t % (2 * b)) < b
            W = np.where(qside, (u >= t) & (u < bnd), (u >= bnd) & (u < t))
            kside_s = (u % (2 * b)) >= b
        Ws.append(W)
        Ms.append(qside & kside_s & (grp == u // (2 * b)))
    if not reverse:
        Ws += [u <= t, u > t]
    else:
        Ws += [u >= t, u < t]
    Ms.append(t == u)
    W_all = jnp.asarray(np.concatenate(Ws, 0).astype(np.float32), BF16)
    M_all = jnp.asarray(np.stack(Ms).astype(np.float32), F32)
    return W_all, M_all


def _hgrn_chain(q, v, z, lb, W_all, m_ref, st, *, C, L, edge_row):
    f = lb + (1.0 - lb) * jax.nn.sigmoid(z)
    k = 1.0 - f
    lf = jnp.log(f)
    hi = lf.astype(BF16)
    r1 = lf - hi.astype(F32)
    mid = r1.astype(BF16)
    lo = (r1 - mid.astype(F32)).astype(BF16)
    d3 = _dot(W_all, jnp.concatenate([hi, mid, lo], axis=1))
    dn = d3[:, :LANE] + d3[:, LANE:2 * LANE] + d3[:, 2 * LANE:]
    e_all = jnp.exp(dn)
    qb = q.astype(BF16)
    a = m_ref[L] * _dot_nt(qb, k.astype(BF16))
    for lvl in range(L):
        e = e_all[lvl * C:(lvl + 1) * C]
        a = a + m_ref[lvl] * _dot_nt((q * e).astype(BF16), (k * e).astype(BF16))
    eq = e_all[L * C:(L + 1) * C]
    ek = e_all[(L + 1) * C:(L + 2) * C]
    vb = v.astype(BF16)
    o = _dot(a.astype(BF16), vb) + _dot_nt((q * eq).astype(BF16), st.astype(BF16))
    g_edge = dn[L * C + edge_row:L * C + edge_row + 1]
    st_new = st * jnp.exp(g_edge) + _dot_tn(vb, (k * ek).astype(BF16))
    return o, st_new


def _hgrn_kernel(*refs, C, L, nc, hb, has_init):
    (qf_ref, vf_ref, zf_ref, qb_ref, vb_ref, zb_ref, lbf_ref, lbb_ref,
     wf_ref, mf_ref, wb_ref, mb_ref) = refs[:12]
    rest = refs[12:]
    if has_init:
        s0_ref, rest = rest[0], rest[1:]
    of_ref, ob_ref, sfin_ref, stf, stb = rest
    c = pl.program_id(2)

    @pl.when(c == 0)
    def _():
        for hh in range(hb):
            if has_init:
                stf[hh] = s0_ref[0, 0, hh]
                stb[hh] = s0_ref[0, 1, hh]
            else:
                stf[hh] = jnp.zeros((LANE, LANE), F32)
                stb[hh] = jnp.zeros((LANE, LANE), F32)

    for hh in range(hb):
        cols = slice(hh * LANE, (hh + 1) * LANE)
        o, s_new = _hgrn_chain(qf_ref[:, cols], vf_ref[:, cols], zf_ref[:, cols], lbf_ref[0][:, cols],
                               wf_ref[...], mf_ref, stf[hh], C=C, L=L, edge_row=C - 1)
        of_ref[:, cols] = o
        stf[hh] = s_new
        o, s_new = _hgrn_chain(qb_ref[:, cols], vb_ref[:, cols], zb_ref[:, cols], lbb_ref[0][:, cols],
                               wb_ref[...], mb_ref, stb[hh], C=C, L=L, edge_row=0)
        ob_ref[:, cols] = o
        stb[hh] = s_new

    @pl.when(c == nc - 1)
    def _():
        for hh in range(hb):
            sfin_ref[0, 0, hh] = stf[hh].T
            sfin_ref[0, 1, hh] = stb[hh].T


def _hgrn_scan(proj, lb, s0t, *, B, T, row0, C=128, hb=4):
    D = proj.shape[1] // 5
    H = D // LANE
    C = _tile(T, C)
    L = int(np.log2(C))
    assert 1 << L == C
    nc = T // C
    rb0 = row0 // C
    HB = H // hb
    W = hb * LANE
    wf, mf = _hgrn_constants(C, False)
    wb, mb = _hgrn_constants(C, True)

    def fwd(col0):
        return pl.BlockSpec((C, W), lambda b, g, c: (rb0 + b * nc + c, col0 // hb + g))

    def bwd(col0):
        return pl.BlockSpec((C, W), lambda b, g, c: (rb0 + b * nc + nc - 1 - c, col0 // hb + g))

    def const(a):
        return pl.BlockSpec(a.shape, lambda b, g, c: (0,) * a.ndim)

    in_specs = [fwd(0), fwd(H), fwd(3 * H), bwd(0), bwd(H), bwd(4 * H),
                pl.BlockSpec((1, 1, W), lambda b, g, c: (0, 0, g)),
                pl.BlockSpec((1, 1, W), lambda b, g, c: (1, 0, g)),
                const(wf), const(mf), const(wb), const(mb)]
    args = [proj, proj, proj, proj, proj, proj, lb, lb, wf, mf, wb, mb]
    if s0t is not None:
        in_specs.append(pl.BlockSpec((1, 2, hb, LANE, LANE), lambda b, g, c: (b, 0, g, 0, 0)))
        args.append(s0t)
    return pl.pallas_call(
        functools.partial(_hgrn_kernel, C=C, L=L, nc=nc, hb=hb, has_init=s0t is not None),
        out_shape=(SDS((B * T, D), F32), SDS((B * T, D), F32), SDS((B, 2, H, LANE, LANE), F32)),
        grid=(B, HB, nc),
        in_specs=in_specs,
        out_specs=(
            pl.BlockSpec((C, W), lambda b, g, c: (b * nc + c, g)),
            pl.BlockSpec((C, W), lambda b, g, c: (b * nc + nc - 1 - c, g)),
            pl.BlockSpec((1, 2, hb, LANE, LANE), lambda b, g, c: (b, 0, g, 0, 0)),
        ),
        scratch_shapes=[pltpu.VMEM((hb, LANE, LANE), F32), pltpu.VMEM((hb, LANE, LANE), F32)],
        compiler_params=_cparams("parallel", "parallel", "arbitrary"),
        name="hgrn_scan",
    )(*args)


def _hgrn_gate_kernel(of_ref, ob_ref, g_ref, w_ref, o_ref):
    o = _rms(of_ref[...] + ob_ref[...]) * w_ref[0]
    o_ref[...] = (o * _silu(g_ref[...])).astype(o_ref.dtype)


def _hgrn_gate(o_f, o_b, proj, o_norm, *, row0, tm=1024):
    M, D = o_f.shape
    H = D // LANE
    tm = _tile(math.gcd(M, row0) if row0 else M, tm)
    rb0 = row0 // tm
    blk = pl.BlockSpec((tm, LANE), lambda i, h: (i, h))
    return pl.pallas_call(
        _hgrn_gate_kernel,
        out_shape=SDS((M, D), BF16),
        grid=(M // tm, H),
        in_specs=[blk, blk,
                  pl.BlockSpec((tm, LANE), lambda i, h: (rb0 + i, 2 * H + h)),
                  pl.BlockSpec((1, 1, LANE), lambda i, h: (h, 0, 0))],
        out_specs=blk,
        compiler_params=_cparams("parallel", "parallel"),
        name="hgrn_gate",
    )(o_f, o_b, proj, o_norm.reshape(H, 1, LANE))


def kernel(x_prompt, x_sample, c, cache_mla_ckv, cache_mla_kpe, state_hgrn, cache_gqa_k, cache_gqa_v, c_ctx, ada_w, ada_b, norm_w, mla_wq_a, mla_q_norm, mla_wq_b, mla_wkv_a, mla_kv_norm, mla_wkv_b, mla_wo, hgrn_w_in, hgrn_lb_logits, hgrn_o_norm, hgrn_wo, gqa_w_qkv, gqa_q_norm, gqa_k_norm, gqa_wo, ffn_w1, ffn_w3, ffn_w2, moe_router, moe_w1, moe_w3, moe_w2):
    Bp, Tp, D = x_prompt.shape
    Bs, Ts, _ = x_sample.shape
    P = cache_mla_ckv.shape[2]
    depth = ada_w.shape[0]
    NP, NS = Bp * Tp, Bs * Ts
    M = NP + NS
    G = 1 + Bs
    assert D == HG_HEADS * LANE
    rows = dict(n_prompt=NP, t_sample=Ts)

    y = jnp.concatenate([x_prompt.reshape(NP, D), x_sample.reshape(NS, D)], axis=0)
    cond8 = jnp.zeros((8, D), F32).at[0].set(c_ctx).at[1:G].set(c)
    mod = _modulation(cond8, ada_w, ada_b).reshape(depth, 8, 6, D)
    mod = jnp.transpose(mod, (0, 2, 1, 3))[:, :, :G, None, :]

    tm_tok = _row_tile(NP, Ts, 512)
    cos_mla, sin_mla = _rope_tables(NP, Bs, Ts, MLA_ROPE // 4)
    cos_gqa, sin_gqa = _rope_tables(NP, Bs, Ts, GQA_HD // 4)
    ones_g = jnp.ones((MLA_HEADS, 1, LANE), F32)

    ckv_list, kpe_list, hg_list, gk_list, gv_list = [], [], [], [], []
    for layer in range(depth):
        kind, j = layer % 3, layer // 3
        nw = norm_w[layer]
        sh1, sc1, g1, sh2, sc2, g2 = (mod[layer, k] for k in range(6))
        h = _adaln_in(y, nw[0], sh1, sc1, tm=tm_tok, **rows)

        if kind == 0:
            Hm = MLA_HEADS
            w_a = jnp.concatenate([mla_wq_a[j], mla_wkv_a[j],
                                   jnp.zeros((D, LANE - MLA_ROPE), F32)], axis=1)
            a = _mm(h, w_a, out_dtype=F32)
            ql, kvl = mla_wq_a.shape[2], mla_kv_norm.shape[1]
            qn = _rmsnorm_cols(a, mla_q_norm[j], col_block=0, width=ql, out_dtype=BF16)
            ckv = _rmsnorm_cols(a, mla_kv_norm[j], col_block=ql // kvl, width=kvl, out_dtype=F32)
            wqb = mla_wq_b[j].reshape(ql, Hm, MLA_NOPE + MLA_ROPE)
            wqb = jnp.concatenate([
                wqb[:, :, :MLA_NOPE].reshape(ql, Hm * MLA_NOPE),
                jnp.pad(wqb[:, :, MLA_NOPE:], ((0, 0), (0, 0), (0, LANE - MLA_ROPE))).reshape(ql, Hm * LANE),
            ], axis=1)
            q = _mm(qn, wqb, out_dtype=F32)
            pe_blk = (ql + kvl) // LANE
            q_pe = _head_rope(q, ones_g, cos_mla, sin_mla, col_block0=Hm, n_heads=Hm, norm=False,
                              half=MLA_ROPE // 4, out_dtype=BF16)
            k_pe = _head_rope(a, ones_g, cos_mla, sin_mla, col_block0=pe_blk, n_heads=1, norm=False,
                              half=MLA_ROPE // 4, out_dtype=BF16)
            wkvb = mla_wkv_b[j].reshape(kvl, Hm, MLA_NOPE + MLA_V)
            wkvb = jnp.concatenate([wkvb[:, :, :MLA_NOPE].reshape(kvl, Hm * MLA_NOPE),
                                    wkvb[:, :, MLA_NOPE:].reshape(kvl, Hm * MLA_V)], axis=1)
            ckv_s = jnp.concatenate([ckv[NP:].reshape(Bs, Ts, kvl), cache_mla_ckv[:, j]], axis=1)
            c_all = jnp.concatenate([ckv[:NP], ckv_s.reshape(Bs * (Ts + P), kvl)], axis=0)
            kv = _mm(c_all, wkvb, out_dtype=BF16)
            kpe_ctx = jnp.pad(cache_mla_kpe[:, j], ((0, 0), (0, 0), (0, LANE - MLA_ROPE))).astype(BF16)
            kpe_s = jnp.concatenate([k_pe[NP:].reshape(Bs, Ts, LANE), kpe_ctx], axis=1)
            kpe_all = jnp.concatenate([k_pe[:NP], kpe_s.reshape(Bs * (Ts + P), LANE)], axis=0)
            scale = (MLA_NOPE + MLA_ROPE) ** -0.5
            qp = [(q, 0), (q_pe, 0)]
            kp = [(kv, 0, "head"), (kpe_all, 0, "shared")]
            vp = (kv, Hm, "head")
            mix_p = _attention(qp, kp, vp, B=Bp, T=Tp, S=Tp, H=Hm, hb=4, q_row0=0, k_row0=0,
                               tq=256, tk=256, scale=scale)
            mix_s = _attention(qp, kp, vp, B=Bs, T=Ts, S=Ts + P, H=Hm, hb=4, q_row0=NP, k_row0=NP,
                               tq=512, tk=512, scale=scale)
            w_o = mla_wo
            ckv_list.append(ckv[:NP].reshape(Bp, Tp, kvl))
            kpe_list.append(a[:NP, ql + kvl:ql + kvl + MLA_ROPE].reshape(Bp, Tp, MLA_ROPE))
        elif kind == 1:
            H = HG_HEADS
            p = jax.nn.softmax(hgrn_lb_logits.astype(F32), axis=1)
            cum = jnp.cumsum(p, axis=1)
            lb = (cum - cum[:, :1])[:, layer].reshape(2, 1, D)
            proj = _mm(h, hgrn_w_in, layer=j, out_dtype=F32)
            of_p, ob_p, st_p = _hgrn_scan(proj, lb, None, B=Bp, T=Tp, row0=0)
            s0t = jnp.swapaxes(state_hgrn[:, j].astype(F32), -1, -2)
            of_s, ob_s, _ = _hgrn_scan(proj, lb, s0t, B=Bs, T=Ts, row0=NP)
            mix_p = _hgrn_gate(of_p, ob_p, proj, hgrn_o_norm[j], row0=0)
            mix_s = _hgrn_gate(of_s, ob_s, proj, hgrn_o_norm[j], row0=NP)
            w_o = hgrn_wo
            hg_list.append(st_p)
        else:
            Hq, Hk = GQA_HEADS, GQA_KV_HEADS
            qkv = _mm(h, gqa_w_qkv, layer=j, out_dtype=F32)
            gains = jnp.concatenate([jnp.tile(gqa_q_norm[j][None], (Hq, 1)),
                                     jnp.tile(gqa_k_norm[j][None], (Hk, 1))], axis=0).reshape(Hq + Hk, 1, LANE)
            k_plain = _head_rope(qkv, gains[Hq:], cos_gqa, sin_gqa, col_block0=Hq, n_heads=Hk, norm=True,
                                 half=GQA_HD // 4, out_dtype=F32, n_rows=NP)
            qk = _head_rope(qkv, gains, cos_gqa, sin_gqa, col_block0=0, n_heads=Hq + Hk, norm=True,
                            half=GQA_HD // 4, out_dtype=BF16)
            kw = Hk * LANE
            k_new = qk[:, Hq * LANE:]
            v_new = qkv[:, (Hq + Hk) * LANE:].astype(BF16)
            k_s = jnp.concatenate([k_new[NP:].reshape(Bs, Ts, kw),
                                   cache_gqa_k[:, j].reshape(Bs, P, kw).astype(BF16)], axis=1)
            v_s = jnp.concatenate([v_new[NP:].reshape(Bs, Ts, kw),
                                   cache_gqa_v[:, j].reshape(Bs, P, kw).astype(BF16)], axis=1)
            k_all = jnp.concatenate([k_new[:NP], k_s.reshape(Bs * (Ts + P), kw)], axis=0)
            v_all = jnp.concatenate([v_new[:NP], v_s.reshape(Bs * (Ts + P), kw)], axis=0)
            scale = GQA_HD ** -0.5
            qp = [(qk, 0)]
            kp = [(k_all, 0, "group")]
            vp = (v_all, 0, "group")
            mix_p = _attention(qp, kp, vp, B=Bp, T=Tp, S=Tp, H=Hq, hb=Hq // Hk, q_row0=0, k_row0=0,
                               tq=256, tk=256, scale=scale)
            mix_s = _attention(qp, kp, vp, B=Bs, T=Ts, S=Ts + P, H=Hq, hb=Hq // Hk, q_row0=NP, k_row0=NP,
                               tq=512, tk=512, scale=scale)
            w_o = gqa_wo
            gk_list.append(k_plain[:NP].reshape(Bp, Tp, Hk, GQA_HD))
            gv_list.append(qkv[:NP, (Hq + Hk) * LANE:].reshape(Bp, Tp, Hk, GQA_HD))

        y = _mm_resid(mix_p, mix_s, w_o, y, nw[1], g1, layer=j, **rows)

        fi = layer // 2
        if layer % 2 == 0:
            h2 = _adaln_in(y, nw[2], sh2, sc2, tm=tm_tok, **rows)
            u = _gate_up(h2, ffn_w1, ffn_w3, fi)
            y = _down_resid(u, _cast_bf16(ffn_w2, fi), y, nw[3], g2, **rows)
        else:
            h2, sel, gw = _adaln_in(y, nw[2], sh2, sc2, tm=tm_tok, router=moe_router[fi], **rows)
            tm_e = _tile(M, 1024)
            tile_e, tile_rows, tile_first, src_tok, dst_row, gate_rows = _dispatch_tables(
                sel[:, :2], gw[:, :2], tm_e)
            x_sorted = _gather_sorted(h2, tile_rows, src_tok, tm=tm_e, D=D)
            u = _moe_gate_up(x_sorted, tile_e, tile_rows, tile_first, moe_w1, moe_w3, fi, tm=tm_e)
            slots = _moe_down(u, tile_e, tile_rows, dst_row, gate_rows, _cast_bf16(moe_w2, fi),
                              n_tok=M, tm=tm_e)
            y = _moe_combine(slots, y, nw[3], g2, **rows)

    return (y[:NP].reshape(Bp, Tp, D), y[NP:].reshape(Bs, Ts, D),
            jnp.stack(ckv_list, axis=1), jnp.stack(kpe_list, axis=1), jnp.stack(hg_list, axis=1),
            jnp.stack(gk_list, axis=1), jnp.stack(gv_list, axis=1))
```

```python
import functools
import math

import numpy as np
import jax
import jax.numpy as jnp
from jax import lax
from jax.experimental import pallas as pl
from jax.experimental.pallas import tpu as pltpu

F32 = jnp.float32
BF16 = jnp.bfloat16
EPS = 1e-6
LANE = 128

GRID_W = 64
ROPE_THETA = 10000.0
MLA_HEADS, MLA_NOPE, MLA_ROPE, MLA_V = 16, 128, 64, 128
HG_HEADS = 16
GQA_HEADS, GQA_KV_HEADS, GQA_HD = 16, 4, 128
N_EXPERTS = 8

VMEM_LIMIT_BYTES = 56 * 1024 * 1024

SDS = jax.ShapeDtypeStruct


def _cparams(*sem):
    return pltpu.CompilerParams(dimension_semantics=sem, vmem_limit_bytes=VMEM_LIMIT_BYTES)


def _tile(n, pref, step=8):
    t = min(pref, n)
    while n % t:
        t -= step
    return t


def _dot(a, b):
    return jnp.dot(a, b, preferred_element_type=F32)


def _dot_nt(a, b):
    return lax.dot_general(a, b, (((1,), (1,)), ((), ())), preferred_element_type=F32)


def _dot_tn(a, b):
    return lax.dot_general(a, b, (((0,), (0,)), ((), ())), preferred_element_type=F32)


def _rms(x):
    return x * lax.rsqrt(jnp.mean(x * x, axis=-1, keepdims=True) + EPS)


def _silu(x):
    return x * jax.nn.sigmoid(x)


def _row_tile(n_prompt, t_sample, pref):
    return _tile(math.gcd(n_prompt, t_sample), pref)


def _group_of(i, tm, n_prompt, t_sample):
    r = i * tm
    return jnp.where(r < n_prompt, 0, 1 + (r - n_prompt) // t_sample)


def _mod_kernel(c_ref, w_ref, b_ref, o_ref):
    s = _silu(c_ref[...]).astype(BF16)
    o_ref[0] = _dot(s, w_ref[0].astype(BF16)) + b_ref[0]


def _modulation(cond8, ada_w, ada_b):
    L, D, N6 = ada_w.shape
    tn = _tile(N6, 1024)
    return pl.pallas_call(
        _mod_kernel,
        out_shape=SDS((L, 8, N6), F32),
        grid=(L, N6 // tn),
        in_specs=[
            pl.BlockSpec((8, D), lambda l, j: (0, 0)),
            pl.BlockSpec((1, D, tn), lambda l, j: (l, 0, j)),
            pl.BlockSpec((1, 1, tn), lambda l, j: (l, 0, j)),
        ],
        out_specs=pl.BlockSpec((1, 8, tn), lambda l, j: (l, 0, j)),
        compiler_params=_cparams("parallel", "parallel"),
        name="modulation",
    )(cond8, ada_w, ada_b.reshape(L, 1, N6))


def _adaln_in_kernel(y_ref, g_ref, sh_ref, sc_ref, h_ref):
    h = (_rms(y_ref[...]) * g_ref[...]) * (1.0 + sc_ref[0]) + sh_ref[0]
    h_ref[...] = h.astype(h_ref.dtype)


def _to_slab(ref, x):
    rows, D = x.shape
    S = D // LANE
    for s in range(S):
        ref[pl.ds(s, rows, stride=S), :] = x[:, s * LANE:(s + 1) * LANE]


def _from_slab(ref, rows, s, S):
    return ref[pl.ds(s, rows, stride=S), :]


def _adaln_route_kernel(y_ref, g_ref, sh_ref, sc_ref, r_ref, h_ref, sel_ref, gw_ref):
    h = (_rms(y_ref[...]) * g_ref[...]) * (1.0 + sc_ref[0]) + sh_ref[0]
    _to_slab(h_ref, h)
    logits = lax.dot_general(h, r_ref[...], (((1,), (0,)), ((), ())),
                             precision=lax.Precision.HIGHEST, preferred_element_type=F32)
    lane = lax.broadcasted_iota(jnp.int32, logits.shape, 1)
    neg = jnp.float32(-jnp.inf)
    logits = jnp.where(lane < N_EXPERTS, logits, neg)
    lane_f = lane.astype(F32)
    m1 = jnp.max(logits, axis=-1, keepdims=True)
    i1 = jnp.min(jnp.where(logits == m1, lane_f, float(LANE)), axis=-1, keepdims=True)
    rest = jnp.where(lane_f == i1, neg, logits)
    m2 = jnp.max(rest, axis=-1, keepdims=True)
    i2 = jnp.min(jnp.where(rest == m2, lane_f, float(LANE)), axis=-1, keepdims=True)
    e = jnp.exp(m2 - m1)
    w1 = 1.0 / (1.0 + e)
    w2 = e / (1.0 + e)
    i1, i2 = i1.astype(jnp.int32), i2.astype(jnp.int32)
    sel_ref[...] = jnp.where(lane == 0, i1, jnp.where(lane == 1, i2, 0))
    gw_ref[...] = jnp.where(lane == 0, w1, jnp.where(lane == 1, w2, 0.0))


def _adaln_in(y, gain, shift, scale, *, n_prompt, t_sample, tm, router=None):
    M, D = y.shape
    grp = functools.partial(_group_of, tm=tm, n_prompt=n_prompt, t_sample=t_sample)
    in_specs = [
        pl.BlockSpec((tm, D), lambda i: (i, 0)),
        pl.BlockSpec((1, D), lambda i: (0, 0)),
        pl.BlockSpec((1, 1, D), lambda i: (grp(i), 0, 0)),
        pl.BlockSpec((1, 1, D), lambda i: (grp(i), 0, 0)),
    ]
    args = [y, gain.reshape(1, D), shift, scale]
    if router is None:
        return pl.pallas_call(
            _adaln_in_kernel,
            out_shape=SDS((M, D), BF16),
            grid=(M // tm,),
            in_specs=in_specs,
            out_specs=pl.BlockSpec((tm, D), lambda i: (i, 0)),
            compiler_params=_cparams("parallel"),
            name="adaln_in",
        )(*args)
    r_pad = jnp.pad(router, ((0, 0), (0, LANE - router.shape[1])))
    return pl.pallas_call(
        _adaln_route_kernel,
        out_shape=(SDS((M * (D // LANE), LANE), F32), SDS((M, LANE), jnp.int32), SDS((M, LANE), F32)),
        grid=(M // tm,),
        in_specs=in_specs + [pl.BlockSpec((D, LANE), lambda i: (0, 0))],
        out_specs=(
            pl.BlockSpec((tm * (D // LANE), LANE), lambda i: (i, 0)),
            pl.BlockSpec((tm, LANE), lambda i: (i, 0)),
            pl.BlockSpec((tm, LANE), lambda i: (i, 0)),
        ),
        compiler_params=_cparams("parallel"),
        name="adaln_route",
    )(*args, r_pad)


def _mm_kernel(x_ref, w_ref, o_ref, wbf):
    @pl.when(pl.program_id(1) == 0)
    def _():
        wbf[...] = w_ref[...].astype(BF16)

    o_ref[...] = _dot(x_ref[...].astype(BF16), wbf[...]).astype(o_ref.dtype)


def _layer_spec(w, layer, block, index_map):
    if layer is None:
        return pl.BlockSpec(block, index_map)
    lead = w.ndim - len(block) - 1
    return pl.BlockSpec((None,) * (lead + 1) + tuple(block),
                        lambda *a: (layer,) + (0,) * lead + tuple(index_map(*a)))


def _mm(x, w, *, out_dtype, layer=None, tm=1024, tn=1024):
    M, K = x.shape
    N = w.shape[-1]
    tm = _tile(M, tm)
    tn = _tile(N, tn, LANE)
    return pl.pallas_call(
        _mm_kernel,
        out_shape=SDS((M, N), out_dtype),
        grid=(N // tn, M // tm),
        in_specs=[
            pl.BlockSpec((tm, K), lambda j, i: (i, 0)),
            _layer_spec(w, layer, (K, tn), lambda j, i: (0, j)),
        ],
        out_specs=pl.BlockSpec((tm, tn), lambda j, i: (i, j)),
        scratch_shapes=[pltpu.VMEM((K, tn), BF16)],
        compiler_params=_cparams("parallel", "arbitrary"),
        name="mm",
    )(x, w)


def _mm_resid_kernel(xp_ref, xs_ref, w_ref, y_ref, g_ref, gate_ref, o_ref, wbf, *, n_ptiles):
    i = pl.program_id(0)

    @pl.when(i == 0)
    def _():
        wbf[...] = w_ref[...].astype(BF16)

    x = jnp.where(i < n_ptiles, xp_ref[...], xs_ref[...])
    a = _dot(x.astype(BF16), wbf[...])
    o_ref[...] = y_ref[...] + gate_ref[0] * (_rms(a) * g_ref[...])


def _mm_resid(x_p, x_s, w, y, gain, gate, *, layer, n_prompt, t_sample, tm=256):
    M, D = y.shape
    K = w.shape[-2]
    tm = _row_tile(n_prompt, t_sample, tm)
    n_ptiles = n_prompt // tm
    grp = functools.partial(_group_of, tm=tm, n_prompt=n_prompt, t_sample=t_sample)
    return pl.pallas_call(
        functools.partial(_mm_resid_kernel, n_ptiles=n_ptiles),
        out_shape=SDS((M, D), F32),
        grid=(M // tm,),
        in_specs=[
            pl.BlockSpec((tm, K), lambda i: (jnp.minimum(i, n_ptiles - 1), 0)),
            pl.BlockSpec((tm, K), lambda i: (jnp.maximum(i - n_ptiles, 0), 0)),
            pl.BlockSpec((None, K, D), lambda i: (layer, 0, 0), pipeline_mode=pl.Buffered(1)),
            pl.BlockSpec((tm, D), lambda i: (i, 0)),
            pl.BlockSpec((1, D), lambda i: (0, 0)),
            pl.BlockSpec((1, 1, D), lambda i: (grp(i), 0, 0)),
        ],
        out_specs=pl.BlockSpec((tm, D), lambda i: (i, 0)),
        scratch_shapes=[pltpu.VMEM((K, D), BF16)],
        compiler_params=_cparams("arbitrary"),
        name="mm_resid",
    )(x_p, x_s, w, y, gain.reshape(1, D), gate)


def _cast_kernel(x_ref, o_ref):
    o_ref[...] = x_ref[...].astype(o_ref.dtype)


def _cast_bf16(w, layer, *, tr=1024):
    shape = w.shape[1:]
    R = math.prod(shape[:-1])
    N = shape[-1]
    w2 = w.reshape(w.shape[0] * R, N)
    tr = _tile(R, tr)
    nb = R // tr
    out = pl.pallas_call(
        _cast_kernel,
        out_shape=SDS((R, N), BF16),
        grid=(nb,),
        in_specs=[pl.BlockSpec((tr, N), lambda i: (layer * nb + i, 0))],
        out_specs=pl.BlockSpec((tr, N), lambda i: (i, 0)),
        compiler_params=_cparams("parallel"),
        name="cast_bf16",
    )(w2)
    return out.reshape(shape)


def _gate_up_kernel(x_ref, w1_ref, w3_ref, u_ref, wb1, wb3):
    @pl.when(pl.program_id(1) == 0)
    def _():
        wb1[...] = w1_ref[...].astype(BF16)
        wb3[...] = w3_ref[...].astype(BF16)

    x = x_ref[...]
    u_ref[...] = (_silu(_dot(x, wb1[...])) * _dot(x, wb3[...])).astype(u_ref.dtype)


def _gate_up(x, w1, w3, layer, *, tm=1024, tn=512):
    M, D = x.shape
    F = w1.shape[-1]
    tm = _tile(M, tm)
    tn = _tile(F, tn, LANE)
    wspec = _layer_spec(w1, layer, (D, tn), lambda j, i: (0, j))
    return pl.pallas_call(
        _gate_up_kernel,
        out_shape=SDS((M, F), BF16),
        grid=(F // tn, M // tm),
        in_specs=[pl.BlockSpec((tm, D), lambda j, i: (i, 0)), wspec, wspec],
        out_specs=pl.BlockSpec((tm, tn), lambda j, i: (i, j)),
        scratch_shapes=[pltpu.VMEM((D, tn), BF16), pltpu.VMEM((D, tn), BF16)],
        compiler_params=_cparams("parallel", "arbitrary"),
        name="gate_up",
    )(x, w1, w3)


def _down_resid_kernel(u_ref, w_ref, y_ref, g_ref, gate_ref, o_ref, *, nk):
    k = pl.program_id(1)

    @pl.when(k == 0)
    def _():
        o_ref[...] = jnp.zeros_like(o_ref)

    o_ref[...] += _dot(u_ref[...], w_ref[...])

    @pl.when(k == nk - 1)
    def _():
        o_ref[...] = y_ref[...] + gate_ref[0] * (_rms(o_ref[...]) * g_ref[...])


def _down_resid(u, w2b, y, gain, gate, *, n_prompt, t_sample, tm=1024, tk=1408):
    M, D = y.shape
    F = u.shape[1]
    tm = _row_tile(n_prompt, t_sample, tm)
    tk = _tile(F, tk, LANE)
    nk = F // tk
    grp = functools.partial(_group_of, tm=tm, n_prompt=n_prompt, t_sample=t_sample)
    return pl.pallas_call(
        functools.partial(_down_resid_kernel, nk=nk),
        out_shape=SDS((M, D), F32),
        grid=(M // tm, nk),
        in_specs=[
            pl.BlockSpec((tm, tk), lambda i, k: (i, k)),
            pl.BlockSpec((tk, D), lambda i, k: (k, 0)),
            pl.BlockSpec((tm, D), lambda i, k: (i, 0), pipeline_mode=pl.Buffered(1)),
            pl.BlockSpec((1, D), lambda i, k: (0, 0)),
            pl.BlockSpec((1, 1, D), lambda i, k: (grp(i), 0, 0)),
        ],
        out_specs=pl.BlockSpec((tm, D), lambda i, k: (i, 0)),
        compiler_params=_cparams("parallel", "arbitrary"),
        name="down_resid",
    )(u, w2b, y, gain.reshape(1, D), gate)


def _gather_sorted_kernel(tr_ref, src_ref, h_hbm, o_ref, xs, sem, *, tm, S, nt):
    i = pl.program_id(0)

    def start_tile(t, slot):
        def issue(r2, c):
            for pr in range(2):
                r = 2 * r2 + pr
                s0 = pl.multiple_of(src_ref[t * tm + r] * S, S)
                pltpu.make_async_copy(h_hbm.at[pl.ds(s0, S)], xs.at[slot, pl.ds(pl.multiple_of(r * S, S), S)],
                                      sem.at[slot]).start(priority=pr)
            return c

        lax.fori_loop(0, tm // 2, issue, 0)

    def finish_tile(slot):
        pltpu.make_async_copy(h_hbm.at[pl.ds(0, tm * S)], xs.at[slot], sem.at[slot]).wait()
        for s in range(S):
            o_ref[:, s * LANE:(s + 1) * LANE] = _from_slab(xs.at[slot], tm, s, S).astype(o_ref.dtype)

    @pl.when(jnp.logical_and(i == 0, tr_ref[0] > 0))
    def _():
        start_tile(0, 0)

    nxt = jnp.minimum(i + 1, nt - 1)
    for slot in range(2):
        @pl.when(jnp.logical_and(jnp.logical_and(i + 1 < nt, tr_ref[nxt] > 0), i % 2 == slot))
        def _():
            start_tile(i + 1, 1 - slot)

    @pl.when(tr_ref[i] == 0)
    def _():
        o_ref[...] = jnp.zeros_like(o_ref)

    for slot in range(2):
        @pl.when(jnp.logical_and(tr_ref[i] > 0, i % 2 == slot))
        def _():
            finish_tile(slot)


def _gather_sorted(h_slab, tile_rows, src_tok, *, tm, D):
    S = D // LANE
    nt = tile_rows.shape[0]
    grid_spec = pltpu.PrefetchScalarGridSpec(
        num_scalar_prefetch=2,
        grid=(nt,),
        in_specs=[pl.BlockSpec(memory_space=pl.ANY)],
        out_specs=pl.BlockSpec((tm, D), lambda i, tr, sr: (i, 0)),
        scratch_shapes=[pltpu.VMEM((2, tm * S, LANE), F32), pltpu.SemaphoreType.DMA((2,))],
    )
    return pl.pallas_call(
        functools.partial(_gather_sorted_kernel, tm=tm, S=S, nt=nt),
        out_shape=SDS((nt * tm, D), BF16),
        grid_spec=grid_spec,
        compiler_params=_cparams("arbitrary"),
        name="gather_sorted",
    )(tile_rows, src_tok, h_slab)


def _moe_gate_up_kernel(te_ref, tr_ref, tf_ref, x_ref, w1_ref, w3_ref, u_ref, wb1, wb3):
    i = pl.program_id(1)

    @pl.when(tf_ref[i] > 0)
    def _():
        wb1[...] = w1_ref[...].astype(BF16)
        wb3[...] = w3_ref[...].astype(BF16)

    @pl.when(tr_ref[i] > 0)
    def _():
        x = x_ref[...]
        u_ref[...] = (_silu(_dot(x, wb1[...])) * _dot(x, wb3[...])).astype(u_ref.dtype)

    @pl.when(tr_ref[i] == 0)
    def _():
        u_ref[...] = jnp.zeros_like(u_ref)


def _moe_gate_up(x_sorted, tile_e, tile_rows, tile_first, w1, w3, layer, *, tm, tn=512):
    R, D = x_sorted.shape
    F = w1.shape[-1]
    tn = _tile(F, tn, LANE)
    wspec = pl.BlockSpec((None, None, D, tn), lambda j, i, te, tr, tf: (layer, te[i], 0, j))
    grid_spec = pltpu.PrefetchScalarGridSpec(
        num_scalar_prefetch=3,
        grid=(F // tn, R // tm),
        in_specs=[pl.BlockSpec((tm, D), lambda j, i, te, tr, tf: (i, 0)), wspec, wspec],
        out_specs=pl.BlockSpec((tm, tn), lambda j, i, te, tr, tf: (i, j)),
        scratch_shapes=[pltpu.VMEM((D, tn), BF16), pltpu.VMEM((D, tn), BF16)],
    )
    return pl.pallas_call(
        _moe_gate_up_kernel,
        out_shape=SDS((R, F), BF16),
        grid_spec=grid_spec,
        compiler_params=_cparams("parallel", "arbitrary"),
        name="moe_gate_up",
    )(tile_e, tile_rows, tile_first, x_sorted, w1, w3)


def _moe_down_kernel(te_ref, tr_ref, dst_ref, u_ref, w_ref, gate_ref, out_hbm, acc, stage, sem,
                     *, nk, nt, tm, S, chunk):
    i = pl.program_id(0)
    k = pl.program_id(1)
    nrows = tr_ref[i]
    base = i * tm

    def row_copy(slot, r, d0):
        return pltpu.make_async_copy(stage.at[slot, pl.ds(pl.multiple_of(r * S, S), S)],
                                     out_hbm.at[pl.ds(d0, S)], sem.at[slot])

    def drain(slot, n):
        for c in range(tm // chunk):
            @pl.when(n >= (c + 1) * chunk)
            def _():
                pltpu.make_async_copy(stage.at[slot, pl.ds(c * chunk * S, chunk * S)],
                                      out_hbm.at[pl.ds(0, chunk * S)], sem.at[slot]).wait()

        def one(r, c):
            row_copy(slot, r, 0).wait()
            return c

        lax.fori_loop((n // chunk) * chunk, n, one, 0)

    @pl.when(nrows > 0)
    def _():
        @pl.when(k == 0)
        def _():
            acc[...] = jnp.zeros_like(acc)

        acc[...] += _dot(u_ref[...], w_ref[0])

        for slot in range(2):
            @pl.when(jnp.logical_and(k == nk - 1, i % 2 == slot))
            def _():
                _to_slab(stage.at[slot], acc[...] * gate_ref[...])

                def issue_row(r, pr):
                    row_copy(slot, r, pl.multiple_of(dst_ref[base + r] * S, S)).start(priority=pr)

                def issue(r2, c):
                    for pr in range(2):
                        issue_row(2 * r2 + pr, pr)
                    return c

                lax.fori_loop(0, nrows // 2, issue, 0)

                @pl.when(nrows % 2 == 1)
                def _():
                    issue_row(nrows - 1, 0)

                @pl.when(i > 0)
                def _():
                    drain(1 - slot, tr_ref[jnp.maximum(i - 1, 0)])

                @pl.when(jnp.logical_or(i == nt - 1, tr_ref[jnp.minimum(i + 1, nt - 1)] == 0))
                def _():
                    drain(slot, nrows)


def _moe_down(u_sorted, tile_e, tile_rows, dst_row, gate_rows, w2b, *, n_tok, tm, tk=1408):
    R, F = u_sorted.shape
    D = w2b.shape[2]
    S = D // LANE
    tk = _tile(F, tk, LANE)
    nk = F // tk

    def kidx(i, k, tr):
        return jnp.where(tr[i] > 0, k, nk - 1)

    grid_spec = pltpu.PrefetchScalarGridSpec(
        num_scalar_prefetch=3,
        grid=(R // tm, nk),
        in_specs=[
            pl.BlockSpec((tm, tk), lambda i, k, te, tr, ds: (i, kidx(i, k, tr))),
            pl.BlockSpec((1, tk, D), lambda i, k, te, tr, ds: (te[i], kidx(i, k, tr), 0)),
            pl.BlockSpec((tm, 1), lambda i, k, te, tr, ds: (i, 0)),
        ],
        out_specs=pl.BlockSpec(memory_space=pl.ANY),
        scratch_shapes=[pltpu.VMEM((tm, D), F32), pltpu.VMEM((2, tm * S, LANE), F32),
                        pltpu.SemaphoreType.DMA((2,))],
    )
    return pl.pallas_call(
        functools.partial(_moe_down_kernel, nk=nk, nt=R // tm, tm=tm, S=S, chunk=_tile(tm, 64)),
        out_shape=SDS((2 * n_tok * S, LANE), F32),
        grid_spec=grid_spec,
        compiler_params=pltpu.CompilerParams(dimension_semantics=("arbitrary", "arbitrary"),
                                             vmem_limit_bytes=VMEM_LIMIT_BYTES, has_side_effects=True),
        name="moe_down",
    )(tile_e, tile_rows, dst_row, u_sorted, w2b, gate_rows)


def _combine_kernel(a_ref, b_ref, y_ref, g_ref, gate_ref, o_ref, *, tm, S):
    parts = [_from_slab(a_ref, tm, s, S) + _from_slab(b_ref, tm, s, S) for s in range(S)]
    ss = parts[0] * parts[0]
    for t in parts[1:]:
        ss = ss + t * t
    r = lax.rsqrt(jnp.sum(ss, axis=-1, keepdims=True) / (S * LANE) + EPS)
    for s, t in enumerate(parts):
        c = slice(s * LANE, (s + 1) * LANE)
        o_ref[:, c] = y_ref[:, c] + gate_ref[0][:, c] * (t * r * g_ref[:, c])


def _moe_combine(slots, y, gain, gate, *, n_prompt, t_sample, tm=256):
    M, D = y.shape
    S = D // LANE
    tm = _row_tile(n_prompt, t_sample, tm)
    nb = M // tm
    grp = functools.partial(_group_of, tm=tm, n_prompt=n_prompt, t_sample=t_sample)
    return pl.pallas_call(
        functools.partial(_combine_kernel, tm=tm, S=S),
        out_shape=SDS((M, D), F32),
        grid=(nb,),
        in_specs=[
            pl.BlockSpec((tm * S, LANE), lambda i: (i, 0)),
            pl.BlockSpec((tm * S, LANE), lambda i: (nb + i, 0)),
            pl.BlockSpec((tm, D), lambda i: (i, 0)),
            pl.BlockSpec((1, D), lambda i: (0, 0)),
            pl.BlockSpec((1, 1, D), lambda i: (grp(i), 0, 0)),
        ],
        out_specs=pl.BlockSpec((tm, D), lambda i: (i, 0)),
        compiler_params=_cparams("parallel"),
        name="moe_combine",
    )(slots, slots, y, gain.reshape(1, D), gate)


def _dispatch_tables(sel, gw, tm):
    M = sel.shape[0]
    E = N_EXPERTS
    i32 = jnp.int32
    e_flat = sel.reshape(-1)
    pair = jnp.arange(2 * M, dtype=i32)
    onehot = (e_flat[:, None] == jnp.arange(E, dtype=i32)[None, :]).astype(i32)
    running = jnp.cumsum(onehot, axis=0)
    rank = jnp.sum(running * onehot, axis=1) - 1
    counts = running[-1]
    padded = ((counts + tm - 1) // tm) * tm
    ends_p = jnp.cumsum(padded)
    starts_p = ends_p - padded
    pos = starts_p[e_flat] + rank
    R = 2 * M + E * tm
    payload = jnp.stack([pair // 2, (pair % 2) * M + pair // 2,
                         lax.bitcast_convert_type(gw.reshape(-1), i32)], axis=1)
    table = jnp.zeros((R, 3), i32).at[pos].set(payload)
    src_tok, dst_row = table[:, 0], table[:, 1]
    gate_rows = lax.bitcast_convert_type(table[:, 2], F32)
    tile_start = jnp.arange(R // tm, dtype=i32) * tm
    tile_e = jnp.minimum(jnp.sum(ends_p[None, :] <= tile_start[:, None], axis=1), E - 1).astype(i32)
    tile_rows = jnp.clip(starts_p[tile_e] + counts[tile_e] - tile_start, 0, tm).astype(i32)
    tile_first = jnp.concatenate([jnp.ones((1,), i32), (tile_e[1:] != tile_e[:-1]).astype(i32)])
    return tile_e, tile_rows, tile_first, src_tok, dst_row, gate_rows.reshape(R, 1)


def _attn_kernel(*refs, n_parts, k_modes, v_mode, hb, scale, nk):
    q_refs = refs[:n_parts]
    k_refs = refs[n_parts:2 * n_parts]
    v_ref = refs[2 * n_parts]
    o_ref = refs[2 * n_parts + 1]
    m_sc, l_sc, acc_sc = refs[2 * n_parts + 2:]
    j = pl.program_id(3)
    c_exp = scale * math.log2(math.e)

    @pl.when(j == 0)
    def _():
        m_sc[...] = jnp.full_like(m_sc, -jnp.inf)
        l_sc[...] = jnp.zeros_like(l_sc)
        acc_sc[...] = jnp.zeros_like(acc_sc)

    def block(ref, mode, hh):
        x = ref[:, hh * LANE:(hh + 1) * LANE] if mode == "head" else ref[...]
        return x.astype(BF16)

    for hh in range(hb):
        q = [block(q_refs[p], "head", hh) for p in range(n_parts)]
        k = [block(k_refs[p], k_modes[p], hh) for p in range(n_parts)]
        q = q[0] if n_parts == 1 else jnp.concatenate(q, axis=1)
        k = k[0] if n_parts == 1 else jnp.concatenate(k, axis=1)
        s = _dot_nt(q, k)
        tiles = [s[:, c * LANE:(c + 1) * LANE] for c in range(s.shape[1] // LANE)]
        m_lane = tiles[0]
        for t in tiles[1:]:
            m_lane = jnp.maximum(m_lane, t)
        m_prev = m_sc[hh]
        m_new = jnp.maximum(m_prev, jnp.max(m_lane, axis=-1, keepdims=True))
        alpha = jnp.exp2((m_prev - m_new) * c_exp)
        p = [jnp.exp2((t - m_new) * c_exp) for t in tiles]
        l_lane = p[0]
        for t in p[1:]:
            l_lane = l_lane + t
        l_sc[hh] = alpha * l_sc[hh] + l_lane
        pb = jnp.concatenate([t.astype(BF16) for t in p], axis=1)
        acc_sc[hh] = alpha * acc_sc[hh] + _dot(pb, block(v_ref, v_mode, hh))
        m_sc[hh] = m_new

    @pl.when(j == nk - 1)
    def _():
        for hh in range(hb):
            l = jnp.sum(l_sc[hh], axis=-1, keepdims=True)
            o_ref[:, hh * LANE:(hh + 1) * LANE] = (acc_sc[hh] / l).astype(o_ref.dtype)


def _attention(q_parts, k_parts, v_part, *, B, T, S, H, hb, q_row0, k_row0, tq, tk, scale):
    tq = _tile(T, tq)
    tk = _tile(S, tk)
    nq, nk = T // tq, S // tk
    qb0, kb0 = q_row0 // tq, k_row0 // tk

    def q_spec(off):
        return pl.BlockSpec((tq, hb * LANE), lambda b, g, i, j: (qb0 + b * nq + i, off // hb + g))

    def k_spec(off, mode):
        if mode == "head":
            return pl.BlockSpec((tk, hb * LANE), lambda b, g, i, j: (kb0 + b * nk + j, off // hb + g))
        return pl.BlockSpec((tk, LANE), lambda b, g, i, j: (kb0 + b * nk + j, off + (g if mode == "group" else 0)))

    in_specs = [q_spec(off) for (_, off) in q_parts]
    in_specs += [k_spec(off, mode) for (_, off, mode) in k_parts]
    in_specs += [k_spec(v_part[1], v_part[2])]
    args = [a for (a, _) in q_parts] + [a for (a, _, _) in k_parts] + [v_part[0]]
    return pl.pallas_call(
        functools.partial(_attn_kernel, n_parts=len(q_parts), k_modes=[m for (_, _, m) in k_parts],
                          v_mode=v_part[2], hb=hb, scale=scale, nk=nk),
        out_shape=SDS((B * T, H * LANE), BF16),
        grid=(B, H // hb, nq, nk),
        in_specs=in_specs,
        out_specs=pl.BlockSpec((tq, hb * LANE), lambda b, g, i, j: (b * nq + i, g)),
        scratch_shapes=[pltpu.VMEM((hb, tq, LANE), F32), pltpu.VMEM((hb, tq, LANE), F32),
                        pltpu.VMEM((hb, tq, LANE), F32)],
        compiler_params=_cparams("parallel", "parallel", "parallel", "arbitrary"),
        name="attention",
    )(*args)


def _rmsnorm_cols_kernel(x_ref, g_ref, o_ref):
    o_ref[...] = (_rms(x_ref[...]) * g_ref[...]).astype(o_ref.dtype)


def _rmsnorm_cols(x, gain, *, col_block, width, out_dtype, tm=1024):
    M = x.shape[0]
    tm = _tile(M, tm)
    return pl.pallas_call(
        _rmsnorm_cols_kernel,
        out_shape=SDS((M, width), out_dtype),
        grid=(M // tm,),
        in_specs=[pl.BlockSpec((tm, width), lambda i: (i, col_block)),
                  pl.BlockSpec((1, width), lambda i: (0, 0))],
        out_specs=pl.BlockSpec((tm, width), lambda i: (i, 0)),
        compiler_params=_cparams("parallel"),
        name="rmsnorm_cols",
    )(x, gain.reshape(1, width))


def _head_rope_kernel(x_ref, g_ref, cos_ref, sin_ref, o_ref, *, norm, half):
    x = x_ref[...]
    if norm:
        x = _rms(x) * g_ref[0]
    lane = lax.broadcasted_iota(jnp.int32, x.shape, 1)
    first = (lane % (2 * half)) < half
    rot = jnp.where(first, -pltpu.roll(x, LANE - half, 1), pltpu.roll(x, half, 1))
    o_ref[...] = (x * cos_ref[...] + rot * sin_ref[...]).astype(o_ref.dtype)


def _head_rope(x, gains, cos, sin, *, col_block0, n_heads, norm, half, out_dtype, n_rows=None, tm=1024):
    M = x.shape[0] if n_rows is None else n_rows
    tm = _tile(M, tm)
    return pl.pallas_call(
        functools.partial(_head_rope_kernel, norm=norm, half=half),
        out_shape=SDS((M, n_heads * LANE), out_dtype),
        grid=(M // tm, n_heads),
        in_specs=[
            pl.BlockSpec((tm, LANE), lambda i, h: (i, col_block0 + h)),
            pl.BlockSpec((1, 1, LANE), lambda i, h: (h, 0, 0)),
            pl.BlockSpec((tm, LANE), lambda i, h: (i, 0)),
            pl.BlockSpec((tm, LANE), lambda i, h: (i, 0)),
        ],
        out_specs=pl.BlockSpec((tm, LANE), lambda i, h: (i, h)),
        compiler_params=_cparams("parallel", "parallel"),
        name="head_rope",
    )(x, gains, cos, sin)


def _rope_tables(n_prompt, dec_batch, t_sample, half):
    tok = jnp.arange(t_sample, dtype=jnp.int32)
    row, col = tok // GRID_W, tok % GRID_W
    inv = ROPE_THETA ** (-jnp.arange(half, dtype=F32) / half)

    def cs(pos):
        ang = pos.astype(F32)[:, None] * inv[None, :]
        return (jnp.concatenate([jnp.cos(ang), jnp.cos(ang)], -1),
                jnp.concatenate([jnp.sin(ang), jnp.sin(ang)], -1))

    cr, sr = cs(row)
    cc, sc = cs(col)
    pad = LANE - 4 * half
    cos = jnp.concatenate([cr, cc, jnp.ones((t_sample, pad), F32)], -1)
    sin = jnp.concatenate([sr, sc, jnp.zeros((t_sample, pad), F32)], -1)
    cos = jnp.concatenate([jnp.ones((n_prompt, LANE), F32), jnp.tile(cos, (dec_batch, 1))], 0)
    sin = jnp.concatenate([jnp.zeros((n_prompt, LANE), F32), jnp.tile(sin, (dec_batch, 1))], 0)
    return cos, sin


def _hgrn_constants(C, reverse):
    L = int(np.log2(C))
    t = np.arange(C)[:, None]
    u = np.arange(C)[None, :]
    Ws, Ms = [], []
    for lvl in range(L):
        b = 1 << lvl
        grp = t // (2 * b)
        if not reverse:
            bnd = grp * 2 * b + b - 1
            qside = (t % (2 * b)) >= b
            W = np.where(qside, (u > bnd) & (u <= t), (u > t) & (u <= bnd))
            kside_s = (u % (2 * b)) < b
        else:
            bnd = grp * 2 * b + b
            qside = (t % (2 * b)) < b
            W = np.where(qside, (u >= t) & (u < bnd), (u >= bnd) & (u < t))
            kside_s = (u % (2 * b)) >= b
        Ws.append(W)
        Ms.append(qside & kside_s & (grp == u // (2 * b)))
    if not reverse:
        Ws += [u <= t, u > t]
    else:
        Ws += [u >= t, u < t]
    Ms.append(t == u)
    W_all = jnp.asarray(np.concatenate(Ws, 0).astype(np.float32), BF16)
    M_all = jnp.asarray(np.stack(Ms).astype(np.float32), F32)
    return W_all, M_all


def _hgrn_chain(q, v, z, lb, W_all, m_ref, st, *, C, L, edge_row):
    f = lb + (1.0 - lb) * jax.nn.sigmoid(z)
    k = 1.0 - f
    lf = jnp.log(f)
    hi = lf.astype(BF16)
    r1 = lf - hi.astype(F32)
    mid = r1.astype(BF16)
    lo = (r1 - mid.astype(F32)).astype(BF16)
    d3 = _dot(W_all, jnp.concatenate([hi, mid, lo], axis=1))
    dn = d3[:, :LANE] + d3[:, LANE:2 * LANE] + d3[:, 2 * LANE:]
    e_all = jnp.exp(dn)
    qb = q.astype(BF16)
    a = m_ref[L] * _dot_nt(qb, k.astype(BF16))
    for lvl in range(L):
        e = e_all[lvl * C:(lvl + 1) * C]
        a = a + m_ref[lvl] * _dot_nt((q * e).astype(BF16), (k * e).astype(BF16))
    eq = e_all[L * C:(L + 1) * C]
    ek = e_all[(L + 1) * C:(L + 2) * C]
    vb = v.astype(BF16)
    o = _dot(a.astype(BF16), vb) + _dot_nt((q * eq).astype(BF16), st.astype(BF16))
    g_edge = dn[L * C + edge_row:L * C + edge_row + 1]
    st_new = st * jnp.exp(g_edge) + _dot_tn(vb, (k * ek).astype(BF16))
    return o, st_new


def _hgrn_kernel(*refs, C, L, nc, hb, has_init):
    (qf_ref, vf_ref, zf_ref, qb_ref, vb_ref, zb_ref, lbf_ref, lbb_ref,
     wf_ref, mf_ref, wb_ref, mb_ref) = refs[:12]
    rest = refs[12:]
    if has_init:
        s0_ref, rest = rest[0], rest[1:]
    of_ref, ob_ref, sfin_ref, stf, stb = rest
    c = pl.program_id(2)

    @pl.when(c == 0)
    def _():
        for hh in range(hb):
            if has_init:
                stf[hh] = s0_ref[0, 0, hh]
                stb[hh] = s0_ref[0, 1, hh]
            else:
                stf[hh] = jnp.zeros((LANE, LANE), F32)
                stb[hh] = jnp.zeros((LANE, LANE), F32)

    for hh in range(hb):
        cols = slice(hh * LANE, (hh + 1) * LANE)
        o, s_new = _hgrn_chain(qf_ref[:, cols], vf_ref[:, cols], zf_ref[:, cols], lbf_ref[0][:, cols],
                               wf_ref[...], mf_ref, stf[hh], C=C, L=L, edge_row=C - 1)
        of_ref[:, cols] = o
        stf[hh] = s_new
        o, s_new = _hgrn_chain(qb_ref[:, cols], vb_ref[:, cols], zb_ref[:, cols], lbb_ref[0][:, cols],
                               wb_ref[...], mb_ref, stb[hh], C=C, L=L, edge_row=0)
        ob_ref[:, cols] = o
        stb[hh] = s_new

    @pl.when(c == nc - 1)
    def _():
        for hh in range(hb):
            sfin_ref[0, 0, hh] = stf[hh].T
            sfin_ref[0, 1, hh] = stb[hh].T


def _hgrn_scan(proj, lb, s0t, *, B, T, row0, C=128, hb=4):
    D = proj.shape[1] // 5
    H = D // LANE
    C = _tile(T, C)
    L = int(np.log2(C))
    assert 1 << L == C
    nc = T // C
    rb0 = row0 // C
    HB = H // hb
    W = hb * LANE
    wf, mf = _hgrn_constants(C, False)
    wb, mb = _hgrn_constants(C, True)

    def fwd(col0):
        return pl.BlockSpec((C, W), lambda b, g, c: (rb0 + b * nc + c, col0 // hb + g))

    def bwd(col0):
        return pl.BlockSpec((C, W), lambda b, g, c: (rb0 + b * nc + nc - 1 - c, col0 // hb + g))

    def const(a):
        return pl.BlockSpec(a.shape, lambda b, g, c: (0,) * a.ndim)

    in_specs = [fwd(0), fwd(H), fwd(3 * H), bwd(0), bwd(H), bwd(4 * H),
                pl.BlockSpec((1, 1, W), lambda b, g, c: (0, 0, g)),
                pl.BlockSpec((1, 1, W), lambda b, g, c: (1, 0, g)),
                const(wf), const(mf), const(wb), const(mb)]
    args = [proj, proj, proj, proj, proj, proj, lb, lb, wf, mf, wb, mb]
    if s0t is not None:
        in_specs.append(pl.BlockSpec((1, 2, hb, LANE, LANE), lambda b, g, c: (b, 0, g, 0, 0)))
        args.append(s0t)
    return pl.pallas_call(
        functools.partial(_hgrn_kernel, C=C, L=L, nc=nc, hb=hb, has_init=s0t is not None),
        out_shape=(SDS((B * T, D), F32), SDS((B * T, D), F32), SDS((B, 2, H, LANE, LANE), F32)),
        grid=(B, HB, nc),
        in_specs=in_specs,
        out_specs=(
            pl.BlockSpec((C, W), lambda b, g, c: (b * nc + c, g)),
            pl.BlockSpec((C, W), lambda b, g, c: (b * nc + nc - 1 - c, g)),
            pl.BlockSpec((1, 2, hb, LANE, LANE), lambda b, g, c: (b, 0, g, 0, 0)),
        ),
        scratch_shapes=[pltpu.VMEM((hb, LANE, LANE), F32), pltpu.VMEM((hb, LANE, LANE), F32)],
        compiler_params=_cparams("parallel", "parallel", "arbitrary"),
        name="hgrn_scan",
    )(*args)


def _hgrn_gate_kernel(of_ref, ob_ref, g_ref, w_ref, o_ref):
    o = _rms(of_ref[...] + ob_ref[...]) * w_ref[0]
    o_ref[...] = (o * _silu(g_ref[...])).astype(o_ref.dtype)


def _hgrn_gate(o_f, o_b, proj, o_norm, *, row0, tm=1024):
    M, D = o_f.shape
    H = D // LANE
    tm = _tile(math.gcd(M, row0) if row0 else M, tm)
    rb0 = row0 // tm
    blk = pl.BlockSpec((tm, LANE), lambda i, h: (i, h))
    return pl.pallas_call(
        _hgrn_gate_kernel,
        out_shape=SDS((M, D), BF16),
        grid=(M // tm, H),
        in_specs=[blk, blk,
                  pl.BlockSpec((tm, LANE), lambda i, h: (rb0 + i, 2 * H + h)),
                  pl.BlockSpec((1, 1, LANE), lambda i, h: (h, 0, 0))],
        out_specs=blk,
        compiler_params=_cparams("parallel", "parallel"),
        name="hgrn_gate",
    )(o_f, o_b, proj, o_norm.reshape(H, 1, LANE))


def kernel(x_prompt, x_sample, c, cache_mla_ckv, cache_mla_kpe, state_hgrn, cache_gqa_k, cache_gqa_v, c_ctx, ada_w, ada_b, norm_w, mla_wq_a, mla_q_norm, mla_wq_b, mla_wkv_a, mla_kv_norm, mla_wkv_b, mla_wo, hgrn_w_in, hgrn_lb_logits, hgrn_o_norm, hgrn_wo, gqa_w_qkv, gqa_q_norm, gqa_k_norm, gqa_wo, ffn_w1, ffn_w3, ffn_w2, moe_router, moe_w1, moe_w3, moe_w2):
    Bp, Tp, D = x_prompt.shape
    Bs, Ts, _ = x_sample.shape
    P = cache_mla_ckv.shape[2]
    depth = ada_w.shape[0]
    NP, NS = Bp * Tp, Bs * Ts
    M = NP + NS
    G = 1 + Bs
    assert D == HG_HEADS * LANE
    rows = dict(n_prompt=NP, t_sample=Ts)

    y = jnp.concatenate([x_prompt.reshape(NP, D), x_sample.reshape(NS, D)], axis=0)
    cond8 = jnp.zeros((8, D), F32).at[0].set(c_ctx).at[1:G].set(c)
    mod = _modulation(cond8, ada_w, ada_b).reshape(depth, 8, 6, D)
    mod = jnp.transpose(mod, (0, 2, 1, 3))[:, :, :G, None, :]

    tm_tok = _row_tile(NP, Ts, 512)
    cos_mla, sin_mla = _rope_tables(NP, Bs, Ts, MLA_ROPE // 4)
    cos_gqa, sin_gqa = _rope_tables(NP, Bs, Ts, GQA_HD // 4)
    ones_g = jnp.ones((MLA_HEADS, 1, LANE), F32)

    ckv_list, kpe_list, hg_list, gk_list, gv_list = [], [], [], [], []
    for layer in range(depth):
        kind, j = layer % 3, layer // 3
        nw = norm_w[layer]
        sh1, sc1, g1, sh2, sc2, g2 = (mod[layer, k] for k in range(6))
        h = _adaln_in(y, nw[0], sh1, sc1, tm=tm_tok, **rows)

        if kind == 0:
            Hm = MLA_HEADS
            w_a = jnp.concatenate([mla_wq_a[j], mla_wkv_a[j],
                                   jnp.zeros((D, LANE - MLA_ROPE), F32)], axis=1)
            a = _mm(h, w_a, out_dtype=F32)
            ql, kvl = mla_wq_a.shape[2], mla_kv_norm.shape[1]
            qn = _rmsnorm_cols(a, mla_q_norm[j], col_block=0, width=ql, out_dtype=BF16)
            ckv = _rmsnorm_cols(a, mla_kv_norm[j], col_block=ql // kvl, width=kvl, out_dtype=F32)
            wqb = mla_wq_b[j].reshape(ql, Hm, MLA_NOPE + MLA_ROPE)
            wqb = jnp.concatenate([
                wqb[:, :, :MLA_NOPE].reshape(ql, Hm * MLA_NOPE),
                jnp.pad(wqb[:, :, MLA_NOPE:], ((0, 0), (0, 0), (0, LANE - MLA_ROPE))).reshape(ql, Hm * LANE),
            ], axis=1)
            q = _mm(qn, wqb, out_dtype=F32)
            pe_blk = (ql + kvl) // LANE
            q_pe = _head_rope(q, ones_g, cos_mla, sin_mla, col_block0=Hm, n_heads=Hm, norm=False,
                              half=MLA_ROPE // 4, out_dtype=BF16)
            k_pe = _head_rope(a, ones_g, cos_mla, sin_mla, col_block0=pe_blk, n_heads=1, norm=False,
                              half=MLA_ROPE // 4, out_dtype=BF16)
            wkvb = mla_wkv_b[j].reshape(kvl, Hm, MLA_NOPE + MLA_V)
            wkvb = jnp.concatenate([wkvb[:, :, :MLA_NOPE].reshape(kvl, Hm * MLA_NOPE),
                                    wkvb[:, :, MLA_NOPE:].reshape(kvl, Hm * MLA_V)], axis=1)
            ckv_s = jnp.concatenate([ckv[NP:].reshape(Bs, Ts, kvl), cache_mla_ckv[:, j]], axis=1)
            c_all = jnp.concatenate([ckv[:NP], ckv_s.reshape(Bs * (Ts + P), kvl)], axis=0)
            kv = _mm(c_all, wkvb, out_dtype=BF16)
            kpe_ctx = jnp.pad(cache_mla_kpe[:, j], ((0, 0), (0, 0), (0, LANE - MLA_ROPE))).astype(BF16)
            kpe_s = jnp.concatenate([k_pe[NP:].reshape(Bs, Ts, LANE), kpe_ctx], axis=1)
            kpe_all = jnp.concatenate([k_pe[:NP], kpe_s.reshape(Bs * (Ts + P), LANE)], axis=0)
            scale = (MLA_NOPE + MLA_ROPE) ** -0.5
            qp = [(q, 0), (q_pe, 0)]
            kp = [(kv, 0, "head"), (kpe_all, 0, "shared")]
            vp = (kv, Hm, "head")
            mix_p = _attention(qp, kp, vp, B=Bp, T=Tp, S=Tp, H=Hm, hb=4, q_row0=0, k_row0=0,
                               tq=256, tk=256, scale=scale)
            mix_s = _attention(qp, kp, vp, B=Bs, T=Ts, S=Ts + P, H=Hm, hb=4, q_row0=NP, k_row0=NP,
                               tq=1024, tk=512, scale=scale)
            w_o = mla_wo
            ckv_list.append(ckv[:NP].reshape(Bp, Tp, kvl))
            kpe_list.append(a[:NP, ql + kvl:ql + kvl + MLA_ROPE].reshape(Bp, Tp, MLA_ROPE))
        elif kind == 1:
            H = HG_HEADS
            p = jax.nn.softmax(hgrn_lb_logits.astype(F32), axis=1)
            cum = jnp.cumsum(p, axis=1)
            lb = (cum - cum[:, :1])[:, layer].reshape(2, 1, D)
            proj = _mm(h, hgrn_w_in, layer=j, out_dtype=F32)
            of_p, ob_p, st_p = _hgrn_scan(proj, lb, None, B=Bp, T=Tp, row0=0)
            s0t = jnp.swapaxes(state_hgrn[:, j].astype(F32), -1, -2)
            of_s, ob_s, _ = _hgrn_scan(proj, lb, s0t, B=Bs, T=Ts, row0=NP)
            mix_p = _hgrn_gate(of_p, ob_p, proj, hgrn_o_norm[j], row0=0)
            mix_s = _hgrn_gate(of_s, ob_s, proj, hgrn_o_norm[j], row0=NP)
            w_o = hgrn_wo
            hg_list.append(st_p)
        else:
            Hq, Hk = GQA_HEADS, GQA_KV_HEADS
            qkv = _mm(h, gqa_w_qkv, layer=j, out_dtype=F32)
            gains = jnp.concatenate([jnp.tile(gqa_q_norm[j][None], (Hq, 1)),
                                     jnp.tile(gqa_k_norm[j][None], (Hk, 1))], axis=0).reshape(Hq + Hk, 1, LANE)
            k_plain = _head_rope(qkv, gains[Hq:], cos_gqa, sin_gqa, col_block0=Hq, n_heads=Hk, norm=True,
                                 half=GQA_HD // 4, out_dtype=F32, n_rows=NP)
            qk = _head_rope(qkv, gains, cos_gqa, sin_gqa, col_block0=0, n_heads=Hq + Hk, norm=True,
                            half=GQA_HD // 4, out_dtype=BF16)
            kw = Hk * LANE
            k_new = qk[:, Hq * LANE:]
            v_new = qkv[:, (Hq + Hk) * LANE:].astype(BF16)
            k_s = jnp.concatenate([k_new[NP:].reshape(Bs, Ts, kw),
                                   cache_gqa_k[:, j].reshape(Bs, P, kw).astype(BF16)], axis=1)
            v_s = jnp.concatenate([v_new[NP:].reshape(Bs, Ts, kw),
                                   cache_gqa_v[:, j].reshape(Bs, P, kw).astype(BF16)], axis=1)
            k_all = jnp.concatenate([k_new[:NP], k_s.reshape(Bs * (Ts + P), kw)], axis=0)
            v_all = jnp.concatenate([v_new[:NP], v_s.reshape(Bs * (Ts + P), kw)], axis=0)
            scale = GQA_HD ** -0.5
            qp = [(qk, 0)]
            kp = [(k_all, 0, "group")]
            vp = (v_all, 0, "group")
            mix_p = _attention(qp, kp, vp, B=Bp, T=Tp, S=Tp, H=Hq, hb=Hq // Hk, q_row0=0, k_row0=0,
                               tq=256, tk=256, scale=scale)
            mix_s = _attention(qp, kp, vp, B=Bs, T=Ts, S=Ts + P, H=Hq, hb=Hq // Hk, q_row0=NP, k_row0=NP,
                               tq=1024, tk=512, scale=scale)
            w_o = gqa_wo
            gk_list.append(k_plain[:NP].reshape(Bp, Tp, Hk, GQA_HD))
            gv_list.append(qkv[:NP, (Hq + Hk) * LANE:].reshape(Bp, Tp, Hk, GQA_HD))

        y = _mm_resid(mix_p, mix_s, w_o, y, nw[1], g1, layer=j, **rows)

        fi = layer // 2
        if layer % 2 == 0:
            h2 = _adaln_in(y, nw[2], sh2, sc2, tm=tm_tok, **rows)
            u = _gate_up(h2, ffn_w1, ffn_w3, fi)
            y = _down_resid(u, _cast_bf16(ffn_w2, fi), y, nw[3], g2, **rows)
        else:
            h2, sel, gw = _adaln_in(y, nw[2], sh2, sc2, tm=tm_tok, router=moe_router[fi], **rows)
            tm_e = _tile(M, 1024)
            tile_e, tile_rows, tile_first, src_tok, dst_row, gate_rows = _dispatch_tables(
                sel[:, :2], gw[:, :2], tm_e)
            x_sorted = _gather_sorted(h2, tile_rows, src_tok, tm=tm_e, D=D)
            u = _moe_gate_up(x_sorted, tile_e, tile_rows, tile_first, moe_w1, moe_w3, fi, tm=tm_e)
            slots = _moe_down(u, tile_e, tile_rows, dst_row, gate_rows, _cast_bf16(moe_w2, fi),
                              n_tok=M, tm=tm_e)
            y = _moe_combine(slots, y, nw[3], g2, **rows)

    return (y[:NP].reshape(Bp, Tp, D), y[NP:].reshape(Bs, Ts, D),
            jnp.stack(ckv_list, axis=1), jnp.stack(kpe_list, axis=1), jnp.stack(hg_list, axis=1),
            jnp.stack(gk_list, axis=1), jnp.stack(gv_list, axis=1))
```

```python
import functools
import math

import numpy as np
import jax
import jax.numpy as jnp
from jax import lax
from jax.experimental import pallas as pl
from jax.experimental.pallas import tpu as pltpu

F32 = jnp.float32
BF16 = jnp.bfloat16
EPS = 1e-6
LANE = 128

GRID_W = 64
ROPE_THETA = 10000.0
MLA_HEADS, MLA_NOPE, MLA_ROPE, MLA_V = 16, 128, 64, 128
HG_HEADS = 16
GQA_HEADS, GQA_KV_HEADS, GQA_HD = 16, 4, 128
N_EXPERTS = 8

VMEM_LIMIT_BYTES = 56 * 1024 * 1024

SDS = jax.ShapeDtypeStruct


def _cparams(*sem):
    return pltpu.CompilerParams(dimension_semantics=sem, vmem_limit_bytes=VMEM_LIMIT_BYTES)


def _tile(n, pref, step=8):
    t = min(pref, n)
    while n % t:
        t -= step
    return t


def _dot(a, b):
    return jnp.dot(a, b, preferred_element_type=F32)


def _dot_nt(a, b):
    return lax.dot_general(a, b, (((1,), (1,)), ((), ())), preferred_element_type=F32)


def _dot_tn(a, b):
    return lax.dot_general(a, b, (((0,), (0,)), ((), ())), preferred_element_type=F32)


def _rms(x):
    return x * lax.rsqrt(jnp.mean(x * x, axis=-1, keepdims=True) + EPS)


def _silu(x):
    return x * jax.nn.sigmoid(x)


def _row_tile(n_prompt, t_sample, pref):
    return _tile(math.gcd(n_prompt, t_sample), pref)


def _group_of(i, tm, n_prompt, t_sample):
    r = i * tm
    return jnp.where(r < n_prompt, 0, 1 + (r - n_prompt) // t_sample)


def _mod_kernel(c_ref, w_ref, b_ref, o_ref):
    s = _silu(c_ref[...]).astype(BF16)
    o_ref[0] = _dot(s, w_ref[0].astype(BF16)) + b_ref[0]


def _modulation(cond8, ada_w, ada_b):
    L, D, N6 = ada_w.shape
    tn = _tile(N6, 1024)
    return pl.pallas_call(
        _mod_kernel,
        out_shape=SDS((L, 8, N6), F32),
        grid=(L, N6 // tn),
        in_specs=[
            pl.BlockSpec((8, D), lambda l, j: (0, 0)),
            pl.BlockSpec((1, D, tn), lambda l, j: (l, 0, j)),
            pl.BlockSpec((1, 1, tn), lambda l, j: (l, 0, j)),
        ],
        out_specs=pl.BlockSpec((1, 8, tn), lambda l, j: (l, 0, j)),
        compiler_params=_cparams("parallel", "parallel"),
        name="modulation",
    )(cond8, ada_w, ada_b.reshape(L, 1, N6))


def _adaln_in_kernel(y_ref, g_ref, sh_ref, sc_ref, h_ref):
    h = (_rms(y_ref[...]) * g_ref[...]) * (1.0 + sc_ref[0]) + sh_ref[0]
    h_ref[...] = h.astype(h_ref.dtype)


def _to_slab(ref, x):
    rows, D = x.shape
    S = D // LANE
    for s in range(S):
        ref[pl.ds(s, rows, stride=S), :] = x[:, s * LANE:(s + 1) * LANE]


def _from_slab(ref, rows, s, S):
    return ref[pl.ds(s, rows, stride=S), :]


def _adaln_route_kernel(y_ref, g_ref, sh_ref, sc_ref, r_ref, h_ref, sel_ref, gw_ref):
    h = (_rms(y_ref[...]) * g_ref[...]) * (1.0 + sc_ref[0]) + sh_ref[0]
    _to_slab(h_ref, h)
    logits = lax.dot_general(h, r_ref[...], (((1,), (0,)), ((), ())),
                             precision=lax.Precision.HIGHEST, preferred_element_type=F32)
    lane = lax.broadcasted_iota(jnp.int32, logits.shape, 1)
    neg = jnp.float32(-jnp.inf)
    logits = jnp.where(lane < N_EXPERTS, logits, neg)
    lane_f = lane.astype(F32)
    m1 = jnp.max(logits, axis=-1, keepdims=True)
    i1 = jnp.min(jnp.where(logits == m1, lane_f, float(LANE)), axis=-1, keepdims=True)
    rest = jnp.where(lane_f == i1, neg, logits)
    m2 = jnp.max(rest, axis=-1, keepdims=True)
    i2 = jnp.min(jnp.where(rest == m2, lane_f, float(LANE)), axis=-1, keepdims=True)
    e = jnp.exp(m2 - m1)
    w1 = 1.0 / (1.0 + e)
    w2 = e / (1.0 + e)
    i1, i2 = i1.astype(jnp.int32), i2.astype(jnp.int32)
    sel_ref[...] = jnp.where(lane == 0, i1, jnp.where(lane == 1, i2, 0))
    gw_ref[...] = jnp.where(lane == 0, w1, jnp.where(lane == 1, w2, 0.0))


def _adaln_in(y, gain, shift, scale, *, n_prompt, t_sample, tm, router=None):
    M, D = y.shape
    grp = functools.partial(_group_of, tm=tm, n_prompt=n_prompt, t_sample=t_sample)
    in_specs = [
        pl.BlockSpec((tm, D), lambda i: (i, 0)),
        pl.BlockSpec((1, D), lambda i: (0, 0)),
        pl.BlockSpec((1, 1, D), lambda i: (grp(i), 0, 0)),
        pl.BlockSpec((1, 1, D), lambda i: (grp(i), 0, 0)),
    ]
    args = [y, gain.reshape(1, D), shift, scale]
    if router is None:
        return pl.pallas_call(
            _adaln_in_kernel,
            out_shape=SDS((M, D), BF16),
            grid=(M // tm,),
            in_specs=in_specs,
            out_specs=pl.BlockSpec((tm, D), lambda i: (i, 0)),
            compiler_params=_cparams("parallel"),
            name="adaln_in",
        )(*args)
    r_pad = jnp.pad(router, ((0, 0), (0, LANE - router.shape[1])))
    return pl.pallas_call(
        _adaln_route_kernel,
        out_shape=(SDS((M * (D // LANE), LANE), F32), SDS((M, LANE), jnp.int32), SDS((M, LANE), F32)),
        grid=(M // tm,),
        in_specs=in_specs + [pl.BlockSpec((D, LANE), lambda i: (0, 0))],
        out_specs=(
            pl.BlockSpec((tm * (D // LANE), LANE), lambda i: (i, 0)),
            pl.BlockSpec((tm, LANE), lambda i: (i, 0)),
            pl.BlockSpec((tm, LANE), lambda i: (i, 0)),
        ),
        compiler_params=_cparams("parallel"),
        name="adaln_route",
    )(*args, r_pad)


def _mm_kernel(x_ref, w_ref, o_ref, wbf):
    @pl.when(pl.program_id(1) == 0)
    def _():
        wbf[...] = w_ref[...].astype(BF16)

    o_ref[...] = _dot(x_ref[...].astype(BF16), wbf[...]).astype(o_ref.dtype)


def _layer_spec(w, layer, block, index_map):
    if layer is None:
        return pl.BlockSpec(block, index_map)
    lead = w.ndim - len(block) - 1
    return pl.BlockSpec((None,) * (lead + 1) + tuple(block),
                        lambda *a: (layer,) + (0,) * lead + tuple(index_map(*a)))


def _mm_rope_kernel(x_ref, w_ref, cos_ref, sin_ref, o_ref, wbf, *, rope_from, half):
    j = pl.program_id(0)

    @pl.when(pl.program_id(1) == 0)
    def _():
        wbf[...] = w_ref[...].astype(BF16)

    acc = _dot(x_ref[...].astype(BF16), wbf[...])

    @pl.when(j < rope_from)
    def _():
        o_ref[...] = acc.astype(o_ref.dtype)

    @pl.when(j >= rope_from)
    def _():
        cos, sin = cos_ref[...], sin_ref[...]
        for c in range(acc.shape[1] // LANE):
            cols = slice(c * LANE, (c + 1) * LANE)
            o_ref[:, cols] = _rope(acc[:, cols], cos, sin, half).astype(o_ref.dtype)


def _mm(x, w, *, out_dtype, layer=None, rope=None, tm=1024, tn=1024):
    M, K = x.shape
    N = w.shape[-1]
    tm = _tile(M, tm)
    tn = _tile(N, tn, LANE)
    in_specs = [
        pl.BlockSpec((tm, K), lambda j, i: (i, 0)),
        _layer_spec(w, layer, (K, tn), lambda j, i: (0, j)),
    ]
    args = [x, w]
    body = _mm_kernel
    if rope is not None:
        cos, sin, first_col, half = rope
        assert first_col % tn == 0
        in_specs += [pl.BlockSpec((tm, LANE), lambda j, i: (i, 0))] * 2
        args += [cos, sin]
        body = functools.partial(_mm_rope_kernel, rope_from=first_col // tn, half=half)
    return pl.pallas_call(
        body,
        out_shape=SDS((M, N), out_dtype),
        grid=(N // tn, M // tm),
        in_specs=in_specs,
        out_specs=pl.BlockSpec((tm, tn), lambda j, i: (i, j)),
        scratch_shapes=[pltpu.VMEM((K, tn), BF16)],
        compiler_params=_cparams("parallel", "arbitrary"),
        name="mm",
    )(*args)


def _mm_resid_kernel(xp_ref, xs_ref, w_ref, y_ref, g_ref, gate_ref, o_ref, wbf, *, n_ptiles):
    i = pl.program_id(0)

    @pl.when(i == 0)
    def _():
        wbf[...] = w_ref[...].astype(BF16)

    x = jnp.where(i < n_ptiles, xp_ref[...], xs_ref[...])
    a = _dot(x.astype(BF16), wbf[...])
    o_ref[...] = y_ref[...] + gate_ref[0] * (_rms(a) * g_ref[...])


def _mm_resid(x_p, x_s, w, y, gain, gate, *, layer, n_prompt, t_sample, tm=256):
    M, D = y.shape
    K = w.shape[-2]
    tm = _row_tile(n_prompt, t_sample, tm)
    n_ptiles = n_prompt // tm
    grp = functools.partial(_group_of, tm=tm, n_prompt=n_prompt, t_sample=t_sample)
    return pl.pallas_call(
        functools.partial(_mm_resid_kernel, n_ptiles=n_ptiles),
        out_shape=SDS((M, D), F32),
        grid=(M // tm,),
        in_specs=[
            pl.BlockSpec((tm, K), lambda i: (jnp.minimum(i, n_ptiles - 1), 0)),
            pl.BlockSpec((tm, K), lambda i: (jnp.maximum(i - n_ptiles, 0), 0)),
            pl.BlockSpec((None, K, D), lambda i: (layer, 0, 0), pipeline_mode=pl.Buffered(1)),
            pl.BlockSpec((tm, D), lambda i: (i, 0)),
            pl.BlockSpec((1, D), lambda i: (0, 0)),
            pl.BlockSpec((1, 1, D), lambda i: (grp(i), 0, 0)),
        ],
        out_specs=pl.BlockSpec((tm, D), lambda i: (i, 0)),
        scratch_shapes=[pltpu.VMEM((K, D), BF16)],
        compiler_params=_cparams("arbitrary"),
        name="mm_resid",
    )(x_p, x_s, w, y, gain.reshape(1, D), gate)


def _cast_kernel(x_ref, o_ref):
    o_ref[...] = x_ref[...].astype(o_ref.dtype)


def _cast_bf16(w, layer, *, tr=1024):
    shape = w.shape[1:]
    R = math.prod(shape[:-1])
    N = shape[-1]
    w2 = w.reshape(w.shape[0] * R, N)
    tr = _tile(R, tr)
    nb = R // tr
    out = pl.pallas_call(
        _cast_kernel,
        out_shape=SDS((R, N), BF16),
        grid=(nb,),
        in_specs=[pl.BlockSpec((tr, N), lambda i: (layer * nb + i, 0))],
        out_specs=pl.BlockSpec((tr, N), lambda i: (i, 0)),
        compiler_params=_cparams("parallel"),
        name="cast_bf16",
    )(w2)
    return out.reshape(shape)


def _gate_up_kernel(x_ref, w1_ref, w3_ref, u_ref, wb1, wb3):
    @pl.when(pl.program_id(1) == 0)
    def _():
        wb1[...] = w1_ref[...].astype(BF16)
        wb3[...] = w3_ref[...].astype(BF16)

    x = x_ref[...]
    u_ref[...] = (_silu(_dot(x, wb1[...])) * _dot(x, wb3[...])).astype(u_ref.dtype)


def _gate_up(x, w1, w3, layer, *, tm=1024, tn=512):
    M, D = x.shape
    F = w1.shape[-1]
    tm = _tile(M, tm)
    tn = _tile(F, tn, LANE)
    wspec = _layer_spec(w1, layer, (D, tn), lambda j, i: (0, j))
    return pl.pallas_call(
        _gate_up_kernel,
        out_shape=SDS((M, F), BF16),
        grid=(F // tn, M // tm),
        in_specs=[pl.BlockSpec((tm, D), lambda j, i: (i, 0)), wspec, wspec],
        out_specs=pl.BlockSpec((tm, tn), lambda j, i: (i, j)),
        scratch_shapes=[pltpu.VMEM((D, tn), BF16), pltpu.VMEM((D, tn), BF16)],
        compiler_params=_cparams("parallel", "arbitrary"),
        name="gate_up",
    )(x, w1, w3)


def _down_resid_kernel(u_ref, w_ref, y_ref, g_ref, gate_ref, o_ref, *, nk):
    k = pl.program_id(1)

    @pl.when(k == 0)
    def _():
        o_ref[...] = jnp.zeros_like(o_ref)

    o_ref[...] += _dot(u_ref[...], w_ref[...])

    @pl.when(k == nk - 1)
    def _():
        o_ref[...] = y_ref[...] + gate_ref[0] * (_rms(o_ref[...]) * g_ref[...])


def _down_resid(u, w2b, y, gain, gate, *, n_prompt, t_sample, tm=1024, tk=1408):
    M, D = y.shape
    F = u.shape[1]
    tm = _row_tile(n_prompt, t_sample, tm)
    tk = _tile(F, tk, LANE)
    nk = F // tk
    grp = functools.partial(_group_of, tm=tm, n_prompt=n_prompt, t_sample=t_sample)
    return pl.pallas_call(
        functools.partial(_down_resid_kernel, nk=nk),
        out_shape=SDS((M, D), F32),
        grid=(M // tm, nk),
        in_specs=[
            pl.BlockSpec((tm, tk), lambda i, k: (i, k)),
            pl.BlockSpec((tk, D), lambda i, k: (k, 0)),
            pl.BlockSpec((tm, D), lambda i, k: (i, 0), pipeline_mode=pl.Buffered(1)),
            pl.BlockSpec((1, D), lambda i, k: (0, 0)),
            pl.BlockSpec((1, 1, D), lambda i, k: (grp(i), 0, 0)),
        ],
        out_specs=pl.BlockSpec((tm, D), lambda i, k: (i, 0)),
        compiler_params=_cparams("parallel", "arbitrary"),
        name="down_resid",
    )(u, w2b, y, gain.reshape(1, D), gate)


def _gather_sorted_kernel(tr_ref, src_ref, h_hbm, o_ref, xs, sem, *, tm, S, nt):
    i = pl.program_id(0)

    def start_tile(t, slot):
        def issue(r2, c):
            for pr in range(2):
                r = 2 * r2 + pr
                s0 = pl.multiple_of(src_ref[t * tm + r] * S, S)
                pltpu.make_async_copy(h_hbm.at[pl.ds(s0, S)], xs.at[slot, pl.ds(pl.multiple_of(r * S, S), S)],
                                      sem.at[slot]).start(priority=pr)
            return c

        lax.fori_loop(0, tm // 2, issue, 0)

    def finish_tile(slot):
        pltpu.make_async_copy(h_hbm.at[pl.ds(0, tm * S)], xs.at[slot], sem.at[slot]).wait()
        for s in range(S):
            o_ref[:, s * LANE:(s + 1) * LANE] = _from_slab(xs.at[slot], tm, s, S).astype(o_ref.dtype)

    @pl.when(jnp.logical_and(i == 0, tr_ref[0] > 0))
    def _():
        start_tile(0, 0)

    nxt = jnp.minimum(i + 1, nt - 1)
    for slot in range(2):
        @pl.when(jnp.logical_and(jnp.logical_and(i + 1 < nt, tr_ref[nxt] > 0), i % 2 == slot))
        def _():
            start_tile(i + 1, 1 - slot)

    @pl.when(tr_ref[i] == 0)
    def _():
        o_ref[...] = jnp.zeros_like(o_ref)

    for slot in range(2):
        @pl.when(jnp.logical_and(tr_ref[i] > 0, i % 2 == slot))
        def _():
            finish_tile(slot)


def _gather_sorted(h_slab, tile_rows, src_tok, *, tm, D):
    S = D // LANE
    nt = tile_rows.shape[0]
    grid_spec = pltpu.PrefetchScalarGridSpec(
        num_scalar_prefetch=2,
        grid=(nt,),
        in_specs=[pl.BlockSpec(memory_space=pl.ANY)],
        out_specs=pl.BlockSpec((tm, D), lambda i, tr, sr: (i, 0)),
        scratch_shapes=[pltpu.VMEM((2, tm * S, LANE), F32), pltpu.SemaphoreType.DMA((2,))],
    )
    return pl.pallas_call(
        functools.partial(_gather_sorted_kernel, tm=tm, S=S, nt=nt),
        out_shape=SDS((nt * tm, D), BF16),
        grid_spec=grid_spec,
        compiler_params=_cparams("arbitrary"),
        name="gather_sorted",
    )(tile_rows, src_tok, h_slab)


def _moe_gate_up_kernel(te_ref, tr_ref, tf_ref, x_ref, w1_ref, w3_ref, u_ref, wb1, wb3):
    i = pl.program_id(1)

    @pl.when(tf_ref[i] > 0)
    def _():
        wb1[...] = w1_ref[...].astype(BF16)
        wb3[...] = w3_ref[...].astype(BF16)

    @pl.when(tr_ref[i] > 0)
    def _():
        x = x_ref[...]
        u_ref[...] = (_silu(_dot(x, wb1[...])) * _dot(x, wb3[...])).astype(u_ref.dtype)

    @pl.when(tr_ref[i] == 0)
    def _():
        u_ref[...] = jnp.zeros_like(u_ref)


def _moe_gate_up(x_sorted, tile_e, tile_rows, tile_first, w1, w3, layer, *, tm, tn=512):
    R, D = x_sorted.shape
    F = w1.shape[-1]
    tn = _tile(F, tn, LANE)
    wspec = pl.BlockSpec((None, None, D, tn), lambda j, i, te, tr, tf: (layer, te[i], 0, j))
    grid_spec = pltpu.PrefetchScalarGridSpec(
        num_scalar_prefetch=3,
        grid=(F // tn, R // tm),
        in_specs=[pl.BlockSpec((tm, D), lambda j, i, te, tr, tf: (i, 0)), wspec, wspec],
        out_specs=pl.BlockSpec((tm, tn), lambda j, i, te, tr, tf: (i, j)),
        scratch_shapes=[pltpu.VMEM((D, tn), BF16), pltpu.VMEM((D, tn), BF16)],
    )
    return pl.pallas_call(
        _moe_gate_up_kernel,
        out_shape=SDS((R, F), BF16),
        grid_spec=grid_spec,
        compiler_params=_cparams("parallel", "arbitrary"),
        name="moe_gate_up",
    )(tile_e, tile_rows, tile_first, x_sorted, w1, w3)


def _moe_down_kernel(te_ref, tr_ref, dst_ref, u_ref, w_ref, gate_ref, out_hbm, acc, stage, sem,
                     *, nk, nt, tm, S, chunk):
    i = pl.program_id(0)
    k = pl.program_id(1)
    nrows = tr_ref[i]
    base = i * tm

    def row_copy(slot, r, d0):
        return pltpu.make_async_copy(stage.at[slot, pl.ds(pl.multiple_of(r * S, S), S)],
                                     out_hbm.at[pl.ds(d0, S)], sem.at[slot])

    def drain(slot, n):
        for c in range(tm // chunk):
            @pl.when(n >= (c + 1) * chunk)
            def _():
                pltpu.make_async_copy(stage.at[slot, pl.ds(c * chunk * S, chunk * S)],
                                      out_hbm.at[pl.ds(0, chunk * S)], sem.at[slot]).wait()

        def one(r, c):
            row_copy(slot, r, 0).wait()
            return c

        lax.fori_loop((n // chunk) * chunk, n, one, 0)

    @pl.when(nrows > 0)
    def _():
        @pl.when(k == 0)
        def _():
            acc[...] = jnp.zeros_like(acc)

        acc[...] += _dot(u_ref[...], w_ref[0])

        for slot in range(2):
            @pl.when(jnp.logical_and(k == nk - 1, i % 2 == slot))
            def _():
                _to_slab(stage.at[slot], acc[...] * gate_ref[...])

                def issue_row(r, pr):
                    row_copy(slot, r, pl.multiple_of(dst_ref[base + r] * S, S)).start(priority=pr)

                def issue(r2, c):
                    for pr in range(2):
                        issue_row(2 * r2 + pr, pr)
                    return c

                lax.fori_loop(0, nrows // 2, issue, 0)

                @pl.when(nrows % 2 == 1)
                def _():
                    issue_row(nrows - 1, 0)

                @pl.when(i > 0)
                def _():
                    drain(1 - slot, tr_ref[jnp.maximum(i - 1, 0)])

                @pl.when(jnp.logical_or(i == nt - 1, tr_ref[jnp.minimum(i + 1, nt - 1)] == 0))
                def _():
                    drain(slot, nrows)


def _moe_down(u_sorted, tile_e, tile_rows, dst_row, gate_rows, w2b, *, n_tok, tm, tk=1408):
    R, F = u_sorted.shape
    D = w2b.shape[2]
    S = D // LANE
    tk = _tile(F, tk, LANE)
    nk = F // tk

    def kidx(i, k, tr):
        return jnp.where(tr[i] > 0, k, nk - 1)

    grid_spec = pltpu.PrefetchScalarGridSpec(
        num_scalar_prefetch=3,
        grid=(R // tm, nk),
        in_specs=[
            pl.BlockSpec((tm, tk), lambda i, k, te, tr, ds: (i, kidx(i, k, tr))),
            pl.BlockSpec((1, tk, D), lambda i, k, te, tr, ds: (te[i], kidx(i, k, tr), 0)),
            pl.BlockSpec((tm, 1), lambda i, k, te, tr, ds: (i, 0)),
        ],
        out_specs=pl.BlockSpec(memory_space=pl.ANY),
        scratch_shapes=[pltpu.VMEM((tm, D), F32), pltpu.VMEM((2, tm * S, LANE), F32),
                        pltpu.SemaphoreType.DMA((2,))],
    )
    return pl.pallas_call(
        functools.partial(_moe_down_kernel, nk=nk, nt=R // tm, tm=tm, S=S, chunk=_tile(tm, 64)),
        out_shape=SDS((2 * n_tok * S, LANE), F32),
        grid_spec=grid_spec,
        compiler_params=pltpu.CompilerParams(dimension_semantics=("arbitrary", "arbitrary"),
                                             vmem_limit_bytes=VMEM_LIMIT_BYTES, has_side_effects=True),
        name="moe_down",
    )(tile_e, tile_rows, dst_row, u_sorted, w2b, gate_rows)


def _combine_kernel(a_ref, b_ref, y_ref, g_ref, gate_ref, o_ref, *, tm, S):
    parts = [_from_slab(a_ref, tm, s, S) + _from_slab(b_ref, tm, s, S) for s in range(S)]
    ss = parts[0] * parts[0]
    for t in parts[1:]:
        ss = ss + t * t
    r = lax.rsqrt(jnp.sum(ss, axis=-1, keepdims=True) / (S * LANE) + EPS)
    for s, t in enumerate(parts):
        c = slice(s * LANE, (s + 1) * LANE)
        o_ref[:, c] = y_ref[:, c] + gate_ref[0][:, c] * (t * r * g_ref[:, c])


def _moe_combine(slots, y, gain, gate, *, n_prompt, t_sample, tm=256):
    M, D = y.shape
    S = D // LANE
    tm = _row_tile(n_prompt, t_sample, tm)
    nb = M // tm
    grp = functools.partial(_group_of, tm=tm, n_prompt=n_prompt, t_sample=t_sample)
    return pl.pallas_call(
        functools.partial(_combine_kernel, tm=tm, S=S),
        out_shape=SDS((M, D), F32),
        grid=(nb,),
        in_specs=[
            pl.BlockSpec((tm * S, LANE), lambda i: (i, 0)),
            pl.BlockSpec((tm * S, LANE), lambda i: (nb + i, 0)),
            pl.BlockSpec((tm, D), lambda i: (i, 0)),
            pl.BlockSpec((1, D), lambda i: (0, 0)),
            pl.BlockSpec((1, 1, D), lambda i: (grp(i), 0, 0)),
        ],
        out_specs=pl.BlockSpec((tm, D), lambda i: (i, 0)),
        compiler_params=_cparams("parallel"),
        name="moe_combine",
    )(slots, slots, y, gain.reshape(1, D), gate)


def _dispatch_tables(sel, gw, tm):
    M = sel.shape[0]
    E = N_EXPERTS
    i32 = jnp.int32
    e_flat = sel.reshape(-1)
    pair = jnp.arange(2 * M, dtype=i32)
    onehot = (e_flat[:, None] == jnp.arange(E, dtype=i32)[None, :]).astype(i32)
    running = jnp.cumsum(onehot, axis=0)
    rank = jnp.sum(running * onehot, axis=1) - 1
    counts = running[-1]
    padded = ((counts + tm - 1) // tm) * tm
    ends_p = jnp.cumsum(padded)
    starts_p = ends_p - padded
    pos = starts_p[e_flat] + rank
    R = 2 * M + E * tm
    payload = jnp.stack([pair // 2, (pair % 2) * M + pair // 2,
                         lax.bitcast_convert_type(gw.reshape(-1), i32)], axis=1)
    table = jnp.zeros((R, 3), i32).at[pos].set(payload)
    src_tok, dst_row = table[:, 0], table[:, 1]
    gate_rows = lax.bitcast_convert_type(table[:, 2], F32)
    tile_start = jnp.arange(R // tm, dtype=i32) * tm
    tile_e = jnp.minimum(jnp.sum(ends_p[None, :] <= tile_start[:, None], axis=1), E - 1).astype(i32)
    tile_rows = jnp.clip(starts_p[tile_e] + counts[tile_e] - tile_start, 0, tm).astype(i32)
    tile_first = jnp.concatenate([jnp.ones((1,), i32), (tile_e[1:] != tile_e[:-1]).astype(i32)])
    return tile_e, tile_rows, tile_first, src_tok, dst_row, gate_rows.reshape(R, 1)


def _attn_kernel(*refs, n_parts, k_modes, v_mode, hb, scale, nk):
    q_refs = refs[:n_parts]
    k_refs = refs[n_parts:2 * n_parts]
    v_ref = refs[2 * n_parts]
    o_ref = refs[2 * n_parts + 1]
    m_sc, l_sc, acc_sc = refs[2 * n_parts + 2:]
    j = pl.program_id(3)
    c_exp = scale * math.log2(math.e)

    @pl.when(j == 0)
    def _():
        m_sc[...] = jnp.full_like(m_sc, -jnp.inf)
        l_sc[...] = jnp.zeros_like(l_sc)
        acc_sc[...] = jnp.zeros_like(acc_sc)

    def block(ref, mode, hh):
        x = ref[:, hh * LANE:(hh + 1) * LANE] if mode == "head" else ref[...]
        return x.astype(BF16)

    for hh in range(hb):
        q = [block(q_refs[p], "head", hh) for p in range(n_parts)]
        k = [block(k_refs[p], k_modes[p], hh) for p in range(n_parts)]
        q = q[0] if n_parts == 1 else jnp.concatenate(q, axis=1)
        k = k[0] if n_parts == 1 else jnp.concatenate(k, axis=1)
        s = _dot_nt(q, k)
        tiles = [s[:, c * LANE:(c + 1) * LANE] for c in range(s.shape[1] // LANE)]
        m_lane = tiles[0]
        for t in tiles[1:]:
            m_lane = jnp.maximum(m_lane, t)
        m_prev = m_sc[hh]
        m_new = jnp.maximum(m_prev, jnp.max(m_lane, axis=-1, keepdims=True))
        alpha = jnp.exp2((m_prev - m_new) * c_exp)
        p = [jnp.exp2((t - m_new) * c_exp) for t in tiles]
        l_lane = p[0]
        for t in p[1:]:
            l_lane = l_lane + t
        l_sc[hh] = alpha * l_sc[hh] + l_lane
        pb = jnp.concatenate([t.astype(BF16) for t in p], axis=1)
        acc_sc[hh] = alpha * acc_sc[hh] + _dot(pb, block(v_ref, v_mode, hh))
        m_sc[hh] = m_new

    @pl.when(j == nk - 1)
    def _():
        for hh in range(hb):
            l = jnp.sum(l_sc[hh], axis=-1, keepdims=True)
            o_ref[:, hh * LANE:(hh + 1) * LANE] = (acc_sc[hh] / l).astype(o_ref.dtype)


def _attention(q_parts, k_parts, v_part, *, B, T, S, H, hb, q_row0, k_row0, tq, tk, scale):
    tq = _tile(T, tq)
    tk = _tile(S, tk)
    nq, nk = T // tq, S // tk
    qb0, kb0 = q_row0 // tq, k_row0 // tk

    def q_spec(off):
        return pl.BlockSpec((tq, hb * LANE), lambda b, g, i, j: (qb0 + b * nq + i, off // hb + g))

    def k_spec(off, mode):
        if mode == "head":
            return pl.BlockSpec((tk, hb * LANE), lambda b, g, i, j: (kb0 + b * nk + j, off // hb + g))
        return pl.BlockSpec((tk, LANE), lambda b, g, i, j: (kb0 + b * nk + j, off + (g if mode == "group" else 0)))

    in_specs = [q_spec(off) for (_, off) in q_parts]
    in_specs += [k_spec(off, mode) for (_, off, mode) in k_parts]
    in_specs += [k_spec(v_part[1], v_part[2])]
    args = [a for (a, _) in q_parts] + [a for (a, _, _) in k_parts] + [v_part[0]]
    return pl.pallas_call(
        functools.partial(_attn_kernel, n_parts=len(q_parts), k_modes=[m for (_, _, m) in k_parts],
                          v_mode=v_part[2], hb=hb, scale=scale, nk=nk),
        out_shape=SDS((B * T, H * LANE), BF16),
        grid=(B, H // hb, nq, nk),
        in_specs=in_specs,
        out_specs=pl.BlockSpec((tq, hb * LANE), lambda b, g, i, j: (b * nq + i, g)),
        scratch_shapes=[pltpu.VMEM((hb, tq, LANE), F32), pltpu.VMEM((hb, tq, LANE), F32),
                        pltpu.VMEM((hb, tq, LANE), F32)],
        compiler_params=_cparams("parallel", "parallel", "parallel", "arbitrary"),
        name="attention",
    )(*args)


def _rmsnorm_cols_kernel(x_ref, g_ref, o_ref):
    o_ref[...] = (_rms(x_ref[...]) * g_ref[...]).astype(o_ref.dtype)


def _rmsnorm_cols(x, gain, *, col_block, width, out_dtype, tm=1024):
    M = x.shape[0]
    tm = _tile(M, tm)
    return pl.pallas_call(
        _rmsnorm_cols_kernel,
        out_shape=SDS((M, width), out_dtype),
        grid=(M // tm,),
        in_specs=[pl.BlockSpec((tm, width), lambda i: (i, col_block)),
                  pl.BlockSpec((1, width), lambda i: (0, 0))],
        out_specs=pl.BlockSpec((tm, width), lambda i: (i, 0)),
        compiler_params=_cparams("parallel"),
        name="rmsnorm_cols",
    )(x, gain.reshape(1, width))


def _rope(x, cos, sin, half):
    lane = lax.broadcasted_iota(jnp.int32, x.shape, 1)
    first = (lane % (2 * half)) < half
    rot = jnp.where(first, -pltpu.roll(x, LANE - half, 1), pltpu.roll(x, half, 1))
    return x * cos + rot * sin


def _head_rope_kernel(x_ref, g_ref, cos_ref, sin_ref, o_ref, *, norm, half, hw):
    cos, sin = cos_ref[...], sin_ref[...]
    for c in range(hw):
        cols = slice(c * LANE, (c + 1) * LANE)
        x = x_ref[:, cols]
        if norm:
            x = _rms(x) * g_ref[c]
        o_ref[:, cols] = _rope(x, cos, sin, half).astype(o_ref.dtype)


def _head_rope(x, gains, cos, sin, *, col_block0, n_heads, norm, half, out_dtype, n_rows=None, tm=1024):
    M = x.shape[0] if n_rows is None else n_rows
    tm = _tile(M, tm)
    hw = math.gcd(math.gcd(n_heads, col_block0) if col_block0 else n_heads, 4)
    return pl.pallas_call(
        functools.partial(_head_rope_kernel, norm=norm, half=half, hw=hw),
        out_shape=SDS((M, n_heads * LANE), out_dtype),
        grid=(M // tm, n_heads // hw),
        in_specs=[
            pl.BlockSpec((tm, hw * LANE), lambda i, h: (i, col_block0 // hw + h)),
            pl.BlockSpec((hw, 1, LANE), lambda i, h: (h, 0, 0)),
            pl.BlockSpec((tm, LANE), lambda i, h: (i, 0)),
            pl.BlockSpec((tm, LANE), lambda i, h: (i, 0)),
        ],
        out_specs=pl.BlockSpec((tm, hw * LANE), lambda i, h: (i, h)),
        compiler_params=_cparams("parallel", "parallel"),
        name="head_rope",
    )(x, gains, cos, sin)


def _rope_tables(n_prompt, dec_batch, t_sample, half):
    tok = jnp.arange(t_sample, dtype=jnp.int32)
    row, col = tok // GRID_W, tok % GRID_W
    inv = ROPE_THETA ** (-jnp.arange(half, dtype=F32) / half)

    def cs(pos):
        ang = pos.astype(F32)[:, None] * inv[None, :]
        return (jnp.concatenate([jnp.cos(ang), jnp.cos(ang)], -1),
                jnp.concatenate([jnp.sin(ang), jnp.sin(ang)], -1))

    cr, sr = cs(row)
    cc, sc = cs(col)
    pad = LANE - 4 * half
    cos = jnp.concatenate([cr, cc, jnp.ones((t_sample, pad), F32)], -1)
    sin = jnp.concatenate([sr, sc, jnp.zeros((t_sample, pad), F32)], -1)
    cos = jnp.concatenate([jnp.ones((n_prompt, LANE), F32), jnp.tile(cos, (dec_batch, 1))], 0)
    sin = jnp.concatenate([jnp.zeros((n_prompt, LANE), F32), jnp.tile(sin, (dec_batch, 1))], 0)
    return cos, sin


def _hgrn_constants(C, reverse):
    L = int(np.log2(C))
    t = np.arange(C)[:, None]
    u = np.arange(C)[None, :]
    Ws, Ms = [], []
    for lvl in range(L):
        b = 1 << lvl
        grp = t // (2 * b)
        if not reverse:
            bnd = grp * 2 * b + b - 1
            qside = (t % (2 * b)) >= b
            W = np.where(qside, (u > bnd) & (u <= t), (u > t) & (u <= bnd))
            kside_s = (u % (2 * b)) < b
        else:
            bnd = grp * 2 * b + b
            qside = (t % (2 * b)) < b
            W = np.where(qside, (u >= t) & (u < bnd), (u >= bnd) & (u < t))
            kside_s = (u % (2 * b)) >= b
        Ws.append(W)
        Ms.append(qside & kside_s & (grp == u // (2 * b)))
    if not reverse:
        Ws += [u <= t, u > t]
    else:
        Ws += [u >= t, u < t]
    Ms.append(t == u)
    W_all = jnp.asarray(np.concatenate(Ws, 0).astype(np.float32), BF16)
    M_all = jnp.asarray(np.stack(Ms).astype(np.float32), F32)
    return W_all, M_all


def _hgrn_chain(q, v, z, lb, W_all, m_ref, st, *, C, L, edge_row):
    f = lb + (1.0 - lb) * jax.nn.sigmoid(z)
    k = 1.0 - f
    lf = jnp.log(f)
    hi = lf.astype(BF16)
    lo = (lf - hi.astype(F32)).astype(BF16)
    d2 = _dot(W_all, jnp.concatenate([hi, lo], axis=1))
    dn = d2[:, :LANE] + d2[:, LANE:]
    e_all = jnp.exp(dn)
    qb = q.astype(BF16)
    a = m_ref[L] * _dot_nt(qb, k.astype(BF16))
    for lvl in range(L):
        e = e_all[lvl * C:(lvl + 1) * C]
        a = a + m_ref[lvl] * _dot_nt((q * e).astype(BF16), (k * e).astype(BF16))
    eq = e_all[L * C:(L + 1) * C]
    ek = e_all[(L + 1) * C:(L + 2) * C]
    vb = v.astype(BF16)
    o = _dot(a.astype(BF16), vb) + _dot_nt((q * eq).astype(BF16), st.astype(BF16))
    g_edge = dn[L * C + edge_row:L * C + edge_row + 1]
    st_new = st * jnp.exp(g_edge) + _dot_tn(vb, (k * ek).astype(BF16))
    return o, st_new


def _hgrn_kernel(*refs, C, L, nc, hb, has_init):
    (qf_ref, vf_ref, zf_ref, qb_ref, vb_ref, zb_ref, lbf_ref, lbb_ref,
     wf_ref, mf_ref, wb_ref, mb_ref) = refs[:12]
    rest = refs[12:]
    if has_init:
        s0_ref, rest = rest[0], rest[1:]
    of_ref, ob_ref, sfin_ref, stf, stb = rest
    c = pl.program_id(2)

    @pl.when(c == 0)
    def _():
        for hh in range(hb):
            if has_init:
                stf[hh] = s0_ref[0, 0, hh]
                stb[hh] = s0_ref[0, 1, hh]
            else:
                stf[hh] = jnp.zeros((LANE, LANE), F32)
                stb[hh] = jnp.zeros((LANE, LANE), F32)

    for hh in range(hb):
        cols = slice(hh * LANE, (hh + 1) * LANE)
        o, s_new = _hgrn_chain(qf_ref[:, cols], vf_ref[:, cols], zf_ref[:, cols], lbf_ref[0][:, cols],
                               wf_ref[...], mf_ref, stf[hh], C=C, L=L, edge_row=C - 1)
        of_ref[:, cols] = o
        stf[hh] = s_new
        o, s_new = _hgrn_chain(qb_ref[:, cols], vb_ref[:, cols], zb_ref[:, cols], lbb_ref[0][:, cols],
                               wb_ref[...], mb_ref, stb[hh], C=C, L=L, edge_row=0)
        ob_ref[:, cols] = o
        stb[hh] = s_new

    @pl.when(c == nc - 1)
    def _():
        for hh in range(hb):
            sfin_ref[0, 0, hh] = stf[hh].T
            sfin_ref[0, 1, hh] = stb[hh].T


def _hgrn_scan(proj, lb, s0t, *, B, T, row0, C=128, hb=4):
    D = proj.shape[1] // 5
    H = D // LANE
    C = _tile(T, C)
    L = int(np.log2(C))
    assert 1 << L == C
    nc = T // C
    rb0 = row0 // C
    HB = H // hb
    W = hb * LANE
    wf, mf = _hgrn_constants(C, False)
    wb, mb = _hgrn_constants(C, True)

    def fwd(col0):
        return pl.BlockSpec((C, W), lambda b, g, c: (rb0 + b * nc + c, col0 // hb + g))

    def bwd(col0):
        return pl.BlockSpec((C, W), lambda b, g, c: (rb0 + b * nc + nc - 1 - c, col0 // hb + g))

    def const(a):
        return pl.BlockSpec(a.shape, lambda b, g, c: (0,) * a.ndim)

    in_specs = [fwd(0), fwd(H), fwd(3 * H), bwd(0), bwd(H), bwd(4 * H),
                pl.BlockSpec((1, 1, W), lambda b, g, c: (0, 0, g)),
                pl.BlockSpec((1, 1, W), lambda b, g, c: (1, 0, g)),
                const(wf), const(mf), const(wb), const(mb)]
    args = [proj, proj, proj, proj, proj, proj, lb, lb, wf, mf, wb, mb]
    if s0t is not None:
        in_specs.append(pl.BlockSpec((1, 2, hb, LANE, LANE), lambda b, g, c: (b, 0, g, 0, 0)))
        args.append(s0t)
    return pl.pallas_call(
        functools.partial(_hgrn_kernel, C=C, L=L, nc=nc, hb=hb, has_init=s0t is not None),
        out_shape=(SDS((B * T, D), F32), SDS((B * T, D), F32), SDS((B, 2, H, LANE, LANE), F32)),
        grid=(B, HB, nc),
        in_specs=in_specs,
        out_specs=(
            pl.BlockSpec((C, W), lambda b, g, c: (b * nc + c, g)),
            pl.BlockSpec((C, W), lambda b, g, c: (b * nc + nc - 1 - c, g)),
            pl.BlockSpec((1, 2, hb, LANE, LANE), lambda b, g, c: (b, 0, g, 0, 0)),
        ),
        scratch_shapes=[pltpu.VMEM((hb, LANE, LANE), F32), pltpu.VMEM((hb, LANE, LANE), F32)],
        compiler_params=_cparams("parallel", "parallel", "arbitrary"),
        name="hgrn_scan",
    )(*args)


def _hgrn_gate_kernel(of_ref, ob_ref, g_ref, w_ref, o_ref, *, hw):
    for c in range(hw):
        cols = slice(c * LANE, (c + 1) * LANE)
        o = _rms(of_ref[:, cols] + ob_ref[:, cols]) * w_ref[c]
        o_ref[:, cols] = (o * _silu(g_ref[:, cols])).astype(o_ref.dtype)


def _hgrn_gate(o_f, o_b, proj, o_norm, *, row0, tm=1024, hw=4):
    M, D = o_f.shape
    H = D // LANE
    tm = _tile(math.gcd(M, row0) if row0 else M, tm)
    rb0 = row0 // tm
    blk = pl.BlockSpec((tm, hw * LANE), lambda i, h: (i, h))
    return pl.pallas_call(
        functools.partial(_hgrn_gate_kernel, hw=hw),
        out_shape=SDS((M, D), BF16),
        grid=(M // tm, H // hw),
        in_specs=[blk, blk,
                  pl.BlockSpec((tm, hw * LANE), lambda i, h: (rb0 + i, 2 * H // hw + h)),
                  pl.BlockSpec((hw, 1, LANE), lambda i, h: (h, 0, 0))],
        out_specs=blk,
        compiler_params=_cparams("parallel", "parallel"),
        name="hgrn_gate",
    )(o_f, o_b, proj, o_norm.reshape(H, 1, LANE))


def kernel(x_prompt, x_sample, c, cache_mla_ckv, cache_mla_kpe, state_hgrn, cache_gqa_k, cache_gqa_v, c_ctx, ada_w, ada_b, norm_w, mla_wq_a, mla_q_norm, mla_wq_b, mla_wkv_a, mla_kv_norm, mla_wkv_b, mla_wo, hgrn_w_in, hgrn_lb_logits, hgrn_o_norm, hgrn_wo, gqa_w_qkv, gqa_q_norm, gqa_k_norm, gqa_wo, ffn_w1, ffn_w3, ffn_w2, moe_router, moe_w1, moe_w3, moe_w2):
    Bp, Tp, D = x_prompt.shape
    Bs, Ts, _ = x_sample.shape
    P = cache_mla_ckv.shape[2]
    depth = ada_w.shape[0]
    NP, NS = Bp * Tp, Bs * Ts
    M = NP + NS
    G = 1 + Bs
    assert D == HG_HEADS * LANE
    rows = dict(n_prompt=NP, t_sample=Ts)

    y = jnp.concatenate([x_prompt.reshape(NP, D), x_sample.reshape(NS, D)], axis=0)
    cond8 = jnp.zeros((8, D), F32).at[0].set(c_ctx).at[1:G].set(c)
    mod = _modulation(cond8, ada_w, ada_b).reshape(depth, 8, 6, D)
    mod = jnp.transpose(mod, (0, 2, 1, 3))[:, :, :G, None, :]

    tm_tok = _row_tile(NP, Ts, 512)
    cos_mla, sin_mla = _rope_tables(NP, Bs, Ts, MLA_ROPE // 4)
    cos_gqa, sin_gqa = _rope_tables(NP, Bs, Ts, GQA_HD // 4)
    ones_g = jnp.ones((MLA_HEADS, 1, LANE), F32)

    ckv_list, kpe_list, hg_list, gk_list, gv_list = [], [], [], [], []
    for layer in range(depth):
        kind, j = layer % 3, layer // 3
        nw = norm_w[layer]
        sh1, sc1, g1, sh2, sc2, g2 = (mod[layer, k] for k in range(6))
        h = _adaln_in(y, nw[0], sh1, sc1, tm=tm_tok, **rows)

        if kind == 0:
            Hm = MLA_HEADS
            w_a = jnp.concatenate([mla_wq_a[j], mla_wkv_a[j],
                                   jnp.zeros((D, LANE - MLA_ROPE), F32)], axis=1)
            a = _mm(h, w_a, out_dtype=F32)
            ql, kvl = mla_wq_a.shape[2], mla_kv_norm.shape[1]
            qn = _rmsnorm_cols(a, mla_q_norm[j], col_block=0, width=ql, out_dtype=BF16)
            ckv = _rmsnorm_cols(a, mla_kv_norm[j], col_block=ql // kvl, width=kvl, out_dtype=F32)
            wqb = mla_wq_b[j].reshape(ql, Hm, MLA_NOPE + MLA_ROPE)
            wqb = jnp.concatenate([
                wqb[:, :, :MLA_NOPE].reshape(ql, Hm * MLA_NOPE),
                jnp.pad(wqb[:, :, MLA_NOPE:], ((0, 0), (0, 0), (0, LANE - MLA_ROPE))).reshape(ql, Hm * LANE),
            ], axis=1)
            q = _mm(qn, wqb, out_dtype=BF16, rope=(cos_mla, sin_mla, Hm * MLA_NOPE, MLA_ROPE // 4))
            pe_blk = (ql + kvl) // LANE
            k_pe = _head_rope(a, ones_g, cos_mla, sin_mla, col_block0=pe_blk, n_heads=1, norm=False,
                              half=MLA_ROPE // 4, out_dtype=BF16)
            wkvb = mla_wkv_b[j].reshape(kvl, Hm, MLA_NOPE + MLA_V)
            wkvb = jnp.concatenate([wkvb[:, :, :MLA_NOPE].reshape(kvl, Hm * MLA_NOPE),
                                    wkvb[:, :, MLA_NOPE:].reshape(kvl, Hm * MLA_V)], axis=1)
            ckv_s = jnp.concatenate([ckv[NP:].reshape(Bs, Ts, kvl), cache_mla_ckv[:, j]], axis=1)
            c_all = jnp.concatenate([ckv[:NP], ckv_s.reshape(Bs * (Ts + P), kvl)], axis=0)
            kv = _mm(c_all, wkvb, out_dtype=BF16)
            kpe_ctx = jnp.pad(cache_mla_kpe[:, j], ((0, 0), (0, 0), (0, LANE - MLA_ROPE))).astype(BF16)
            kpe_s = jnp.concatenate([k_pe[NP:].reshape(Bs, Ts, LANE), kpe_ctx], axis=1)
            kpe_all = jnp.concatenate([k_pe[:NP], kpe_s.reshape(Bs * (Ts + P), LANE)], axis=0)
            scale = (MLA_NOPE + MLA_ROPE) ** -0.5
            qp = [(q, 0), (q, Hm)]
            kp = [(kv, 0, "head"), (kpe_all, 0, "shared")]
            vp = (kv, Hm, "head")
            mix_p = _attention(qp, kp, vp, B=Bp, T=Tp, S=Tp, H=Hm, hb=4, q_row0=0, k_row0=0,
                               tq=256, tk=256, scale=scale)
            mix_s = _attention(qp, kp, vp, B=Bs, T=Ts, S=Ts + P, H=Hm, hb=4, q_row0=NP, k_row0=NP,
                               tq=1024, tk=512, scale=scale)
            w_o = mla_wo
            ckv_list.append(ckv[:NP].reshape(Bp, Tp, kvl))
            kpe_list.append(a[:NP, ql + kvl:ql + kvl + MLA_ROPE].reshape(Bp, Tp, MLA_ROPE))
        elif kind == 1:
            H = HG_HEADS
            p = jax.nn.softmax(hgrn_lb_logits.astype(F32), axis=1)
            cum = jnp.cumsum(p, axis=1)
            lb = (cum - cum[:, :1])[:, layer].reshape(2, 1, D)
            proj = _mm(h, hgrn_w_in, layer=j, out_dtype=F32)
            of_p, ob_p, st_p = _hgrn_scan(proj, lb, None, B=Bp, T=Tp, row0=0)
            s0t = jnp.swapaxes(state_hgrn[:, j].astype(F32), -1, -2)
            of_s, ob_s, _ = _hgrn_scan(proj, lb, s0t, B=Bs, T=Ts, row0=NP)
            mix_p = _hgrn_gate(of_p, ob_p, proj, hgrn_o_norm[j], row0=0)
            mix_s = _hgrn_gate(of_s, ob_s, proj, hgrn_o_norm[j], row0=NP)
            w_o = hgrn_wo
            hg_list.append(st_p)
        else:
            Hq, Hk = GQA_HEADS, GQA_KV_HEADS
            qkv = _mm(h, gqa_w_qkv, layer=j, out_dtype=F32)
            gains = jnp.concatenate([jnp.tile(gqa_q_norm[j][None], (Hq, 1)),
                                     jnp.tile(gqa_k_norm[j][None], (Hk, 1))], axis=0).reshape(Hq + Hk, 1, LANE)
            k_plain = _head_rope(qkv, gains[Hq:], cos_gqa, sin_gqa, col_block0=Hq, n_heads=Hk, norm=True,
                                 half=GQA_HD // 4, out_dtype=F32, n_rows=NP)
            qk = _head_rope(qkv, gains, cos_gqa, sin_gqa, col_block0=0, n_heads=Hq + Hk, norm=True,
                            half=GQA_HD // 4, out_dtype=BF16)
            kw = Hk * LANE
            k_new = qk[:, Hq * LANE:]
            v_new = qkv[:, (Hq + Hk) * LANE:].astype(BF16)
            k_s = jnp.concatenate([k_new[NP:].reshape(Bs, Ts, kw),
                                   cache_gqa_k[:, j].reshape(Bs, P, kw).astype(BF16)], axis=1)
            v_s = jnp.concatenate([v_new[NP:].reshape(Bs, Ts, kw),
                                   cache_gqa_v[:, j].reshape(Bs, P, kw).astype(BF16)], axis=1)
            k_all = jnp.concatenate([k_new[:NP], k_s.reshape(Bs * (Ts + P), kw)], axis=0)
            v_all = jnp.concatenate([v_new[:NP], v_s.reshape(Bs * (Ts + P), kw)], axis=0)
            scale = GQA_HD ** -0.5
            qp = [(qk, 0)]
            kp = [(k_all, 0, "group")]
            vp = (v_all, 0, "group")
            mix_p = _attention(qp, kp, vp, B=Bp, T=Tp, S=Tp, H=Hq, hb=Hq // Hk, q_row0=0, k_row0=0,
                               tq=256, tk=256, scale=scale)
            mix_s = _attention(qp, kp, vp, B=Bs, T=Ts, S=Ts + P, H=Hq, hb=Hq // Hk, q_row0=NP, k_row0=NP,
                               tq=1024, tk=512, scale=scale)
            w_o = gqa_wo
            gk_list.append(k_plain[:NP].reshape(Bp, Tp, Hk, GQA_HD))
            gv_list.append(qkv[:NP, (Hq + Hk) * LANE:].reshape(Bp, Tp, Hk, GQA_HD))

        y = _mm_resid(mix_p, mix_s, w_o, y, nw[1], g1, layer=j, **rows)

        fi = layer // 2
        if layer % 2 == 0:
            h2 = _adaln_in(y, nw[2], sh2, sc2, tm=tm_tok, **rows)
            u = _gate_up(h2, ffn_w1, ffn_w3, fi)
            y = _down_resid(u, _cast_bf16(ffn_w2, fi), y, nw[3], g2, **rows)
        else:
            h2, sel, gw = _adaln_in(y, nw[2], sh2, sc2, tm=tm_tok, router=moe_router[fi], **rows)
            tm_e = _tile(M, 1024)
            tile_e, tile_rows, tile_first, src_tok, dst_row, gate_rows = _dispatch_tables(
                sel[:, :2], gw[:, :2], tm_e)
            x_sorted = _gather_sorted(h2, tile_rows, src_tok, tm=tm_e, D=D)
            u = _moe_gate_up(x_sorted, tile_e, tile_rows, tile_first, moe_w1, moe_w3, fi, tm=tm_e)
            slots = _moe_down(u, tile_e, tile_rows, dst_row, gate_rows, _cast_bf16(moe_w2, fi),
                              n_tok=M, tm=tm_e)
            y = _moe_combine(slots, y, nw[3], g2, **rows)

    return (y[:NP].reshape(Bp, Tp, D), y[NP:].reshape(Bs, Ts, D),
            jnp.stack(ckv_list, axis=1), jnp.stack(kpe_list, axis=1), jnp.stack(hg_list, axis=1),
            jnp.stack(gk_list, axis=1), jnp.stack(gv_list, axis=1))
```

```python
import functools
import math

import numpy as np
import jax
import jax.numpy as jnp
from jax import lax
from jax.experimental import pallas as pl
from jax.experimental.pallas import tpu as pltpu

F32 = jnp.float32
BF16 = jnp.bfloat16
EPS = 1e-6
LANE = 128

GRID_W = 64
ROPE_THETA = 10000.0
MLA_HEADS, MLA_NOPE, MLA_ROPE, MLA_V = 16, 128, 64, 128
HG_HEADS = 16
GQA_HEADS, GQA_KV_HEADS, GQA_HD = 16, 4, 128
N_EXPERTS = 8

VMEM_LIMIT_BYTES = 56 * 1024 * 1024

SDS = jax.ShapeDtypeStruct


def _cparams(*sem):
    return pltpu.CompilerParams(dimension_semantics=sem, vmem_limit_bytes=VMEM_LIMIT_BYTES)


def _tile(n, pref, step=8):
    t = min(pref, n)
    while n % t:
        t -= step
    return t


def _dot(a, b):
    return jnp.dot(a, b, preferred_element_type=F32)


def _dot_nt(a, b):
    return lax.dot_general(a, b, (((1,), (1,)), ((), ())), preferred_element_type=F32)


def _dot_tn(a, b):
    return lax.dot_general(a, b, (((0,), (0,)), ((), ())), preferred_element_type=F32)


def _rms(x):
    return x * lax.rsqrt(jnp.mean(x * x, axis=-1, keepdims=True) + EPS)


def _silu(x):
    return x * jax.nn.sigmoid(x)


def _row_tile(n_prompt, t_sample, pref):
    return _tile(math.gcd(n_prompt, t_sample), pref)


def _group_of(i, tm, n_prompt, t_sample):
    r = i * tm
    return jnp.where(r < n_prompt, 0, 1 + (r - n_prompt) // t_sample)


def _mod_kernel(c_ref, w_ref, b_ref, o_ref):
    s = _silu(c_ref[...]).astype(BF16)
    o_ref[0] = _dot(s, w_ref[0].astype(BF16)) + b_ref[0]


def _modulation(cond8, ada_w, ada_b):
    L, D, N6 = ada_w.shape
    tn = _tile(N6, 1024)
    return pl.pallas_call(
        _mod_kernel,
        out_shape=SDS((L, 8, N6), F32),
        grid=(L, N6 // tn),
        in_specs=[
            pl.BlockSpec((8, D), lambda l, j: (0, 0)),
            pl.BlockSpec((1, D, tn), lambda l, j: (l, 0, j)),
            pl.BlockSpec((1, 1, tn), lambda l, j: (l, 0, j)),
        ],
        out_specs=pl.BlockSpec((1, 8, tn), lambda l, j: (l, 0, j)),
        compiler_params=_cparams("parallel", "parallel"),
        name="modulation",
    )(cond8, ada_w, ada_b.reshape(L, 1, N6))


def _adaln_in_kernel(y_ref, g_ref, sh_ref, sc_ref, h_ref):
    h = (_rms(y_ref[...]) * g_ref[...]) * (1.0 + sc_ref[0]) + sh_ref[0]
    h_ref[...] = h.astype(h_ref.dtype)


def _to_slab(ref, x):
    rows, D = x.shape
    S = D // LANE
    for s in range(S):
        ref[pl.ds(s, rows, stride=S), :] = x[:, s * LANE:(s + 1) * LANE]


def _from_slab(ref, rows, s, S):
    return ref[pl.ds(s, rows, stride=S), :]


def _adaln_route_kernel(y_ref, g_ref, sh_ref, sc_ref, r_ref, h_ref, sel_ref, gw_ref):
    h = (_rms(y_ref[...]) * g_ref[...]) * (1.0 + sc_ref[0]) + sh_ref[0]
    _to_slab(h_ref, h)
    logits = lax.dot_general(h, r_ref[...], (((1,), (0,)), ((), ())),
                             precision=lax.Precision.HIGHEST, preferred_element_type=F32)
    lane = lax.broadcasted_iota(jnp.int32, logits.shape, 1)
    neg = jnp.float32(-jnp.inf)
    logits = jnp.where(lane < N_EXPERTS, logits, neg)
    lane_f = lane.astype(F32)
    m1 = jnp.max(logits, axis=-1, keepdims=True)
    i1 = jnp.min(jnp.where(logits == m1, lane_f, float(LANE)), axis=-1, keepdims=True)
    rest = jnp.where(lane_f == i1, neg, logits)
    m2 = jnp.max(rest, axis=-1, keepdims=True)
    i2 = jnp.min(jnp.where(rest == m2, lane_f, float(LANE)), axis=-1, keepdims=True)
    e = jnp.exp(m2 - m1)
    w1 = 1.0 / (1.0 + e)
    w2 = e / (1.0 + e)
    i1, i2 = i1.astype(jnp.int32), i2.astype(jnp.int32)
    sel_ref[...] = jnp.where(lane == 0, i1, jnp.where(lane == 1, i2, 0))
    gw_ref[...] = jnp.where(lane == 0, w1, jnp.where(lane == 1, w2, 0.0))


def _adaln_in(y, gain, shift, scale, *, n_prompt, t_sample, tm, router=None):
    M, D = y.shape
    grp = functools.partial(_group_of, tm=tm, n_prompt=n_prompt, t_sample=t_sample)
    in_specs = [
        pl.BlockSpec((tm, D), lambda i: (i, 0)),
        pl.BlockSpec((1, D), lambda i: (0, 0)),
        pl.BlockSpec((1, 1, D), lambda i: (grp(i), 0, 0)),
        pl.BlockSpec((1, 1, D), lambda i: (grp(i), 0, 0)),
    ]
    args = [y, gain.reshape(1, D), shift, scale]
    if router is None:
        return pl.pallas_call(
            _adaln_in_kernel,
            out_shape=SDS((M, D), BF16),
            grid=(M // tm,),
            in_specs=in_specs,
            out_specs=pl.BlockSpec((tm, D), lambda i: (i, 0)),
            compiler_params=_cparams("parallel"),
            name="adaln_in",
        )(*args)
    r_pad = jnp.pad(router, ((0, 0), (0, LANE - router.shape[1])))
    return pl.pallas_call(
        _adaln_route_kernel,
        out_shape=(SDS((M * (D // LANE), LANE), F32), SDS((M, LANE), jnp.int32), SDS((M, LANE), F32)),
        grid=(M // tm,),
        in_specs=in_specs + [pl.BlockSpec((D, LANE), lambda i: (0, 0))],
        out_specs=(
            pl.BlockSpec((tm * (D // LANE), LANE), lambda i: (i, 0)),
            pl.BlockSpec((tm, LANE), lambda i: (i, 0)),
            pl.BlockSpec((tm, LANE), lambda i: (i, 0)),
        ),
        compiler_params=_cparams("parallel"),
        name="adaln_route",
    )(*args, r_pad)


def _mm_kernel(x_ref, w_ref, o_ref, wbf):
    @pl.when(pl.program_id(1) == 0)
    def _():
        wbf[...] = w_ref[...].astype(BF16)

    o_ref[...] = _dot(x_ref[...].astype(BF16), wbf[...]).astype(o_ref.dtype)


def _layer_spec(w, layer, block, index_map):
    if layer is None:
        return pl.BlockSpec(block, index_map)
    lead = w.ndim - len(block) - 1
    return pl.BlockSpec((None,) * (lead + 1) + tuple(block),
                        lambda *a: (layer,) + (0,) * lead + tuple(index_map(*a)))


def _mm_rope_kernel(x_ref, w_ref, cos_ref, sin_ref, o_ref, wbf, *, rope_from, half):
    j = pl.program_id(0)

    @pl.when(pl.program_id(1) == 0)
    def _():
        wbf[...] = w_ref[...].astype(BF16)

    acc = _dot(x_ref[...].astype(BF16), wbf[...])

    @pl.when(j < rope_from)
    def _():
        o_ref[...] = acc.astype(o_ref.dtype)

    @pl.when(j >= rope_from)
    def _():
        cos, sin = cos_ref[...], sin_ref[...]
        for c in range(acc.shape[1] // LANE):
            cols = slice(c * LANE, (c + 1) * LANE)
            o_ref[:, cols] = _rope(acc[:, cols], cos, sin, half).astype(o_ref.dtype)


def _mm(x, w, *, out_dtype, layer=None, rope=None, tm=1024, tn=1024):
    M, K = x.shape
    N = w.shape[-1]
    tm = _tile(M, tm)
    tn = _tile(N, tn, LANE)
    in_specs = [
        pl.BlockSpec((tm, K), lambda j, i: (i, 0)),
        _layer_spec(w, layer, (K, tn), lambda j, i: (0, j)),
    ]
    args = [x, w]
    body = _mm_kernel
    if rope is not None:
        cos, sin, first_col, half = rope
        assert first_col % tn == 0
        in_specs += [pl.BlockSpec((tm, LANE), lambda j, i: (i, 0))] * 2
        args += [cos, sin]
        body = functools.partial(_mm_rope_kernel, rope_from=first_col // tn, half=half)
    return pl.pallas_call(
        body,
        out_shape=SDS((M, N), out_dtype),
        grid=(N // tn, M // tm),
        in_specs=in_specs,
        out_specs=pl.BlockSpec((tm, tn), lambda j, i: (i, j)),
        scratch_shapes=[pltpu.VMEM((K, tn), BF16)],
        compiler_params=_cparams("parallel", "arbitrary"),
        name="mm",
    )(*args)


def _mm_resid_kernel(xp_ref, xs_ref, w_ref, y_ref, g_ref, gate_ref, o_ref, wbf, *, n_ptiles):
    i = pl.program_id(0)

    @pl.when(i == 0)
    def _():
        wbf[...] = w_ref[...].astype(BF16)

    x = jnp.where(i < n_ptiles, xp_ref[...], xs_ref[...])
    a = _dot(x.astype(BF16), wbf[...])
    o_ref[...] = y_ref[...] + gate_ref[0] * (_rms(a) * g_ref[...])


def _mm_resid(x_p, x_s, w, y, gain, gate, *, layer, n_prompt, t_sample, tm=256):
    M, D = y.shape
    K = w.shape[-2]
    tm = _row_tile(n_prompt, t_sample, tm)
    n_ptiles = n_prompt // tm
    grp = functools.partial(_group_of, tm=tm, n_prompt=n_prompt, t_sample=t_sample)
    return pl.pallas_call(
        functools.partial(_mm_resid_kernel, n_ptiles=n_ptiles),
        out_shape=SDS((M, D), F32),
        grid=(M // tm,),
        in_specs=[
            pl.BlockSpec((tm, K), lambda i: (jnp.minimum(i, n_ptiles - 1), 0)),
            pl.BlockSpec((tm, K), lambda i: (jnp.maximum(i - n_ptiles, 0), 0)),
            pl.BlockSpec((None, K, D), lambda i: (layer, 0, 0), pipeline_mode=pl.Buffered(1)),
            pl.BlockSpec((tm, D), lambda i: (i, 0)),
            pl.BlockSpec((1, D), lambda i: (0, 0)),
            pl.BlockSpec((1, 1, D), lambda i: (grp(i), 0, 0)),
        ],
        out_specs=pl.BlockSpec((tm, D), lambda i: (i, 0)),
        scratch_shapes=[pltpu.VMEM((K, D), BF16)],
        compiler_params=_cparams("arbitrary"),
        name="mm_resid",
    )(x_p, x_s, w, y, gain.reshape(1, D), gate)


def _cast_kernel(x_ref, o_ref):
    o_ref[...] = x_ref[...].astype(o_ref.dtype)


def _cast_bf16(w, layer, *, tr=1024):
    shape = w.shape[1:]
    R = math.prod(shape[:-1])
    N = shape[-1]
    w2 = w.reshape(w.shape[0] * R, N)
    tr = _tile(R, tr)
    nb = R // tr
    out = pl.pallas_call(
        _cast_kernel,
        out_shape=SDS((R, N), BF16),
        grid=(nb,),
        in_specs=[pl.BlockSpec((tr, N), lambda i: (layer * nb + i, 0))],
        out_specs=pl.BlockSpec((tr, N), lambda i: (i, 0)),
        compiler_params=_cparams("parallel"),
        name="cast_bf16",
    )(w2)
    return out.reshape(shape)


def _gate_up_kernel(x_ref, w1_ref, w3_ref, u_ref, wb1, wb3):
    @pl.when(pl.program_id(1) == 0)
    def _():
        wb1[...] = w1_ref[...].astype(BF16)
        wb3[...] = w3_ref[...].astype(BF16)

    x = x_ref[...]
    u_ref[...] = (_silu(_dot(x, wb1[...])) * _dot(x, wb3[...])).astype(u_ref.dtype)


def _gate_up(x, w1, w3, layer, *, tm=1024, tn=512):
    M, D = x.shape
    F = w1.shape[-1]
    tm = _tile(M, tm)
    tn = _tile(F, tn, LANE)
    wspec = _layer_spec(w1, layer, (D, tn), lambda j, i: (0, j))
    return pl.pallas_call(
        _gate_up_kernel,
        out_shape=SDS((M, F), BF16),
        grid=(F // tn, M // tm),
        in_specs=[pl.BlockSpec((tm, D), lambda j, i: (i, 0)), wspec, wspec],
        out_specs=pl.BlockSpec((tm, tn), lambda j, i: (i, j)),
        scratch_shapes=[pltpu.VMEM((D, tn), BF16), pltpu.VMEM((D, tn), BF16)],
        compiler_params=_cparams("parallel", "arbitrary"),
        name="gate_up",
    )(x, w1, w3)


def _down_resid_kernel(u_ref, w_ref, y_ref, g_ref, gate_ref, o_ref, *, nk):
    k = pl.program_id(1)

    @pl.when(k == 0)
    def _():
        o_ref[...] = jnp.zeros_like(o_ref)

    o_ref[...] += _dot(u_ref[...], w_ref[...])

    @pl.when(k == nk - 1)
    def _():
        o_ref[...] = y_ref[...] + gate_ref[0] * (_rms(o_ref[...]) * g_ref[...])


def _down_resid(u, w2b, y, gain, gate, *, n_prompt, t_sample, tm=1024, tk=1408):
    M, D = y.shape
    F = u.shape[1]
    tm = _row_tile(n_prompt, t_sample, tm)
    tk = _tile(F, tk, LANE)
    nk = F // tk
    grp = functools.partial(_group_of, tm=tm, n_prompt=n_prompt, t_sample=t_sample)
    return pl.pallas_call(
        functools.partial(_down_resid_kernel, nk=nk),
        out_shape=SDS((M, D), F32),
        grid=(M // tm, nk),
        in_specs=[
            pl.BlockSpec((tm, tk), lambda i, k: (i, k)),
            pl.BlockSpec((tk, D), lambda i, k: (k, 0)),
            pl.BlockSpec((tm, D), lambda i, k: (i, 0), pipeline_mode=pl.Buffered(1)),
            pl.BlockSpec((1, D), lambda i, k: (0, 0)),
            pl.BlockSpec((1, 1, D), lambda i, k: (grp(i), 0, 0)),
        ],
        out_specs=pl.BlockSpec((tm, D), lambda i, k: (i, 0)),
        compiler_params=_cparams("parallel", "arbitrary"),
        name="down_resid",
    )(u, w2b, y, gain.reshape(1, D), gate)


def _gather_sorted_kernel(tr_ref, src_ref, h_hbm, o_ref, xs, sem, *, tm, S, nt):
    i = pl.program_id(0)

    def start_tile(t, slot):
        def issue(r2, c):
            for pr in range(2):
                r = 2 * r2 + pr
                s0 = pl.multiple_of(src_ref[t * tm + r] * S, S)
                pltpu.make_async_copy(h_hbm.at[pl.ds(s0, S)], xs.at[slot, pl.ds(pl.multiple_of(r * S, S), S)],
                                      sem.at[slot]).start(priority=pr)
            return c

        lax.fori_loop(0, tm // 2, issue, 0)

    def finish_tile(slot):
        pltpu.make_async_copy(h_hbm.at[pl.ds(0, tm * S)], xs.at[slot], sem.at[slot]).wait()
        for s in range(S):
            o_ref[:, s * LANE:(s + 1) * LANE] = _from_slab(xs.at[slot], tm, s, S).astype(o_ref.dtype)

    @pl.when(jnp.logical_and(i == 0, tr_ref[0] > 0))
    def _():
        start_tile(0, 0)

    nxt = jnp.minimum(i + 1, nt - 1)
    for slot in range(2):
        @pl.when(jnp.logical_and(jnp.logical_and(i + 1 < nt, tr_ref[nxt] > 0), i % 2 == slot))
        def _():
            start_tile(i + 1, 1 - slot)

    @pl.when(tr_ref[i] == 0)
    def _():
        o_ref[...] = jnp.zeros_like(o_ref)

    for slot in range(2):
        @pl.when(jnp.logical_and(tr_ref[i] > 0, i % 2 == slot))
        def _():
            finish_tile(slot)


def _gather_sorted(h_slab, tile_rows, src_tok, *, tm, D):
    S = D // LANE
    nt = tile_rows.shape[0]
    grid_spec = pltpu.PrefetchScalarGridSpec(
        num_scalar_prefetch=2,
        grid=(nt,),
        in_specs=[pl.BlockSpec(memory_space=pl.ANY)],
        out_specs=pl.BlockSpec((tm, D), lambda i, tr, sr: (i, 0)),
        scratch_shapes=[pltpu.VMEM((2, tm * S, LANE), F32), pltpu.SemaphoreType.DMA((2,))],
    )
    return pl.pallas_call(
        functools.partial(_gather_sorted_kernel, tm=tm, S=S, nt=nt),
        out_shape=SDS((nt * tm, D), BF16),
        grid_spec=grid_spec,
        compiler_params=_cparams("arbitrary"),
        name="gather_sorted",
    )(tile_rows, src_tok, h_slab)


def _moe_gate_up_kernel(te_ref, tr_ref, tf_ref, x_ref, w1_ref, w3_ref, u_ref, wb1, wb3):
    i = pl.program_id(1)

    @pl.when(tf_ref[i] > 0)
    def _():
        wb1[...] = w1_ref[...].astype(BF16)
        wb3[...] = w3_ref[...].astype(BF16)

    def gate_up(rows):
        x = x_ref[rows, :]
        u_ref[rows, :] = (_silu(_dot(x, wb1[...])) * _dot(x, wb3[...])).astype(u_ref.dtype)

    nrows = tr_ref[i]
    half = x_ref.shape[0] // 2
    lower, upper = pl.ds(0, half), pl.ds(half, half)

    @pl.when(nrows > half)
    def _():
        gate_up(pl.ds(0, 2 * half))

    @pl.when(jnp.logical_and(nrows > 0, nrows <= half))
    def _():
        gate_up(lower)
        u_ref[upper, :] = jnp.zeros((half, u_ref.shape[1]), u_ref.dtype)

    @pl.when(nrows == 0)
    def _():
        u_ref[...] = jnp.zeros_like(u_ref)


def _moe_gate_up(x_sorted, tile_e, tile_rows, tile_first, w1, w3, layer, *, tm, tn=512):
    R, D = x_sorted.shape
    F = w1.shape[-1]
    tn = _tile(F, tn, LANE)
    wspec = pl.BlockSpec((None, None, D, tn), lambda j, i, te, tr, tf: (layer, te[i], 0, j))
    grid_spec = pltpu.PrefetchScalarGridSpec(
        num_scalar_prefetch=3,
        grid=(F // tn, R // tm),
        in_specs=[pl.BlockSpec((tm, D), lambda j, i, te, tr, tf: (i, 0)), wspec, wspec],
        out_specs=pl.BlockSpec((tm, tn), lambda j, i, te, tr, tf: (i, j)),
        scratch_shapes=[pltpu.VMEM((D, tn), BF16), pltpu.VMEM((D, tn), BF16)],
    )
    return pl.pallas_call(
        _moe_gate_up_kernel,
        out_shape=SDS((R, F), BF16),
        grid_spec=grid_spec,
        compiler_params=_cparams("parallel", "arbitrary"),
        name="moe_gate_up",
    )(tile_e, tile_rows, tile_first, x_sorted, w1, w3)


def _moe_down_kernel(te_ref, tr_ref, dst_ref, u_ref, w_ref, gate_ref, out_hbm, acc, stage, sem,
                     *, nk, nt, tm, S, chunk):
    i = pl.program_id(0)
    k = pl.program_id(1)
    nrows = tr_ref[i]
    base = i * tm

    def row_copy(slot, r, d0):
        return pltpu.make_async_copy(stage.at[slot, pl.ds(pl.multiple_of(r * S, S), S)],
                                     out_hbm.at[pl.ds(d0, S)], sem.at[slot])

    def drain(slot, n):
        for c in range(tm // chunk):
            @pl.when(n >= (c + 1) * chunk)
            def _():
                pltpu.make_async_copy(stage.at[slot, pl.ds(c * chunk * S, chunk * S)],
                                      out_hbm.at[pl.ds(0, chunk * S)], sem.at[slot]).wait()

        def one(r, c):
            row_copy(slot, r, 0).wait()
            return c

        lax.fori_loop((n // chunk) * chunk, n, one, 0)

    @pl.when(nrows > 0)
    def _():
        @pl.when(k == 0)
        def _():
            acc[...] = jnp.zeros_like(acc)

        half = tm // 2

        @pl.when(nrows > half)
        def _():
            acc[...] += _dot(u_ref[...], w_ref[0])

        @pl.when(nrows <= half)
        def _():
            acc[pl.ds(0, half), :] += _dot(u_ref[pl.ds(0, half), :], w_ref[0])

        for slot in range(2):
            @pl.when(jnp.logical_and(k == nk - 1, i % 2 == slot))
            def _():
                _to_slab(stage.at[slot], acc[...] * gate_ref[...])

                def issue_row(r, pr):
                    row_copy(slot, r, pl.multiple_of(dst_ref[base + r] * S, S)).start(priority=pr)

                def issue(r2, c):
                    for pr in range(2):
                        issue_row(2 * r2 + pr, pr)
                    return c

                lax.fori_loop(0, nrows // 2, issue, 0)

                @pl.when(nrows % 2 == 1)
                def _():
                    issue_row(nrows - 1, 0)

                @pl.when(i > 0)
                def _():
                    drain(1 - slot, tr_ref[jnp.maximum(i - 1, 0)])

                @pl.when(jnp.logical_or(i == nt - 1, tr_ref[jnp.minimum(i + 1, nt - 1)] == 0))
                def _():
                    drain(slot, nrows)


def _moe_down(u_sorted, tile_e, tile_rows, dst_row, gate_rows, w2b, *, n_tok, tm, tk=1408):
    R, F = u_sorted.shape
    D = w2b.shape[2]
    S = D // LANE
    tk = _tile(F, tk, LANE)
    nk = F // tk

    def kidx(i, k, tr):
        return jnp.where(tr[i] > 0, k, nk - 1)

    grid_spec = pltpu.PrefetchScalarGridSpec(
        num_scalar_prefetch=3,
        grid=(R // tm, nk),
        in_specs=[
            pl.BlockSpec((tm, tk), lambda i, k, te, tr, ds: (i, kidx(i, k, tr))),
            pl.BlockSpec((1, tk, D), lambda i, k, te, tr, ds: (te[i], kidx(i, k, tr), 0)),
            pl.BlockSpec((tm, 1), lambda i, k, te, tr, ds: (i, 0)),
        ],
        out_specs=pl.BlockSpec(memory_space=pl.ANY),
        scratch_shapes=[pltpu.VMEM((tm, D), F32), pltpu.VMEM((2, tm * S, LANE), F32),
                        pltpu.SemaphoreType.DMA((2,))],
    )
    return pl.pallas_call(
        functools.partial(_moe_down_kernel, nk=nk, nt=R // tm, tm=tm, S=S, chunk=_tile(tm, 64)),
        out_shape=SDS((2 * n_tok * S, LANE), F32),
        grid_spec=grid_spec,
        compiler_params=pltpu.CompilerParams(dimension_semantics=("arbitrary", "arbitrary"),
                                             vmem_limit_bytes=VMEM_LIMIT_BYTES, has_side_effects=True),
        name="moe_down",
    )(tile_e, tile_rows, dst_row, u_sorted, w2b, gate_rows)


def _combine_kernel(a_ref, b_ref, y_ref, g_ref, gate_ref, o_ref, *, tm, S):
    parts = [_from_slab(a_ref, tm, s, S) + _from_slab(b_ref, tm, s, S) for s in range(S)]
    ss = parts[0] * parts[0]
    for t in parts[1:]:
        ss = ss + t * t
    r = lax.rsqrt(jnp.sum(ss, axis=-1, keepdims=True) / (S * LANE) + EPS)
    for s, t in enumerate(parts):
        c = slice(s * LANE, (s + 1) * LANE)
        o_ref[:, c] = y_ref[:, c] + gate_ref[0][:, c] * (t * r * g_ref[:, c])


def _moe_combine(slots, y, gain, gate, *, n_prompt, t_sample, tm=256):
    M, D = y.shape
    S = D // LANE
    tm = _row_tile(n_prompt, t_sample, tm)
    nb = M // tm
    grp = functools.partial(_group_of, tm=tm, n_prompt=n_prompt, t_sample=t_sample)
    return pl.pallas_call(
        functools.partial(_combine_kernel, tm=tm, S=S),
        out_shape=SDS((M, D), F32),
        grid=(nb,),
        in_specs=[
            pl.BlockSpec((tm * S, LANE), lambda i: (i, 0)),
            pl.BlockSpec((tm * S, LANE), lambda i: (nb + i, 0)),
            pl.BlockSpec((tm, D), lambda i: (i, 0)),
            pl.BlockSpec((1, D), lambda i: (0, 0)),
            pl.BlockSpec((1, 1, D), lambda i: (grp(i), 0, 0)),
        ],
        out_specs=pl.BlockSpec((tm, D), lambda i: (i, 0)),
        compiler_params=_cparams("parallel"),
        name="moe_combine",
    )(slots, slots, y, gain.reshape(1, D), gate)


def _dispatch_tables(sel, gw, tm):
    M = sel.shape[0]
    E = N_EXPERTS
    i32 = jnp.int32
    e_flat = sel.reshape(-1)
    pair = jnp.arange(2 * M, dtype=i32)
    onehot = (e_flat[:, None] == jnp.arange(E, dtype=i32)[None, :]).astype(i32)
    running = jnp.cumsum(onehot, axis=0)
    rank = jnp.sum(running * onehot, axis=1) - 1
    counts = running[-1]
    padded = ((counts + tm - 1) // tm) * tm
    ends_p = jnp.cumsum(padded)
    starts_p = ends_p - padded
    pos = starts_p[e_flat] + rank
    R = 2 * M + E * tm
    payload = jnp.stack([pair // 2, (pair % 2) * M + pair // 2,
                         lax.bitcast_convert_type(gw.reshape(-1), i32)], axis=1)
    table = jnp.zeros((R, 3), i32).at[pos].set(payload)
    src_tok, dst_row = table[:, 0], table[:, 1]
    gate_rows = lax.bitcast_convert_type(table[:, 2], F32)
    tile_start = jnp.arange(R // tm, dtype=i32) * tm
    tile_e = jnp.minimum(jnp.sum(ends_p[None, :] <= tile_start[:, None], axis=1), E - 1).astype(i32)
    tile_rows = jnp.clip(starts_p[tile_e] + counts[tile_e] - tile_start, 0, tm).astype(i32)
    tile_first = jnp.concatenate([jnp.ones((1,), i32), (tile_e[1:] != tile_e[:-1]).astype(i32)])
    return tile_e, tile_rows, tile_first, src_tok, dst_row, gate_rows.reshape(R, 1)


def _attn_kernel(*refs, n_parts, k_modes, v_mode, hb, scale, nk):
    q_refs = refs[:n_parts]
    k_refs = refs[n_parts:2 * n_parts]
    v_ref = refs[2 * n_parts]
    o_ref = refs[2 * n_parts + 1]
    m_sc, l_sc, acc_sc = refs[2 * n_parts + 2:]
    j = pl.program_id(3)
    c_exp = scale * math.log2(math.e)

    @pl.when(j == 0)
    def _():
        m_sc[...] = jnp.full_like(m_sc, -jnp.inf)
        l_sc[...] = jnp.zeros_like(l_sc)
        acc_sc[...] = jnp.zeros_like(acc_sc)

    def block(ref, mode, hh):
        x = ref[:, hh * LANE:(hh + 1) * LANE] if mode == "head" else ref[...]
        return x.astype(BF16)

    for hh in range(hb):
        q = [block(q_refs[p], "head", hh) for p in range(n_parts)]
        k = [block(k_refs[p], k_modes[p], hh) for p in range(n_parts)]
        q = q[0] if n_parts == 1 else jnp.concatenate(q, axis=1)
        k = k[0] if n_parts == 1 else jnp.concatenate(k, axis=1)
        s = _dot_nt(q, k)
        tiles = [s[:, c * LANE:(c + 1) * LANE] for c in range(s.shape[1] // LANE)]
        m_lane = tiles[0]
        for t in tiles[1:]:
            m_lane = jnp.maximum(m_lane, t)
        m_prev = m_sc[hh]
        m_new = jnp.maximum(m_prev, jnp.max(m_lane, axis=-1, keepdims=True))
        alpha = jnp.exp2((m_prev - m_new) * c_exp)
        p = [jnp.exp2((t - m_new) * c_exp) for t in tiles]
        l_lane = p[0]
        for t in p[1:]:
            l_lane = l_lane + t
        l_sc[hh] = alpha * l_sc[hh] + l_lane
        pb = jnp.concatenate([t.astype(BF16) for t in p], axis=1)
        acc_sc[hh] = alpha * acc_sc[hh] + _dot(pb, block(v_ref, v_mode, hh))
        m_sc[hh] = m_new

    @pl.when(j == nk - 1)
    def _():
        for hh in range(hb):
            l = jnp.sum(l_sc[hh], axis=-1, keepdims=True)
            o_ref[:, hh * LANE:(hh + 1) * LANE] = (acc_sc[hh] / l).astype(o_ref.dtype)


def _attention(q_parts, k_parts, v_part, *, B, T, S, H, hb, q_row0, k_row0, tq, tk, scale):
    tq = _tile(T, tq)
    tk = _tile(S, tk)
    nq, nk = T // tq, S // tk
    qb0, kb0 = q_row0 // tq, k_row0 // tk

    def q_spec(off):
        return pl.BlockSpec((tq, hb * LANE), lambda b, g, i, j: (qb0 + b * nq + i, off // hb + g))

    def k_spec(off, mode):
        if mode == "head":
            return pl.BlockSpec((tk, hb * LANE), lambda b, g, i, j: (kb0 + b * nk + j, off // hb + g))
        return pl.BlockSpec((tk, LANE), lambda b, g, i, j: (kb0 + b * nk + j, off + (g if mode == "group" else 0)))

    in_specs = [q_spec(off) for (_, off) in q_parts]
    in_specs += [k_spec(off, mode) for (_, off, mode) in k_parts]
    in_specs += [k_spec(v_part[1], v_part[2])]
    args = [a for (a, _) in q_parts] + [a for (a, _, _) in k_parts] + [v_part[0]]
    return pl.pallas_call(
        functools.partial(_attn_kernel, n_parts=len(q_parts), k_modes=[m for (_, _, m) in k_parts],
                          v_mode=v_part[2], hb=hb, scale=scale, nk=nk),
        out_shape=SDS((B * T, H * LANE), BF16),
        grid=(B, H // hb, nq, nk),
        in_specs=in_specs,
        out_specs=pl.BlockSpec((tq, hb * LANE), lambda b, g, i, j: (b * nq + i, g)),
        scratch_shapes=[pltpu.VMEM((hb, tq, LANE), F32), pltpu.VMEM((hb, tq, LANE), F32),
                        pltpu.VMEM((hb, tq, LANE), F32)],
        compiler_params=_cparams("parallel", "parallel", "parallel", "arbitrary"),
        name="attention",
    )(*args)


def _rmsnorm_cols_kernel(x_ref, g_ref, o_ref):
    o_ref[...] = (_rms(x_ref[...]) * g_ref[...]).astype(o_ref.dtype)


def _rmsnorm_cols(x, gain, *, col_block, width, out_dtype, tm=1024):
    M = x.shape[0]
    tm = _tile(M, tm)
    return pl.pallas_call(
        _rmsnorm_cols_kernel,
        out_shape=SDS((M, width), out_dtype),
        grid=(M // tm,),
        in_specs=[pl.BlockSpec((tm, width), lambda i: (i, col_block)),
                  pl.BlockSpec((1, width), lambda i: (0, 0))],
        out_specs=pl.BlockSpec((tm, width), lambda i: (i, 0)),
        compiler_params=_cparams("parallel"),
        name="rmsnorm_cols",
    )(x, gain.reshape(1, width))


def _rope(x, cos, sin, half):
    lane = lax.broadcasted_iota(jnp.int32, x.shape, 1)
    first = (lane % (2 * half)) < half
    rot = jnp.where(first, -pltpu.roll(x, LANE - half, 1), pltpu.roll(x, half, 1))
    return x * cos + rot * sin


def _head_rope_kernel(x_ref, g_ref, cos_ref, sin_ref, o_ref, *, norm, half, hw):
    cos, sin = cos_ref[...], sin_ref[...]
    for c in range(hw):
        cols = slice(c * LANE, (c + 1) * LANE)
        x = x_ref[:, cols]
        if norm:
            x = _rms(x) * g_ref[c]
        o_ref[:, cols] = _rope(x, cos, sin, half).astype(o_ref.dtype)


def _head_rope(x, gains, cos, sin, *, col_block0, n_heads, norm, half, out_dtype, n_rows=None, tm=1024):
    M = x.shape[0] if n_rows is None else n_rows
    tm = _tile(M, tm)
    hw = math.gcd(math.gcd(n_heads, col_block0) if col_block0 else n_heads, 4)
    return pl.pallas_call(
        functools.partial(_head_rope_kernel, norm=norm, half=half, hw=hw),
        out_shape=SDS((M, n_heads * LANE), out_dtype),
        grid=(M // tm, n_heads // hw),
        in_specs=[
            pl.BlockSpec((tm, hw * LANE), lambda i, h: (i, col_block0 // hw + h)),
            pl.BlockSpec((hw, 1, LANE), lambda i, h: (h, 0, 0)),
            pl.BlockSpec((tm, LANE), lambda i, h: (i, 0)),
            pl.BlockSpec((tm, LANE), lambda i, h: (i, 0)),
        ],
        out_specs=pl.BlockSpec((tm, hw * LANE), lambda i, h: (i, h)),
        compiler_params=_cparams("parallel", "parallel"),
        name="head_rope",
    )(x, gains, cos, sin)


def _rope_tables(n_prompt, dec_batch, t_sample, half):
    tok = jnp.arange(t_sample, dtype=jnp.int32)
    row, col = tok // GRID_W, tok % GRID_W
    inv = ROPE_THETA ** (-jnp.arange(half, dtype=F32) / half)

    def cs(pos):
        ang = pos.astype(F32)[:, None] * inv[None, :]
        return (jnp.concatenate([jnp.cos(ang), jnp.cos(ang)], -1),
                jnp.concatenate([jnp.sin(ang), jnp.sin(ang)], -1))

    cr, sr = cs(row)
    cc, sc = cs(col)
    pad = LANE - 4 * half
    cos = jnp.concatenate([cr, cc, jnp.ones((t_sample, pad), F32)], -1)
    sin = jnp.concatenate([sr, sc, jnp.zeros((t_sample, pad), F32)], -1)
    cos = jnp.concatenate([jnp.ones((n_prompt, LANE), F32), jnp.tile(cos, (dec_batch, 1))], 0)
    sin = jnp.concatenate([jnp.zeros((n_prompt, LANE), F32), jnp.tile(sin, (dec_batch, 1))], 0)
    return cos, sin


def _hgrn_constants(C, reverse):
    L = int(np.log2(C))
    t = np.arange(C)[:, None]
    u = np.arange(C)[None, :]
    Ws, Ms = [], []
    for lvl in range(L):
        b = 1 << lvl
        grp = t // (2 * b)
        if not reverse:
            bnd = grp * 2 * b + b - 1
            qside = (t % (2 * b)) >= b
            W = np.where(qside, (u > bnd) & (u <= t), (u > t) & (u <= bnd))
            kside_s = (u % (2 * b)) < b
        else:
            bnd = grp * 2 * b + b
            qside = (t % (2 * b)) < b
            W = np.where(qside, (u >= t) & (u < bnd), (u >= bnd) & (u < t))
            kside_s = (u % (2 * b)) >= b
        Ws.append(W)
        Ms.append(qside & kside_s & (grp == u // (2 * b)))
    if not reverse:
        Ws += [u <= t, u > t]
    else:
        Ws += [u >= t, u < t]
    Ms.append(t == u)
    W_all = jnp.asarray(np.concatenate(Ws, 0).astype(np.float32), BF16)
    M_all = jnp.asarray(np.stack(Ms).astype(np.float32), F32)
    return W_all, M_all


def _hgrn_chain(q, v, z, lb, W_all, m_ref, st, *, C, L, edge_row):
    f = lb + (1.0 - lb) * jax.nn.sigmoid(z)
    k = 1.0 - f
    lf = jnp.log(f)
    hi = lf.astype(BF16)
    lo = (lf - hi.astype(F32)).astype(BF16)
    d2 = _dot(W_all, jnp.concatenate([hi, lo], axis=1))
    dn = d2[:, :LANE] + d2[:, LANE:]
    e_all = jnp.exp(dn)
    qb = q.astype(BF16)
    a = m_ref[L] * _dot_nt(qb, k.astype(BF16))
    for lvl in range(L):
        e = e_all[lvl * C:(lvl + 1) * C]
        a = a + m_ref[lvl] * _dot_nt((q * e).astype(BF16), (k * e).astype(BF16))
    eq = e_all[L * C:(L + 1) * C]
    ek = e_all[(L + 1) * C:(L + 2) * C]
    vb = v.astype(BF16)
    o = _dot(a.astype(BF16), vb) + _dot_nt((q * eq).astype(BF16), st.astype(BF16))
    g_edge = dn[L * C + edge_row:L * C + edge_row + 1]
    st_new = st * jnp.exp(g_edge) + _dot_tn(vb, (k * ek).astype(BF16))
    return o, st_new


def _hgrn_kernel(*refs, C, L, nc, hb, has_init):
    (qf_ref, vf_ref, zf_ref, qb_ref, vb_ref, zb_ref, lbf_ref, lbb_ref,
     wf_ref, mf_ref, wb_ref, mb_ref) = refs[:12]
    rest = refs[12:]
    if has_init:
        s0_ref, rest = rest[0], rest[1:]
    of_ref, ob_ref, sfin_ref, stf, stb = rest
    c = pl.program_id(2)

    @pl.when(c == 0)
    def _():
        for hh in range(hb):
            if has_init:
                stf[hh] = s0_ref[0, 0, hh]
                stb[hh] = s0_ref[0, 1, hh]
            else:
                stf[hh] = jnp.zeros((LANE, LANE), F32)
                stb[hh] = jnp.zeros((LANE, LANE), F32)

    for hh in range(hb):
        cols = slice(hh * LANE, (hh + 1) * LANE)
        o, s_new = _hgrn_chain(qf_ref[:, cols], vf_ref[:, cols], zf_ref[:, cols], lbf_ref[0][:, cols],
                               wf_ref[...], mf_ref, stf[hh], C=C, L=L, edge_row=C - 1)
        of_ref[:, cols] = o
        stf[hh] = s_new
        o, s_new = _hgrn_chain(qb_ref[:, cols], vb_ref[:, cols], zb_ref[:, cols], lbb_ref[0][:, cols],
                               wb_ref[...], mb_ref, stb[hh], C=C, L=L, edge_row=0)
        ob_ref[:, cols] = o
        stb[hh] = s_new

    @pl.when(c == nc - 1)
    def _():
        for hh in range(hb):
            sfin_ref[0, 0, hh] = stf[hh].T
            sfin_ref[0, 1, hh] = stb[hh].T


def _hgrn_scan(proj, lb, s0t, *, B, T, row0, C=128, hb=4):
    D = proj.shape[1] // 5
    H = D // LANE
    C = _tile(T, C)
    L = int(np.log2(C))
    assert 1 << L == C
    nc = T // C
    rb0 = row0 // C
    HB = H // hb
    W = hb * LANE
    wf, mf = _hgrn_constants(C, False)
    wb, mb = _hgrn_constants(C, True)

    def fwd(col0):
        return pl.BlockSpec((C, W), lambda b, g, c: (rb0 + b * nc + c, col0 // hb + g))

    def bwd(col0):
        return pl.BlockSpec((C, W), lambda b, g, c: (rb0 + b * nc + nc - 1 - c, col0 // hb + g))

    def const(a):
        return pl.BlockSpec(a.shape, lambda b, g, c: (0,) * a.ndim)

    in_specs = [fwd(0), fwd(H), fwd(3 * H), bwd(0), bwd(H), bwd(4 * H),
                pl.BlockSpec((1, 1, W), lambda b, g, c: (0, 0, g)),
                pl.BlockSpec((1, 1, W), lambda b, g, c: (1, 0, g)),
                const(wf), const(mf), const(wb), const(mb)]
    args = [proj, proj, proj, proj, proj, proj, lb, lb, wf, mf, wb, mb]
    if s0t is not None:
        in_specs.append(pl.BlockSpec((1, 2, hb, LANE, LANE), lambda b, g, c: (b, 0, g, 0, 0)))
        args.append(s0t)
    return pl.pallas_call(
        functools.partial(_hgrn_kernel, C=C, L=L, nc=nc, hb=hb, has_init=s0t is not None),
        out_shape=(SDS((B * T, D), F32), SDS((B * T, D), F32), SDS((B, 2, H, LANE, LANE), F32)),
        grid=(B, HB, nc),
        in_specs=in_specs,
        out_specs=(
            pl.BlockSpec((C, W), lambda b, g, c: (b * nc + c, g)),
            pl.BlockSpec((C, W), lambda b, g, c: (b * nc + nc - 1 - c, g)),
            pl.BlockSpec((1, 2, hb, LANE, LANE), lambda b, g, c: (b, 0, g, 0, 0)),
        ),
        scratch_shapes=[pltpu.VMEM((hb, LANE, LANE), F32), pltpu.VMEM((hb, LANE, LANE), F32)],
        compiler_params=_cparams("parallel", "parallel", "arbitrary"),
        name="hgrn_scan",
    )(*args)


def _hgrn_gate_kernel(of_ref, ob_ref, g_ref, w_ref, o_ref, *, hw):
    for c in range(hw):
        cols = slice(c * LANE, (c + 1) * LANE)
        o = _rms(of_ref[:, cols] + ob_ref[:, cols]) * w_ref[c]
        o_ref[:, cols] = (o * _silu(g_ref[:, cols])).astype(o_ref.dtype)


def _hgrn_gate(o_f, o_b, proj, o_norm, *, row0, tm=1024, hw=4):
    M, D = o_f.shape
    H = D // LANE
    tm = _tile(math.gcd(M, row0) if row0 else M, tm)
    rb0 = row0 // tm
    blk = pl.BlockSpec((tm, hw * LANE), lambda i, h: (i, h))
    return pl.pallas_call(
        functools.partial(_hgrn_gate_kernel, hw=hw),
        out_shape=SDS((M, D), BF16),
        grid=(M // tm, H // hw),
        in_specs=[blk, blk,
                  pl.BlockSpec((tm, hw * LANE), lambda i, h: (rb0 + i, 2 * H // hw + h)),
                  pl.BlockSpec((hw, 1, LANE), lambda i, h: (h, 0, 0))],
        out_specs=blk,
        compiler_params=_cparams("parallel", "parallel"),
        name="hgrn_gate",
    )(o_f, o_b, proj, o_norm.reshape(H, 1, LANE))


def kernel(x_prompt, x_sample, c, cache_mla_ckv, cache_mla_kpe, state_hgrn, cache_gqa_k, cache_gqa_v, c_ctx, ada_w, ada_b, norm_w, mla_wq_a, mla_q_norm, mla_wq_b, mla_wkv_a, mla_kv_norm, mla_wkv_b, mla_wo, hgrn_w_in, hgrn_lb_logits, hgrn_o_norm, hgrn_wo, gqa_w_qkv, gqa_q_norm, gqa_k_norm, gqa_wo, ffn_w1, ffn_w3, ffn_w2, moe_router, moe_w1, moe_w3, moe_w2):
    Bp, Tp, D = x_prompt.shape
    Bs, Ts, _ = x_sample.shape
    P = cache_mla_ckv.shape[2]
    depth = ada_w.shape[0]
    NP, NS = Bp * Tp, Bs * Ts
    M = NP + NS
    G = 1 + Bs
    assert D == HG_HEADS * LANE
    rows = dict(n_prompt=NP, t_sample=Ts)

    y = jnp.concatenate([x_prompt.reshape(NP, D), x_sample.reshape(NS, D)], axis=0)
    cond8 = jnp.zeros((8, D), F32).at[0].set(c_ctx).at[1:G].set(c)
    mod = _modulation(cond8, ada_w, ada_b).reshape(depth, 8, 6, D)
    mod = jnp.transpose(mod, (0, 2, 1, 3))[:, :, :G, None, :]

    tm_tok = _row_tile(NP, Ts, 512)
    cos_mla, sin_mla = _rope_tables(NP, Bs, Ts, MLA_ROPE // 4)
    cos_gqa, sin_gqa = _rope_tables(NP, Bs, Ts, GQA_HD // 4)
    ones_g = jnp.ones((MLA_HEADS, 1, LANE), F32)

    ckv_list, kpe_list, hg_list, gk_list, gv_list = [], [], [], [], []
    for layer in range(depth):
        kind, j = layer % 3, layer // 3
        nw = norm_w[layer]
        sh1, sc1, g1, sh2, sc2, g2 = (mod[layer, k] for k in range(6))
        h = _adaln_in(y, nw[0], sh1, sc1, tm=tm_tok, **rows)

        if kind == 0:
            Hm = MLA_HEADS
            w_a = jnp.concatenate([mla_wq_a[j], mla_wkv_a[j],
                                   jnp.zeros((D, LANE - MLA_ROPE), F32)], axis=1)
            a = _mm(h, w_a, out_dtype=F32)
            ql, kvl = mla_wq_a.shape[2], mla_kv_norm.shape[1]
            qn = _rmsnorm_cols(a, mla_q_norm[j], col_block=0, width=ql, out_dtype=BF16)
            ckv = _rmsnorm_cols(a, mla_kv_norm[j], col_block=ql // kvl, width=kvl, out_dtype=F32)
            wqb = mla_wq_b[j].reshape(ql, Hm, MLA_NOPE + MLA_ROPE)
            wqb = jnp.concatenate([
                wqb[:, :, :MLA_NOPE].reshape(ql, Hm * MLA_NOPE),
                jnp.pad(wqb[:, :, MLA_NOPE:], ((0, 0), (0, 0), (0, LANE - MLA_ROPE))).reshape(ql, Hm * LANE),
            ], axis=1)
            q = _mm(qn, wqb, out_dtype=BF16, rope=(cos_mla, sin_mla, Hm * MLA_NOPE, MLA_ROPE // 4))
            pe_blk = (ql + kvl) // LANE
            k_pe = _head_rope(a, ones_g, cos_mla, sin_mla, col_block0=pe_blk, n_heads=1, norm=False,
                              half=MLA_ROPE // 4, out_dtype=BF16)
            wkvb = mla_wkv_b[j].reshape(kvl, Hm, MLA_NOPE + MLA_V)
            wkvb = jnp.concatenate([wkvb[:, :, :MLA_NOPE].reshape(kvl, Hm * MLA_NOPE),
                                    wkvb[:, :, MLA_NOPE:].reshape(kvl, Hm * MLA_V)], axis=1)
            ckv_s = jnp.concatenate([ckv[NP:].reshape(Bs, Ts, kvl), cache_mla_ckv[:, j]], axis=1)
            c_all = jnp.concatenate([ckv[:NP], ckv_s.reshape(Bs * (Ts + P), kvl)], axis=0)
            kv = _mm(c_all, wkvb, out_dtype=BF16)
            kpe_ctx = jnp.pad(cache_mla_kpe[:, j], ((0, 0), (0, 0), (0, LANE - MLA_ROPE))).astype(BF16)
            kpe_s = jnp.concatenate([k_pe[NP:].reshape(Bs, Ts, LANE), kpe_ctx], axis=1)
            kpe_all = jnp.concatenate([k_pe[:NP], kpe_s.reshape(Bs * (Ts + P), LANE)], axis=0)
            scale = (MLA_NOPE + MLA_ROPE) ** -0.5
            qp = [(q, 0), (q, Hm)]
            kp = [(kv, 0, "head"), (kpe_all, 0, "shared")]
            vp = (kv, Hm, "head")
            mix_p = _attention(qp, kp, vp, B=Bp, T=Tp, S=Tp, H=Hm, hb=4, q_row0=0, k_row0=0,
                               tq=256, tk=256, scale=scale)
            mix_s = _attention(qp, kp, vp, B=Bs, T=Ts, S=Ts + P, H=Hm, hb=4, q_row0=NP, k_row0=NP,
                               tq=2048, tk=512, scale=scale)
            w_o = mla_wo
            ckv_list.append(ckv[:NP].reshape(Bp, Tp, kvl))
            kpe_list.append(a[:NP, ql + kvl:ql + kvl + MLA_ROPE].reshape(Bp, Tp, MLA_ROPE))
        elif kind == 1:
            H = HG_HEADS
            p = jax.nn.softmax(hgrn_lb_logits.astype(F32), axis=1)
            cum = jnp.cumsum(p, axis=1)
            lb = (cum - cum[:, :1])[:, layer].reshape(2, 1, D)
            proj = _mm(h, hgrn_w_in, layer=j, out_dtype=F32)
            of_p, ob_p, st_p = _hgrn_scan(proj, lb, None, B=Bp, T=Tp, row0=0)
            s0t = jnp.swapaxes(state_hgrn[:, j].astype(F32), -1, -2)
            of_s, ob_s, _ = _hgrn_scan(proj, lb, s0t, B=Bs, T=Ts, row0=NP)
            mix_p = _hgrn_gate(of_p, ob_p, proj, hgrn_o_norm[j], row0=0)
            mix_s = _hgrn_gate(of_s, ob_s, proj, hgrn_o_norm[j], row0=NP)
            w_o = hgrn_wo
            hg_list.append(st_p)
        else:
            Hq, Hk = GQA_HEADS, GQA_KV_HEADS
            qkv = _mm(h, gqa_w_qkv, layer=j, out_dtype=F32)
            gains = jnp.concatenate([jnp.tile(gqa_q_norm[j][None], (Hq, 1)),
                                     jnp.tile(gqa_k_norm[j][None], (Hk, 1))], axis=0).reshape(Hq + Hk, 1, LANE)
            k_plain = _head_rope(qkv, gains[Hq:], cos_gqa, sin_gqa, col_block0=Hq, n_heads=Hk, norm=True,
                                 half=GQA_HD // 4, out_dtype=F32, n_rows=NP)
            qk = _head_rope(qkv, gains, cos_gqa, sin_gqa, col_block0=0, n_heads=Hq + Hk, norm=True,
                            half=GQA_HD // 4, out_dtype=BF16)
            kw = Hk * LANE
            k_new = qk[:, Hq * LANE:]
            v_new = qkv[:, (Hq + Hk) * LANE:].astype(BF16)
            k_s = jnp.concatenate([k_new[NP:].reshape(Bs, Ts, kw),
                                   cache_gqa_k[:, j].reshape(Bs, P, kw).astype(BF16)], axis=1)
            v_s = jnp.concatenate([v_new[NP:].reshape(Bs, Ts, kw),
                                   cache_gqa_v[:, j].reshape(Bs, P, kw).astype(BF16)], axis=1)
            k_all = jnp.concatenate([k_new[:NP], k_s.reshape(Bs * (Ts + P), kw)], axis=0)
            v_all = jnp.concatenate([v_new[:NP], v_s.reshape(Bs * (Ts + P), kw)], axis=0)
            scale = GQA_HD ** -0.5
            qp = [(qk, 0)]
            kp = [(k_all, 0, "group")]
            vp = (v_all, 0, "group")
            mix_p = _attention(qp, kp, vp, B=Bp, T=Tp, S=Tp, H=Hq, hb=Hq // Hk, q_row0=0, k_row0=0,
                               tq=256, tk=256, scale=scale)
            mix_s = _attention(qp, kp, vp, B=Bs, T=Ts, S=Ts + P, H=Hq, hb=Hq // Hk, q_row0=NP, k_row0=NP,
                               tq=2048, tk=512, scale=scale)
            w_o = gqa_wo
            gk_list.append(k_plain[:NP].reshape(Bp, Tp, Hk, GQA_HD))
            gv_list.append(qkv[:NP, (Hq + Hk) * LANE:].reshape(Bp, Tp, Hk, GQA_HD))

        y = _mm_resid(mix_p, mix_s, w_o, y, nw[1], g1, layer=j, **rows)

        fi = layer // 2
        if layer % 2 == 0:
            h2 = _adaln_in(y, nw[2], sh2, sc2, tm=tm_tok, **rows)
            u = _gate_up(h2, ffn_w1, ffn_w3, fi)
            y = _down_resid(u, _cast_bf16(ffn_w2, fi), y, nw[3], g2, **rows)
        else:
            h2, sel, gw = _adaln_in(y, nw[2], sh2, sc2, tm=tm_tok, router=moe_router[fi], **rows)
            tm_e = _tile(M, 1024)
            tile_e, tile_rows, tile_first, src_tok, dst_row, gate_rows = _dispatch_tables(
                sel[:, :2], gw[:, :2], tm_e)
            x_sorted = _gather_sorted(h2, tile_rows, src_tok, tm=tm_e, D=D)
            u = _moe_gate_up(x_sorted, tile_e, tile_rows, tile_first, moe_w1, moe_w3, fi, tm=tm_e)
            slots = _moe_down(u, tile_e, tile_rows, dst_row, gate_rows, _cast_bf16(moe_w2, fi),
                              n_tok=M, tm=tm_e)
            y = _moe_combine(slots, y, nw[3], g2, **rows)

    return (y[:NP].reshape(Bp, Tp, D), y[NP:].reshape(Bs, Ts, D),
            jnp.stack(ckv_list, axis=1), jnp.stack(kpe_list, axis=1), jnp.stack(hg_list, axis=1),
            jnp.stack(gk_list, axis=1), jnp.stack(gv_list, axis=1))
```

```python
import functools
import math

import numpy as np
import jax
import jax.numpy as jnp
from jax import lax
from jax.experimental import pallas as pl
from jax.experimental.pallas import tpu as pltpu

F32 = jnp.float32
BF16 = jnp.bfloat16
EPS = 1e-6
LANE = 128

GRID_W = 64
ROPE_THETA = 10000.0
MLA_HEADS, MLA_NOPE, MLA_ROPE, MLA_V = 16, 128, 64, 128
HG_HEADS = 16
GQA_HEADS, GQA_KV_HEADS, GQA_HD = 16, 4, 128
N_EXPERTS = 8

VMEM_LIMIT_BYTES = 56 * 1024 * 1024

SDS = jax.ShapeDtypeStruct


def _cparams(*sem):
    return pltpu.CompilerParams(dimension_semantics=sem, vmem_limit_bytes=VMEM_LIMIT_BYTES)


def _tile(n, pref, step=8):
    t = min(pref, n)
    while n % t:
        t -= step
    return t


def _dot(a, b):
    return jnp.dot(a, b, preferred_element_type=F32)


def _dot_nt(a, b):
    return lax.dot_general(a, b, (((1,), (1,)), ((), ())), preferred_element_type=F32)


def _dot_tn(a, b):
    return lax.dot_general(a, b, (((0,), (0,)), ((), ())), preferred_element_type=F32)


def _rms(x):
    return x * lax.rsqrt(jnp.mean(x * x, axis=-1, keepdims=True) + EPS)


def _silu(x):
    return x * jax.nn.sigmoid(x)


def _row_tile(n_prompt, t_sample, pref):
    return _tile(math.gcd(n_prompt, t_sample), pref)


def _group_of(i, tm, n_prompt, t_sample):
    r = i * tm
    return jnp.where(r < n_prompt, 0, 1 + (r - n_prompt) // t_sample)


def _mod_kernel(c_ref, w_ref, b_ref, o_ref):
    s = _silu(c_ref[...]).astype(BF16)
    o_ref[0] = _dot(s, w_ref[0].astype(BF16)) + b_ref[0]


def _modulation(cond8, ada_w, ada_b):
    L, D, N6 = ada_w.shape
    tn = _tile(N6, 1024)
    return pl.pallas_call(
        _mod_kernel,
        out_shape=SDS((L, 8, N6), F32),
        grid=(L, N6 // tn),
        in_specs=[
            pl.BlockSpec((8, D), lambda l, j: (0, 0)),
            pl.BlockSpec((1, D, tn), lambda l, j: (l, 0, j)),
            pl.BlockSpec((1, 1, tn), lambda l, j: (l, 0, j)),
        ],
        out_specs=pl.BlockSpec((1, 8, tn), lambda l, j: (l, 0, j)),
        compiler_params=_cparams("parallel", "parallel"),
        name="modulation",
    )(cond8, ada_w, ada_b.reshape(L, 1, N6))


def _adaln_in_kernel(y_ref, g_ref, sh_ref, sc_ref, h_ref):
    h = (_rms(y_ref[...]) * g_ref[...]) * (1.0 + sc_ref[0]) + sh_ref[0]
    h_ref[...] = h.astype(h_ref.dtype)


def _to_slab(ref, x):
    rows, D = x.shape
    S = D // LANE
    for s in range(S):
        ref[pl.ds(s, rows, stride=S), :] = x[:, s * LANE:(s + 1) * LANE]


def _from_slab(ref, rows, s, S):
    return ref[pl.ds(s, rows, stride=S), :]


def _adaln_route_kernel(y_ref, g_ref, sh_ref, sc_ref, r_ref, h_ref, sel_ref, gw_ref):
    h = (_rms(y_ref[...]) * g_ref[...]) * (1.0 + sc_ref[0]) + sh_ref[0]
    _to_slab(h_ref, h)
    logits = lax.dot_general(h, r_ref[...], (((1,), (0,)), ((), ())),
                             precision=lax.Precision.HIGHEST, preferred_element_type=F32)
    lane = lax.broadcasted_iota(jnp.int32, logits.shape, 1)
    neg = jnp.float32(-jnp.inf)
    logits = jnp.where(lane < N_EXPERTS, logits, neg)
    lane_f = lane.astype(F32)
    m1 = jnp.max(logits, axis=-1, keepdims=True)
    i1 = jnp.min(jnp.where(logits == m1, lane_f, float(LANE)), axis=-1, keepdims=True)
    rest = jnp.where(lane_f == i1, neg, logits)
    m2 = jnp.max(rest, axis=-1, keepdims=True)
    i2 = jnp.min(jnp.where(rest == m2, lane_f, float(LANE)), axis=-1, keepdims=True)
    e = jnp.exp(m2 - m1)
    w1 = 1.0 / (1.0 + e)
    w2 = e / (1.0 + e)
    i1, i2 = i1.astype(jnp.int32), i2.astype(jnp.int32)
    sel_ref[...] = jnp.where(lane == 0, i1, jnp.where(lane == 1, i2, 0))
    gw_ref[...] = jnp.where(lane == 0, w1, jnp.where(lane == 1, w2, 0.0))


def _adaln_in(y, gain, shift, scale, *, n_prompt, t_sample, tm, router=None):
    M, D = y.shape
    grp = functools.partial(_group_of, tm=tm, n_prompt=n_prompt, t_sample=t_sample)
    in_specs = [
        pl.BlockSpec((tm, D), lambda i: (i, 0)),
        pl.BlockSpec((1, D), lambda i: (0, 0)),
        pl.BlockSpec((1, 1, D), lambda i: (grp(i), 0, 0)),
        pl.BlockSpec((1, 1, D), lambda i: (grp(i), 0, 0)),
    ]
    args = [y, gain.reshape(1, D), shift, scale]
    if router is None:
        return pl.pallas_call(
            _adaln_in_kernel,
            out_shape=SDS((M, D), BF16),
            grid=(M // tm,),
            in_specs=in_specs,
            out_specs=pl.BlockSpec((tm, D), lambda i: (i, 0)),
            compiler_params=_cparams("parallel"),
            name="adaln_in",
        )(*args)
    r_pad = jnp.pad(router, ((0, 0), (0, LANE - router.shape[1])))
    return pl.pallas_call(
        _adaln_route_kernel,
        out_shape=(SDS((M * (D // LANE), LANE), F32), SDS((M, LANE), jnp.int32), SDS((M, LANE), F32)),
        grid=(M // tm,),
        in_specs=in_specs + [pl.BlockSpec((D, LANE), lambda i: (0, 0))],
        out_specs=(
            pl.BlockSpec((tm * (D // LANE), LANE), lambda i: (i, 0)),
            pl.BlockSpec((tm, LANE), lambda i: (i, 0)),
            pl.BlockSpec((tm, LANE), lambda i: (i, 0)),
        ),
        compiler_params=_cparams("parallel"),
        name="adaln_route",
    )(*args, r_pad)


def _mm_kernel(x_ref, w_ref, o_ref, wbf):
    @pl.when(pl.program_id(1) == 0)
    def _():
        wbf[...] = w_ref[...].astype(BF16)

    o_ref[...] = _dot(x_ref[...].astype(BF16), wbf[...]).astype(o_ref.dtype)


def _layer_spec(w, layer, block, index_map):
    if layer is None:
        return pl.BlockSpec(block, index_map)
    lead = w.ndim - len(block) - 1
    return pl.BlockSpec((None,) * (lead + 1) + tuple(block),
                        lambda *a: (layer,) + (0,) * lead + tuple(index_map(*a)))


def _mm_rope_kernel(x_ref, w_ref, cos_ref, sin_ref, o_ref, wbf, *, rope_from, half):
    j = pl.program_id(0)

    @pl.when(pl.program_id(1) == 0)
    def _():
        wbf[...] = w_ref[...].astype(BF16)

    acc = _dot(x_ref[...].astype(BF16), wbf[...])

    @pl.when(j < rope_from)
    def _():
        o_ref[...] = acc.astype(o_ref.dtype)

    @pl.when(j >= rope_from)
    def _():
        cos, sin = cos_ref[...], sin_ref[...]
        for c in range(acc.shape[1] // LANE):
            cols = slice(c * LANE, (c + 1) * LANE)
            o_ref[:, cols] = _rope(acc[:, cols], cos, sin, half).astype(o_ref.dtype)


def _mm(x, w, *, out_dtype, layer=None, rope=None, tm=1024, tn=1024):
    M, K = x.shape
    N = w.shape[-1]
    tm = _tile(M, tm)
    tn = _tile(N, tn, LANE)
    in_specs = [
        pl.BlockSpec((tm, K), lambda j, i: (i, 0)),
        _layer_spec(w, layer, (K, tn), lambda j, i: (0, j)),
    ]
    args = [x, w]
    body = _mm_kernel
    if rope is not None:
        cos, sin, first_col, half = rope
        assert first_col % tn == 0
        in_specs += [pl.BlockSpec((tm, LANE), lambda j, i: (i, 0))] * 2
        args += [cos, sin]
        body = functools.partial(_mm_rope_kernel, rope_from=first_col // tn, half=half)
    return pl.pallas_call(
        body,
        out_shape=SDS((M, N), out_dtype),
        grid=(N // tn, M // tm),
        in_specs=in_specs,
        out_specs=pl.BlockSpec((tm, tn), lambda j, i: (i, j)),
        scratch_shapes=[pltpu.VMEM((K, tn), BF16)],
        compiler_params=_cparams("parallel", "arbitrary"),
        name="mm",
    )(*args)


def _mm_resid_kernel(xp_ref, xs_ref, w_ref, y_ref, g_ref, gate_ref, o_ref, wbf, *, n_ptiles):
    i = pl.program_id(0)

    @pl.when(i == 0)
    def _():
        wbf[...] = w_ref[...].astype(BF16)

    x = jnp.where(i < n_ptiles, xp_ref[...], xs_ref[...])
    a = _dot(x.astype(BF16), wbf[...])
    o_ref[...] = y_ref[...] + gate_ref[0] * (_rms(a) * g_ref[...])


def _mm_resid(x_p, x_s, w, y, gain, gate, *, layer, n_prompt, t_sample, tm=256):
    M, D = y.shape
    K = w.shape[-2]
    tm = _row_tile(n_prompt, t_sample, tm)
    n_ptiles = n_prompt // tm
    grp = functools.partial(_group_of, tm=tm, n_prompt=n_prompt, t_sample=t_sample)
    return pl.pallas_call(
        functools.partial(_mm_resid_kernel, n_ptiles=n_ptiles),
        out_shape=SDS((M, D), F32),
        grid=(M // tm,),
        in_specs=[
            pl.BlockSpec((tm, K), lambda i: (jnp.minimum(i, n_ptiles - 1), 0)),
            pl.BlockSpec((tm, K), lambda i: (jnp.maximum(i - n_ptiles, 0), 0)),
            pl.BlockSpec((None, K, D), lambda i: (layer, 0, 0), pipeline_mode=pl.Buffered(1)),
            pl.BlockSpec((tm, D), lambda i: (i, 0)),
            pl.BlockSpec((1, D), lambda i: (0, 0)),
            pl.BlockSpec((1, 1, D), lambda i: (grp(i), 0, 0)),
        ],
        out_specs=pl.BlockSpec((tm, D), lambda i: (i, 0)),
        scratch_shapes=[pltpu.VMEM((K, D), BF16)],
        compiler_params=_cparams("arbitrary"),
        name="mm_resid",
    )(x_p, x_s, w, y, gain.reshape(1, D), gate)


def _cast_kernel(x_ref, o_ref):
    o_ref[...] = x_ref[...].astype(o_ref.dtype)


def _cast_bf16(w, layer, *, tr=1024):
    shape = w.shape[1:]
    R = math.prod(shape[:-1])
    N = shape[-1]
    w2 = w.reshape(w.shape[0] * R, N)
    tr = _tile(R, tr)
    nb = R // tr
    out = pl.pallas_call(
        _cast_kernel,
        out_shape=SDS((R, N), BF16),
        grid=(nb,),
        in_specs=[pl.BlockSpec((tr, N), lambda i: (layer * nb + i, 0))],
        out_specs=pl.BlockSpec((tr, N), lambda i: (i, 0)),
        compiler_params=_cparams("parallel"),
        name="cast_bf16",
    )(w2)
    return out.reshape(shape)


def _gate_up_kernel(x_ref, w1_ref, w3_ref, u_ref, wb1, wb3):
    @pl.when(pl.program_id(1) == 0)
    def _():
        wb1[...] = w1_ref[...].astype(BF16)
        wb3[...] = w3_ref[...].astype(BF16)

    x = x_ref[...]
    u_ref[...] = (_silu(_dot(x, wb1[...])) * _dot(x, wb3[...])).astype(u_ref.dtype)


def _gate_up(x, w1, w3, layer, *, tm=1024, tn=512):
    M, D = x.shape
    F = w1.shape[-1]
    tm = _tile(M, tm)
    tn = _tile(F, tn, LANE)
    wspec = _layer_spec(w1, layer, (D, tn), lambda j, i: (0, j))
    return pl.pallas_call(
        _gate_up_kernel,
        out_shape=SDS((M, F), BF16),
        grid=(F // tn, M // tm),
        in_specs=[pl.BlockSpec((tm, D), lambda j, i: (i, 0)), wspec, wspec],
        out_specs=pl.BlockSpec((tm, tn), lambda j, i: (i, j)),
        scratch_shapes=[pltpu.VMEM((D, tn), BF16), pltpu.VMEM((D, tn), BF16)],
        compiler_params=_cparams("parallel", "arbitrary"),
        name="gate_up",
    )(x, w1, w3)


def _down_resid_kernel(u_ref, w_ref, y_ref, g_ref, gate_ref, o_ref, *, nk):
    k = pl.program_id(1)

    @pl.when(k == 0)
    def _():
        o_ref[...] = jnp.zeros_like(o_ref)

    o_ref[...] += _dot(u_ref[...], w_ref[...])

    @pl.when(k == nk - 1)
    def _():
        o_ref[...] = y_ref[...] + gate_ref[0] * (_rms(o_ref[...]) * g_ref[...])


def _down_resid(u, w2b, y, gain, gate, *, n_prompt, t_sample, tm=1024, tk=1408):
    M, D = y.shape
    F = u.shape[1]
    tm = _row_tile(n_prompt, t_sample, tm)
    tk = _tile(F, tk, LANE)
    nk = F // tk
    grp = functools.partial(_group_of, tm=tm, n_prompt=n_prompt, t_sample=t_sample)
    return pl.pallas_call(
        functools.partial(_down_resid_kernel, nk=nk),
        out_shape=SDS((M, D), F32),
        grid=(M // tm, nk),
        in_specs=[
            pl.BlockSpec((tm, tk), lambda i, k: (i, k)),
            pl.BlockSpec((tk, D), lambda i, k: (k, 0)),
            pl.BlockSpec((tm, D), lambda i, k: (i, 0), pipeline_mode=pl.Buffered(1)),
            pl.BlockSpec((1, D), lambda i, k: (0, 0)),
            pl.BlockSpec((1, 1, D), lambda i, k: (grp(i), 0, 0)),
        ],
        out_specs=pl.BlockSpec((tm, D), lambda i, k: (i, 0)),
        compiler_params=_cparams("parallel", "arbitrary"),
        name="down_resid",
    )(u, w2b, y, gain.reshape(1, D), gate)


def _gather_sorted_kernel(tr_ref, src_ref, h_hbm, o_ref, xs, sem, *, tm, S, nt):
    i = pl.program_id(0)

    def start_tile(t, slot):
        def issue(r2, c):
            for pr in range(2):
                r = 2 * r2 + pr
                s0 = pl.multiple_of(src_ref[t * tm + r] * S, S)
                pltpu.make_async_copy(h_hbm.at[pl.ds(s0, S)], xs.at[slot, pl.ds(pl.multiple_of(r * S, S), S)],
                                      sem.at[slot]).start(priority=pr)
            return c

        lax.fori_loop(0, tm // 2, issue, 0)

    def finish_tile(slot):
        pltpu.make_async_copy(h_hbm.at[pl.ds(0, tm * S)], xs.at[slot], sem.at[slot]).wait()
        for s in range(S):
            o_ref[:, s * LANE:(s + 1) * LANE] = _from_slab(xs.at[slot], tm, s, S).astype(o_ref.dtype)

    @pl.when(jnp.logical_and(i == 0, tr_ref[0] > 0))
    def _():
        start_tile(0, 0)

    nxt = jnp.minimum(i + 1, nt - 1)
    for slot in range(2):
        @pl.when(jnp.logical_and(jnp.logical_and(i + 1 < nt, tr_ref[nxt] > 0), i % 2 == slot))
        def _():
            start_tile(i + 1, 1 - slot)

    @pl.when(tr_ref[i] == 0)
    def _():
        o_ref[...] = jnp.zeros_like(o_ref)

    for slot in range(2):
        @pl.when(jnp.logical_and(tr_ref[i] > 0, i % 2 == slot))
        def _():
            finish_tile(slot)


def _gather_sorted(h_slab, tile_rows, src_tok, *, tm, D):
    S = D // LANE
    nt = tile_rows.shape[0]
    grid_spec = pltpu.PrefetchScalarGridSpec(
        num_scalar_prefetch=2,
        grid=(nt,),
        in_specs=[pl.BlockSpec(memory_space=pl.ANY)],
        out_specs=pl.BlockSpec((tm, D), lambda i, tr, sr: (i, 0)),
        scratch_shapes=[pltpu.VMEM((2, tm * S, LANE), F32), pltpu.SemaphoreType.DMA((2,))],
    )
    return pl.pallas_call(
        functools.partial(_gather_sorted_kernel, tm=tm, S=S, nt=nt),
        out_shape=SDS((nt * tm, D), BF16),
        grid_spec=grid_spec,
        compiler_params=_cparams("arbitrary"),
        name="gather_sorted",
    )(tile_rows, src_tok, h_slab)


def _moe_gate_up_kernel(te_ref, tr_ref, tf_ref, x_ref, w1_ref, w3_ref, u_ref, wb1, wb3):
    i = pl.program_id(1)

    @pl.when(tf_ref[i] > 0)
    def _():
        wb1[...] = w1_ref[...].astype(BF16)
        wb3[...] = w3_ref[...].astype(BF16)

    def gate_up(rows):
        x = x_ref[rows, :]
        u_ref[rows, :] = (_silu(_dot(x, wb1[...])) * _dot(x, wb3[...])).astype(u_ref.dtype)

    nrows = tr_ref[i]
    half = x_ref.shape[0] // 2
    lower, upper = pl.ds(0, half), pl.ds(half, half)

    @pl.when(nrows > half)
    def _():
        gate_up(pl.ds(0, 2 * half))

    @pl.when(jnp.logical_and(nrows > 0, nrows <= half))
    def _():
        gate_up(lower)
        u_ref[upper, :] = jnp.zeros((half, u_ref.shape[1]), u_ref.dtype)

    @pl.when(nrows == 0)
    def _():
        u_ref[...] = jnp.zeros_like(u_ref)


def _moe_gate_up(x_sorted, tile_e, tile_rows, tile_first, w1, w3, layer, *, tm, tn=512):
    R, D = x_sorted.shape
    F = w1.shape[-1]
    tn = _tile(F, tn, LANE)
    wspec = pl.BlockSpec((None, None, D, tn), lambda j, i, te, tr, tf: (layer, te[i], 0, j))
    grid_spec = pltpu.PrefetchScalarGridSpec(
        num_scalar_prefetch=3,
        grid=(F // tn, R // tm),
        in_specs=[pl.BlockSpec((tm, D), lambda j, i, te, tr, tf: (i, 0)), wspec, wspec],
        out_specs=pl.BlockSpec((tm, tn), lambda j, i, te, tr, tf: (i, j)),
        scratch_shapes=[pltpu.VMEM((D, tn), BF16), pltpu.VMEM((D, tn), BF16)],
    )
    return pl.pallas_call(
        _moe_gate_up_kernel,
        out_shape=SDS((R, F), BF16),
        grid_spec=grid_spec,
        compiler_params=_cparams("parallel", "arbitrary"),
        name="moe_gate_up",
    )(tile_e, tile_rows, tile_first, x_sorted, w1, w3)


def _moe_down_kernel(te_ref, tr_ref, dst_ref, u_ref, w_ref, gate_ref, out_hbm, acc, stage, sem,
                     *, nk, nt, tm, S, chunk):
    i = pl.program_id(0)
    k = pl.program_id(1)
    nrows = tr_ref[i]
    base = i * tm

    def row_copy(slot, r, d0):
        return pltpu.make_async_copy(stage.at[slot, pl.ds(pl.multiple_of(r * S, S), S)],
                                     out_hbm.at[pl.ds(d0, S)], sem.at[slot])

    def drain(slot, n):
        for c in range(tm // chunk):
            @pl.when(n >= (c + 1) * chunk)
            def _():
                pltpu.make_async_copy(stage.at[slot, pl.ds(c * chunk * S, chunk * S)],
                                      out_hbm.at[pl.ds(0, chunk * S)], sem.at[slot]).wait()

        def one(r, c):
            row_copy(slot, r, 0).wait()
            return c

        lax.fori_loop((n // chunk) * chunk, n, one, 0)

    @pl.when(nrows > 0)
    def _():
        @pl.when(k == 0)
        def _():
            acc[...] = jnp.zeros_like(acc)

        half = tm // 2

        @pl.when(nrows > half)
        def _():
            acc[...] += _dot(u_ref[...], w_ref[0])

        @pl.when(nrows <= half)
        def _():
            acc[pl.ds(0, half), :] += _dot(u_ref[pl.ds(0, half), :], w_ref[0])

        for slot in range(2):
            @pl.when(jnp.logical_and(k == nk - 1, i % 2 == slot))
            def _():
                _to_slab(stage.at[slot], acc[...] * gate_ref[...])

                def issue_row(r, pr):
                    row_copy(slot, r, pl.multiple_of(dst_ref[base + r] * S, S)).start(priority=pr)

                def issue(r2, c):
                    for pr in range(2):
                        issue_row(2 * r2 + pr, pr)
                    return c

                lax.fori_loop(0, nrows // 2, issue, 0)

                @pl.when(nrows % 2 == 1)
                def _():
                    issue_row(nrows - 1, 0)

                @pl.when(i > 0)
                def _():
                    drain(1 - slot, tr_ref[jnp.maximum(i - 1, 0)])

                @pl.when(jnp.logical_or(i == nt - 1, tr_ref[jnp.minimum(i + 1, nt - 1)] == 0))
                def _():
                    drain(slot, nrows)


def _moe_down(u_sorted, tile_e, tile_rows, dst_row, gate_rows, w2b, *, n_tok, tm, tk=1408):
    R, F = u_sorted.shape
    D = w2b.shape[2]
    S = D // LANE
    tk = _tile(F, tk, LANE)
    nk = F // tk

    def kidx(i, k, tr):
        return jnp.where(tr[i] > 0, k, nk - 1)

    grid_spec = pltpu.PrefetchScalarGridSpec(
        num_scalar_prefetch=3,
        grid=(R // tm, nk),
        in_specs=[
            pl.BlockSpec((tm, tk), lambda i, k, te, tr, ds: (i, kidx(i, k, tr))),
            pl.BlockSpec((1, tk, D), lambda i, k, te, tr, ds: (te[i], kidx(i, k, tr), 0)),
            pl.BlockSpec((tm, 1), lambda i, k, te, tr, ds: (i, 0)),
        ],
        out_specs=pl.BlockSpec(memory_space=pl.ANY),
        scratch_shapes=[pltpu.VMEM((tm, D), F32), pltpu.VMEM((2, tm * S, LANE), F32),
                        pltpu.SemaphoreType.DMA((2,))],
    )
    return pl.pallas_call(
        functools.partial(_moe_down_kernel, nk=nk, nt=R // tm, tm=tm, S=S, chunk=_tile(tm, 64)),
        out_shape=SDS((2 * n_tok * S, LANE), F32),
        grid_spec=grid_spec,
        compiler_params=pltpu.CompilerParams(dimension_semantics=("arbitrary", "arbitrary"),
                                             vmem_limit_bytes=VMEM_LIMIT_BYTES, has_side_effects=True),
        name="moe_down",
    )(tile_e, tile_rows, dst_row, u_sorted, w2b, gate_rows)


def _combine_kernel(a_ref, b_ref, y_ref, g_ref, gate_ref, o_ref, *, tm, S):
    parts = [_from_slab(a_ref, tm, s, S) + _from_slab(b_ref, tm, s, S) for s in range(S)]
    ss = parts[0] * parts[0]
    for t in parts[1:]:
        ss = ss + t * t
    r = lax.rsqrt(jnp.sum(ss, axis=-1, keepdims=True) / (S * LANE) + EPS)
    for s, t in enumerate(parts):
        c = slice(s * LANE, (s + 1) * LANE)
        o_ref[:, c] = y_ref[:, c] + gate_ref[0][:, c] * (t * r * g_ref[:, c])


def _moe_combine(slots, y, gain, gate, *, n_prompt, t_sample, tm=256):
    M, D = y.shape
    S = D // LANE
    tm = _row_tile(n_prompt, t_sample, tm)
    nb = M // tm
    grp = functools.partial(_group_of, tm=tm, n_prompt=n_prompt, t_sample=t_sample)
    return pl.pallas_call(
        functools.partial(_combine_kernel, tm=tm, S=S),
        out_shape=SDS((M, D), F32),
        grid=(nb,),
        in_specs=[
            pl.BlockSpec((tm * S, LANE), lambda i: (i, 0)),
            pl.BlockSpec((tm * S, LANE), lambda i: (nb + i, 0)),
            pl.BlockSpec((tm, D), lambda i: (i, 0)),
            pl.BlockSpec((1, D), lambda i: (0, 0)),
            pl.BlockSpec((1, 1, D), lambda i: (grp(i), 0, 0)),
        ],
        out_specs=pl.BlockSpec((tm, D), lambda i: (i, 0)),
        compiler_params=_cparams("parallel"),
        name="moe_combine",
    )(slots, slots, y, gain.reshape(1, D), gate)


def _dispatch_tables(sel, gw, tm):
    M = sel.shape[0]
    E = N_EXPERTS
    i32 = jnp.int32
    e_flat = sel.reshape(-1)
    pair = jnp.arange(2 * M, dtype=i32)
    onehot = (e_flat[:, None] == jnp.arange(E, dtype=i32)[None, :]).astype(i32)
    running = jnp.cumsum(onehot, axis=0)
    rank = jnp.sum(running * onehot, axis=1) - 1
    counts = running[-1]
    padded = ((counts + tm - 1) // tm) * tm
    ends_p = jnp.cumsum(padded)
    starts_p = ends_p - padded
    pos = starts_p[e_flat] + rank
    R = 2 * M + E * tm
    payload = jnp.stack([pair // 2, (pair % 2) * M + pair // 2,
                         lax.bitcast_convert_type(gw.reshape(-1), i32)], axis=1)
    table = jnp.zeros((R, 3), i32).at[pos].set(payload)
    src_tok, dst_row = table[:, 0], table[:, 1]
    gate_rows = lax.bitcast_convert_type(table[:, 2], F32)
    tile_start = jnp.arange(R // tm, dtype=i32) * tm
    tile_e = jnp.minimum(jnp.sum(ends_p[None, :] <= tile_start[:, None], axis=1), E - 1).astype(i32)
    tile_rows = jnp.clip(starts_p[tile_e] + counts[tile_e] - tile_start, 0, tm).astype(i32)
    tile_first = jnp.concatenate([jnp.ones((1,), i32), (tile_e[1:] != tile_e[:-1]).astype(i32)])
    return tile_e, tile_rows, tile_first, src_tok, dst_row, gate_rows.reshape(R, 1)


def _attn_kernel(*refs, n_parts, k_modes, v_mode, hb, scale, nk):
    q_refs = refs[:n_parts]
    k_refs = refs[n_parts:2 * n_parts]
    v_ref = refs[2 * n_parts]
    o_ref = refs[2 * n_parts + 1]
    m_sc, l_sc, acc_sc = refs[2 * n_parts + 2:]
    j = pl.program_id(3)
    c_exp = scale * math.log2(math.e)

    @pl.when(j == 0)
    def _():
        m_sc[...] = jnp.full_like(m_sc, -jnp.inf)
        l_sc[...] = jnp.zeros_like(l_sc)
        acc_sc[...] = jnp.zeros_like(acc_sc)

    def block(ref, mode, hh):
        x = ref[:, hh * LANE:(hh + 1) * LANE] if mode == "head" else ref[...]
        return x.astype(BF16)

    for hh in range(hb):
        q = [block(q_refs[p], "head", hh) for p in range(n_parts)]
        k = [block(k_refs[p], k_modes[p], hh) for p in range(n_parts)]
        q = q[0] if n_parts == 1 else jnp.concatenate(q, axis=1)
        k = k[0] if n_parts == 1 else jnp.concatenate(k, axis=1)
        s = _dot_nt(q, k)
        tiles = [s[:, c * LANE:(c + 1) * LANE] for c in range(s.shape[1] // LANE)]
        m_lane = tiles[0]
        for t in tiles[1:]:
            m_lane = jnp.maximum(m_lane, t)
        m_prev = m_sc[hh]
        m_new = jnp.maximum(m_prev, jnp.max(m_lane, axis=-1, keepdims=True))
        alpha = jnp.exp2((m_prev - m_new) * c_exp)
        p = [jnp.exp2((t - m_new) * c_exp) for t in tiles]
        l_lane = p[0]
        for t in p[1:]:
            l_lane = l_lane + t
        l_sc[hh] = alpha * l_sc[hh] + l_lane
        pb = jnp.concatenate([t.astype(BF16) for t in p], axis=1)
        acc_sc[hh] = alpha * acc_sc[hh] + _dot(pb, block(v_ref, v_mode, hh))
        m_sc[hh] = m_new

    @pl.when(j == nk - 1)
    def _():
        for hh in range(hb):
            l = jnp.sum(l_sc[hh], axis=-1, keepdims=True)
            o_ref[:, hh * LANE:(hh + 1) * LANE] = (acc_sc[hh] / l).astype(o_ref.dtype)


def _attention(q_parts, k_parts, v_part, *, B, T, S, H, hb, q_row0, k_row0, tq, tk, scale):
    tq = _tile(T, tq)
    tk = _tile(S, tk)
    nq, nk = T // tq, S // tk
    qb0, kb0 = q_row0 // tq, k_row0 // tk

    def q_spec(off):
        return pl.BlockSpec((tq, hb * LANE), lambda b, g, i, j: (qb0 + b * nq + i, off // hb + g))

    def k_spec(off, mode):
        if mode == "head":
            return pl.BlockSpec((tk, hb * LANE), lambda b, g, i, j: (kb0 + b * nk + j, off // hb + g))
        return pl.BlockSpec((tk, LANE), lambda b, g, i, j: (kb0 + b * nk + j, off + (g if mode == "group" else 0)))

    in_specs = [q_spec(off) for (_, off) in q_parts]
    in_specs += [k_spec(off, mode) for (_, off, mode) in k_parts]
    in_specs += [k_spec(v_part[1], v_part[2])]
    args = [a for (a, _) in q_parts] + [a for (a, _, _) in k_parts] + [v_part[0]]
    return pl.pallas_call(
        functools.partial(_attn_kernel, n_parts=len(q_parts), k_modes=[m for (_, _, m) in k_parts],
                          v_mode=v_part[2], hb=hb, scale=scale, nk=nk),
        out_shape=SDS((B * T, H * LANE), BF16),
        grid=(B, H // hb, nq, nk),
        in_specs=in_specs,
        out_specs=pl.BlockSpec((tq, hb * LANE), lambda b, g, i, j: (b * nq + i, g)),
        scratch_shapes=[pltpu.VMEM((hb, tq, LANE), F32), pltpu.VMEM((hb, tq, LANE), F32),
                        pltpu.VMEM((hb, tq, LANE), F32)],
        compiler_params=_cparams("parallel", "parallel", "parallel", "arbitrary"),
        name="attention",
    )(*args)


def _rmsnorm_cols_kernel(x_ref, g_ref, o_ref):
    o_ref[...] = (_rms(x_ref[...]) * g_ref[...]).astype(o_ref.dtype)


def _rmsnorm_cols(x, gain, *, col_block, width, out_dtype, tm=1024):
    M = x.shape[0]
    tm = _tile(M, tm)
    return pl.pallas_call(
        _rmsnorm_cols_kernel,
        out_shape=SDS((M, width), out_dtype),
        grid=(M // tm,),
        in_specs=[pl.BlockSpec((tm, width), lambda i: (i, col_block)),
                  pl.BlockSpec((1, width), lambda i: (0, 0))],
        out_specs=pl.BlockSpec((tm, width), lambda i: (i, 0)),
        compiler_params=_cparams("parallel"),
        name="rmsnorm_cols",
    )(x, gain.reshape(1, width))


def _rope(x, cos, sin, half):
    lane = lax.broadcasted_iota(jnp.int32, x.shape, 1)
    first = (lane % (2 * half)) < half
    rot = jnp.where(first, -pltpu.roll(x, LANE - half, 1), pltpu.roll(x, half, 1))
    return x * cos + rot * sin


def _head_rope_kernel(x_ref, g_ref, cos_ref, sin_ref, o_ref, *, norm, half, hw):
    cos, sin = cos_ref[...], sin_ref[...]
    for c in range(hw):
        cols = slice(c * LANE, (c + 1) * LANE)
        x = x_ref[:, cols]
        if norm:
            x = _rms(x) * g_ref[c]
        o_ref[:, cols] = _rope(x, cos, sin, half).astype(o_ref.dtype)


def _head_rope(x, gains, cos, sin, *, col_block0, n_heads, norm, half, out_dtype, n_rows=None, tm=1024):
    M = x.shape[0] if n_rows is None else n_rows
    tm = _tile(M, tm)
    hw = math.gcd(math.gcd(n_heads, col_block0) if col_block0 else n_heads, 4)
    return pl.pallas_call(
        functools.partial(_head_rope_kernel, norm=norm, half=half, hw=hw),
        out_shape=SDS((M, n_heads * LANE), out_dtype),
        grid=(M // tm, n_heads // hw),
        in_specs=[
            pl.BlockSpec((tm, hw * LANE), lambda i, h: (i, col_block0 // hw + h)),
            pl.BlockSpec((hw, 1, LANE), lambda i, h: (h, 0, 0)),
            pl.BlockSpec((tm, LANE), lambda i, h: (i, 0)),
            pl.BlockSpec((tm, LANE), lambda i, h: (i, 0)),
        ],
        out_specs=pl.BlockSpec((tm, hw * LANE), lambda i, h: (i, h)),
        compiler_params=_cparams("parallel", "parallel"),
        name="head_rope",
    )(x, gains, cos, sin)


def _rope_tables(n_prompt, dec_batch, t_sample, half):
    tok = jnp.arange(t_sample, dtype=jnp.int32)
    row, col = tok // GRID_W, tok % GRID_W
    inv = ROPE_THETA ** (-jnp.arange(half, dtype=F32) / half)

    def cs(pos):
        ang = pos.astype(F32)[:, None] * inv[None, :]
        return (jnp.concatenate([jnp.cos(ang), jnp.cos(ang)], -1),
                jnp.concatenate([jnp.sin(ang), jnp.sin(ang)], -1))

    cr, sr = cs(row)
    cc, sc = cs(col)
    pad = LANE - 4 * half
    cos = jnp.concatenate([cr, cc, jnp.ones((t_sample, pad), F32)], -1)
    sin = jnp.concatenate([sr, sc, jnp.zeros((t_sample, pad), F32)], -1)
    cos = jnp.concatenate([jnp.ones((n_prompt, LANE), F32), jnp.tile(cos, (dec_batch, 1))], 0)
    sin = jnp.concatenate([jnp.zeros((n_prompt, LANE), F32), jnp.tile(sin, (dec_batch, 1))], 0)
    return cos, sin


def _hgrn_constants(C, reverse):
    L = int(np.log2(C))
    t = np.arange(C)[:, None]
    u = np.arange(C)[None, :]
    Ws, Ms = [], []
    for lvl in range(L):
        b = 1 << lvl
        grp = t // (2 * b)
        if not reverse:
            bnd = grp * 2 * b + b - 1
            qside = (t % (2 * b)) >= b
            W = np.where(qside, (u > bnd) & (u <= t), (u > t) & (u <= bnd))
            kside_s = (u % (2 * b)) < b
        else:
            bnd = grp * 2 * b + b
            qside = (t % (2 * b)) < b
            W = np.where(qside, (u >= t) & (u < bnd), (u >= bnd) & (u < t))
            kside_s = (u % (2 * b)) >= b
        Ws.append(W)
        Ms.append(qside & kside_s & (grp == u // (2 * b)))
    if not reverse:
        Ws += [u <= t, u > t]
    else:
        Ws += [u >= t, u < t]
    Ms.append(t == u)
    W_all = jnp.asarray(np.concatenate(Ws, 0).astype(np.float32), BF16)
    M_all = jnp.asarray(np.stack(Ms).astype(np.float32), F32)
    return W_all, M_all


def _hgrn_decays(z, lb, W_all):
    f = lb + (1.0 - lb) * jax.nn.sigmoid(z)
    lf = jnp.log(f)
    hi = lf.astype(BF16)
    lo = (lf - hi.astype(F32)).astype(BF16)
    w = z.shape[1]
    d2 = _dot(W_all, jnp.concatenate([hi, lo], axis=1))
    return 1.0 - f, d2[:, :w] + d2[:, w:]


def _hgrn_scores(q, k, dn, m_ref, *, C, L):
    e_all = jnp.exp(dn)
    a = m_ref[L] * _dot_nt(q.astype(BF16), k.astype(BF16))
    for lvl in range(L):
        e = e_all[lvl * C:(lvl + 1) * C]
        a = a + m_ref[lvl] * _dot_nt((q * e).astype(BF16), (k * e).astype(BF16))
    eq = e_all[L * C:(L + 1) * C]
    ek = e_all[(L + 1) * C:(L + 2) * C]
    return a, (q * eq).astype(BF16), (k * ek).astype(BF16)


def _hgrn_kernel(*refs, C, L, nc, hb, has_init):
    (qf_ref, vf_ref, zf_ref, qb_ref, vb_ref, zb_ref, lbf_ref, lbb_ref,
     wf_ref, mf_ref, wb_ref, mb_ref) = refs[:12]
    rest = refs[12:]
    if has_init:
        s0_ref, rest = rest[0], rest[1:]
    of_ref, ob_ref, sfin_ref, stf, stb = rest
    c = pl.program_id(2)

    @pl.when(c == 0)
    def _():
        for hh in range(hb):
            if has_init:
                stf[hh] = s0_ref[0, 0, hh]
                stb[hh] = s0_ref[0, 1, hh]
            else:
                stf[hh] = jnp.zeros((LANE, LANE), F32)
                stb[hh] = jnp.zeros((LANE, LANE), F32)

    dirs = [(qf_ref, vf_ref, zf_ref, lbf_ref, wf_ref, mf_ref, stf, of_ref, C - 1),
            (qb_ref, vb_ref, zb_ref, lbb_ref, wb_ref, mb_ref, stb, ob_ref, 0)]
    decays = [_hgrn_decays(z_ref[...], lb_ref[0], w_ref[...]) for (_, _, z_ref, lb_ref, w_ref, _, _, _, _) in dirs]
    chains = []
    for (q_ref, v_ref, _, _, _, m_ref, st, o_ref, edge_row), (k_all, dn_all) in zip(dirs, decays):
        for hh in range(hb):
            cols = slice(hh * LANE, (hh + 1) * LANE)
            chains.append((q_ref[:, cols], v_ref[:, cols], k_all[:, cols], dn_all[:, cols], m_ref, st, o_ref,
                           edge_row, hh, cols))
    scores = [_hgrn_scores(q, k, dn, m_ref, C=C, L=L) for (q, _, k, dn, m_ref, _, _, _, _, _) in chains]
    for (q, v, k, dn, _, st, o_ref, edge_row, hh, cols), (a, qe, ke) in zip(chains, scores):
        vb = v.astype(BF16)
        o_ref[:, cols] = _dot(a.astype(BF16), vb) + _dot_nt(qe, st[hh].astype(BF16))
        g_edge = dn[L * C + edge_row:L * C + edge_row + 1]
        st[hh] = st[hh] * jnp.exp(g_edge) + _dot_tn(vb, ke)

    @pl.when(c == nc - 1)
    def _():
        for hh in range(hb):
            sfin_ref[0, 0, hh] = stf[hh].T
            sfin_ref[0, 1, hh] = stb[hh].T


def _hgrn_scan(proj, lb, s0t, *, B, T, row0, C=128, hb=4):
    D = proj.shape[1] // 5
    H = D // LANE
    C = _tile(T, C)
    L = int(np.log2(C))
    assert 1 << L == C
    nc = T // C
    rb0 = row0 // C
    HB = H // hb
    W = hb * LANE
    wf, mf = _hgrn_constants(C, False)
    wb, mb = _hgrn_constants(C, True)

    def fwd(col0):
        return pl.BlockSpec((C, W), lambda b, g, c: (rb0 + b * nc + c, col0 // hb + g))

    def bwd(col0):
        return pl.BlockSpec((C, W), lambda b, g, c: (rb0 + b * nc + nc - 1 - c, col0 // hb + g))

    def const(a):
        return pl.BlockSpec(a.shape, lambda b, g, c: (0,) * a.ndim)

    in_specs = [fwd(0), fwd(H), fwd(3 * H), bwd(0), bwd(H), bwd(4 * H),
                pl.BlockSpec((1, 1, W), lambda b, g, c: (0, 0, g)),
                pl.BlockSpec((1, 1, W), lambda b, g, c: (1, 0, g)),
                const(wf), const(mf), const(wb), const(mb)]
    args = [proj, proj, proj, proj, proj, proj, lb, lb, wf, mf, wb, mb]
    if s0t is not None:
        in_specs.append(pl.BlockSpec((1, 2, hb, LANE, LANE), lambda b, g, c: (b, 0, g, 0, 0)))
        args.append(s0t)
    return pl.pallas_call(
        functools.partial(_hgrn_kernel, C=C, L=L, nc=nc, hb=hb, has_init=s0t is not None),
        out_shape=(SDS((B * T, D), F32), SDS((B * T, D), F32), SDS((B, 2, H, LANE, LANE), F32)),
        grid=(B, HB, nc),
        in_specs=in_specs,
        out_specs=(
            pl.BlockSpec((C, W), lambda b, g, c: (b * nc + c, g)),
            pl.BlockSpec((C, W), lambda b, g, c: (b * nc + nc - 1 - c, g)),
            pl.BlockSpec((1, 2, hb, LANE, LANE), lambda b, g, c: (b, 0, g, 0, 0)),
        ),
        scratch_shapes=[pltpu.VMEM((hb, LANE, LANE), F32), pltpu.VMEM((hb, LANE, LANE), F32)],
        compiler_params=_cparams("parallel", "parallel", "arbitrary"),
        name="hgrn_scan",
    )(*args)


def _hgrn_gate_kernel(of_ref, ob_ref, g_ref, w_ref, o_ref, *, hw):
    for c in range(hw):
        cols = slice(c * LANE, (c + 1) * LANE)
        o = _rms(of_ref[:, cols] + ob_ref[:, cols]) * w_ref[c]
        o_ref[:, cols] = (o * _silu(g_ref[:, cols])).astype(o_ref.dtype)


def _hgrn_gate(o_f, o_b, proj, o_norm, *, row0, tm=1024, hw=4):
    M, D = o_f.shape
    H = D // LANE
    tm = _tile(math.gcd(M, row0) if row0 else M, tm)
    rb0 = row0 // tm
    blk = pl.BlockSpec((tm, hw * LANE), lambda i, h: (i, h))
    return pl.pallas_call(
        functools.partial(_hgrn_gate_kernel, hw=hw),
        out_shape=SDS((M, D), BF16),
        grid=(M // tm, H // hw),
        in_specs=[blk, blk,
                  pl.BlockSpec((tm, hw * LANE), lambda i, h: (rb0 + i, 2 * H // hw + h)),
                  pl.BlockSpec((hw, 1, LANE), lambda i, h: (h, 0, 0))],
        out_specs=blk,
        compiler_params=_cparams("parallel", "parallel"),
        name="hgrn_gate",
    )(o_f, o_b, proj, o_norm.reshape(H, 1, LANE))


def kernel(x_prompt, x_sample, c, cache_mla_ckv, cache_mla_kpe, state_hgrn, cache_gqa_k, cache_gqa_v, c_ctx, ada_w, ada_b, norm_w, mla_wq_a, mla_q_norm, mla_wq_b, mla_wkv_a, mla_kv_norm, mla_wkv_b, mla_wo, hgrn_w_in, hgrn_lb_logits, hgrn_o_norm, hgrn_wo, gqa_w_qkv, gqa_q_norm, gqa_k_norm, gqa_wo, ffn_w1, ffn_w3, ffn_w2, moe_router, moe_w1, moe_w3, moe_w2):
    Bp, Tp, D = x_prompt.shape
    Bs, Ts, _ = x_sample.shape
    P = cache_mla_ckv.shape[2]
    depth = ada_w.shape[0]
    NP, NS = Bp * Tp, Bs * Ts
    M = NP + NS
    G = 1 + Bs
    assert D == HG_HEADS * LANE
    rows = dict(n_prompt=NP, t_sample=Ts)

    y = jnp.concatenate([x_prompt.reshape(NP, D), x_sample.reshape(NS, D)], axis=0)
    cond8 = jnp.zeros((8, D), F32).at[0].set(c_ctx).at[1:G].set(c)
    mod = _modulation(cond8, ada_w, ada_b).reshape(depth, 8, 6, D)
    mod = jnp.transpose(mod, (0, 2, 1, 3))[:, :, :G, None, :]

    tm_tok = _row_tile(NP, Ts, 512)
    cos_mla, sin_mla = _rope_tables(NP, Bs, Ts, MLA_ROPE // 4)
    cos_gqa, sin_gqa = _rope_tables(NP, Bs, Ts, GQA_HD // 4)
    ones_g = jnp.ones((MLA_HEADS, 1, LANE), F32)

    ckv_list, kpe_list, hg_list, gk_list, gv_list = [], [], [], [], []
    for layer in range(depth):
        kind, j = layer % 3, layer // 3
        nw = norm_w[layer]
        sh1, sc1, g1, sh2, sc2, g2 = (mod[layer, k] for k in range(6))
        h = _adaln_in(y, nw[0], sh1, sc1, tm=tm_tok, **rows)

        if kind == 0:
            Hm = MLA_HEADS
            w_a = jnp.concatenate([mla_wq_a[j], mla_wkv_a[j],
                                   jnp.zeros((D, LANE - MLA_ROPE), F32)], axis=1)
            a = _mm(h, w_a, out_dtype=F32)
            ql, kvl = mla_wq_a.shape[2], mla_kv_norm.shape[1]
            qn = _rmsnorm_cols(a, mla_q_norm[j], col_block=0, width=ql, out_dtype=BF16)
            ckv = _rmsnorm_cols(a, mla_kv_norm[j], col_block=ql // kvl, width=kvl, out_dtype=F32)
            wqb = mla_wq_b[j].reshape(ql, Hm, MLA_NOPE + MLA_ROPE)
            wqb = jnp.concatenate([
                wqb[:, :, :MLA_NOPE].reshape(ql, Hm * MLA_NOPE),
                jnp.pad(wqb[:, :, MLA_NOPE:], ((0, 0), (0, 0), (0, LANE - MLA_ROPE))).reshape(ql, Hm * LANE),
            ], axis=1)
            q = _mm(qn, wqb, out_dtype=BF16, rope=(cos_mla, sin_mla, Hm * MLA_NOPE, MLA_ROPE // 4))
            pe_blk = (ql + kvl) // LANE
            k_pe = _head_rope(a, ones_g, cos_mla, sin_mla, col_block0=pe_blk, n_heads=1, norm=False,
                              half=MLA_ROPE // 4, out_dtype=BF16)
            wkvb = mla_wkv_b[j].reshape(kvl, Hm, MLA_NOPE + MLA_V)
            wkvb = jnp.concatenate([wkvb[:, :, :MLA_NOPE].reshape(kvl, Hm * MLA_NOPE),
                                    wkvb[:, :, MLA_NOPE:].reshape(kvl, Hm * MLA_V)], axis=1)
            ckv_s = jnp.concatenate([ckv[NP:].reshape(Bs, Ts, kvl), cache_mla_ckv[:, j]], axis=1)
            c_all = jnp.concatenate([ckv[:NP], ckv_s.reshape(Bs * (Ts + P), kvl)], axis=0)
            kv = _mm(c_all, wkvb, out_dtype=BF16)
            kpe_ctx = jnp.pad(cache_mla_kpe[:, j], ((0, 0), (0, 0), (0, LANE - MLA_ROPE))).astype(BF16)
            kpe_s = jnp.concatenate([k_pe[NP:].reshape(Bs, Ts, LANE), kpe_ctx], axis=1)
            kpe_all = jnp.concatenate([k_pe[:NP], kpe_s.reshape(Bs * (Ts + P), LANE)], axis=0)
            scale = (MLA_NOPE + MLA_ROPE) ** -0.5
            qp = [(q, 0), (q, Hm)]
            kp = [(kv, 0, "head"), (kpe_all, 0, "shared")]
            vp = (kv, Hm, "head")
            mix_p = _attention(qp, kp, vp, B=Bp, T=Tp, S=Tp, H=Hm, hb=4, q_row0=0, k_row0=0,
                               tq=256, tk=256, scale=scale)
            mix_s = _attention(qp, kp, vp, B=Bs, T=Ts, S=Ts + P, H=Hm, hb=4, q_row0=NP, k_row0=NP,
                               tq=2048, tk=512, scale=scale)
            w_o = mla_wo
            ckv_list.append(ckv[:NP].reshape(Bp, Tp, kvl))
            kpe_list.append(a[:NP, ql + kvl:ql + kvl + MLA_ROPE].reshape(Bp, Tp, MLA_ROPE))
        elif kind == 1:
            H = HG_HEADS
            p = jax.nn.softmax(hgrn_lb_logits.astype(F32), axis=1)
            cum = jnp.cumsum(p, axis=1)
            lb = (cum - cum[:, :1])[:, layer].reshape(2, 1, D)
            proj = _mm(h, hgrn_w_in, layer=j, out_dtype=F32)
            of_p, ob_p, st_p = _hgrn_scan(proj, lb, None, B=Bp, T=Tp, row0=0)
            s0t = jnp.swapaxes(state_hgrn[:, j].astype(F32), -1, -2)
            of_s, ob_s, _ = _hgrn_scan(proj, lb, s0t, B=Bs, T=Ts, row0=NP)
            mix_p = _hgrn_gate(of_p, ob_p, proj, hgrn_o_norm[j], row0=0)
            mix_s = _hgrn_gate(of_s, ob_s, proj, hgrn_o_norm[j], row0=NP)
            w_o = hgrn_wo
            hg_list.append(st_p)
        else:
            Hq, Hk = GQA_HEADS, GQA_KV_HEADS
            qkv = _mm(h, gqa_w_qkv, layer=j, out_dtype=F32)
            gains = jnp.concatenate([jnp.tile(gqa_q_norm[j][None], (Hq, 1)),
                                     jnp.tile(gqa_k_norm[j][None], (Hk, 1))], axis=0).reshape(Hq + Hk, 1, LANE)
            k_plain = _head_rope(qkv, gains[Hq:], cos_gqa, sin_gqa, col_block0=Hq, n_heads=Hk, norm=True,
                                 half=GQA_HD // 4, out_dtype=F32, n_rows=NP)
            qk = _head_rope(qkv, gains, cos_gqa, sin_gqa, col_block0=0, n_heads=Hq + Hk, norm=True,
                            half=GQA_HD // 4, out_dtype=BF16)
            kw = Hk * LANE
            k_new = qk[:, Hq * LANE:]
            v_new = qkv[:, (Hq + Hk) * LANE:].astype(BF16)
            k_s = jnp.concatenate([k_new[NP:].reshape(Bs, Ts, kw),
                                   cache_gqa_k[:, j].reshape(Bs, P, kw).astype(BF16)], axis=1)
            v_s = jnp.concatenate([v_new[NP:].reshape(Bs, Ts, kw),
                                   cache_gqa_v[:, j].reshape(Bs, P, kw).astype(BF16)], axis=1)
            k_all = jnp.concatenate([k_new[:NP], k_s.reshape(Bs * (Ts + P), kw)], axis=0)
            v_all = jnp.concatenate([v_new[:NP], v_s.reshape(Bs * (Ts + P), kw)], axis=0)
            scale = GQA_HD ** -0.5
            qp = [(qk, 0)]
            kp = [(k_all, 0, "group")]
            vp = (v_all, 0, "group")
            mix_p = _attention(qp, kp, vp, B=Bp, T=Tp, S=Tp, H=Hq, hb=Hq // Hk, q_row0=0, k_row0=0,
                               tq=256, tk=256, scale=scale)
            mix_s = _attention(qp, kp, vp, B=Bs, T=Ts, S=Ts + P, H=Hq, hb=Hq // Hk, q_row0=NP, k_row0=NP,
                               tq=2048, tk=512, scale=scale)
            w_o = gqa_wo
            gk_list.append(k_plain[:NP].reshape(Bp, Tp, Hk, GQA_HD))
            gv_list.append(qkv[:NP, (Hq + Hk) * LANE:].reshape(Bp, Tp, Hk, GQA_HD))

        y = _mm_resid(mix_p, mix_s, w_o, y, nw[1], g1, layer=j, **rows)

        fi = layer // 2
        if layer % 2 == 0:
            h2 = _adaln_in(y, nw[2], sh2, sc2, tm=tm_tok, **rows)
            u = _gate_up(h2, ffn_w1, ffn_w3, fi)
            y = _down_resid(u, _cast_bf16(ffn_w2, fi), y, nw[3], g2, **rows)
        else:
            h2, sel, gw = _adaln_in(y, nw[2], sh2, sc2, tm=tm_tok, router=moe_router[fi], **rows)
            tm_e = _tile(M, 1024)
            tile_e, tile_rows, tile_first, src_tok, dst_row, gate_rows = _dispatch_tables(
                sel[:, :2], gw[:, :2], tm_e)
            x_sorted = _gather_sorted(h2, tile_rows, src_tok, tm=tm_e, D=D)
            u = _moe_gate_up(x_sorted, tile_e, tile_rows, tile_first, moe_w1, moe_w3, fi, tm=tm_e)
            slots = _moe_down(u, tile_e, tile_rows, dst_row, gate_rows, _cast_bf16(moe_w2, fi),
                              n_tok=M, tm=tm_e)
            y = _moe_combine(slots, y, nw[3], g2, **rows)

    return (y[:NP].reshape(Bp, Tp, D), y[NP:].reshape(Bs, Ts, D),
            jnp.stack(ckv_list, axis=1), jnp.stack(kpe_list, axis=1), jnp.stack(hg_list, axis=1),
            jnp.stack(gk_list, axis=1), jnp.stack(gv_list, axis=1))
```

```python
import functools
import math

import numpy as np
import jax
import jax.numpy as jnp
from jax import lax
from jax.experimental import pallas as pl
from jax.experimental.pallas import tpu as pltpu

F32 = jnp.float32
BF16 = jnp.bfloat16
EPS = 1e-6
LANE = 128

GRID_W = 64
ROPE_THETA = 10000.0
MLA_HEADS, MLA_NOPE, MLA_ROPE, MLA_V = 16, 128, 64, 128
HG_HEADS = 16
GQA_HEADS, GQA_KV_HEADS, GQA_HD = 16, 4, 128
N_EXPERTS = 8

VMEM_LIMIT_BYTES = 56 * 1024 * 1024

SDS = jax.ShapeDtypeStruct


def _cparams(*sem):
    return pltpu.CompilerParams(dimension_semantics=sem, vmem_limit_bytes=VMEM_LIMIT_BYTES)


def _tile(n, pref, step=8):
    t = min(pref, n)
    while n % t:
        t -= step
    return t


def _dot(a, b):
    return jnp.dot(a, b, preferred_element_type=F32)


def _dot_nt(a, b):
    return lax.dot_general(a, b, (((1,), (1,)), ((), ())), preferred_element_type=F32)


def _dot_tn(a, b):
    return lax.dot_general(a, b, (((0,), (0,)), ((), ())), preferred_element_type=F32)


def _rms(x):
    return x * lax.rsqrt(jnp.mean(x * x, axis=-1, keepdims=True) + EPS)


def _silu(x):
    return x * jax.nn.sigmoid(x)


def _row_tile(n_prompt, t_sample, pref):
    return _tile(math.gcd(n_prompt, t_sample), pref)


def _group_of(i, tm, n_prompt, t_sample):
    r = i * tm
    return jnp.where(r < n_prompt, 0, 1 + (r - n_prompt) // t_sample)


def _mod_kernel(c_ref, w_ref, b_ref, o_ref):
    s = _silu(c_ref[...]).astype(BF16)
    o_ref[0] = _dot(s, w_ref[0].astype(BF16)) + b_ref[0]


def _modulation(cond8, ada_w, ada_b):
    L, D, N6 = ada_w.shape
    tn = _tile(N6, 1024)
    return pl.pallas_call(
        _mod_kernel,
        out_shape=SDS((L, 8, N6), F32),
        grid=(L, N6 // tn),
        in_specs=[
            pl.BlockSpec((8, D), lambda l, j: (0, 0)),
            pl.BlockSpec((1, D, tn), lambda l, j: (l, 0, j)),
            pl.BlockSpec((1, 1, tn), lambda l, j: (l, 0, j)),
        ],
        out_specs=pl.BlockSpec((1, 8, tn), lambda l, j: (l, 0, j)),
        compiler_params=_cparams("parallel", "parallel"),
        name="modulation",
    )(cond8, ada_w, ada_b.reshape(L, 1, N6))


def _adaln_in_kernel(y_ref, g_ref, sh_ref, sc_ref, h_ref):
    h = (_rms(y_ref[...]) * g_ref[...]) * (1.0 + sc_ref[0]) + sh_ref[0]
    h_ref[...] = h.astype(h_ref.dtype)


def _to_slab(ref, x):
    rows, D = x.shape
    S = D // LANE
    for s in range(S):
        ref[pl.ds(s, rows, stride=S), :] = x[:, s * LANE:(s + 1) * LANE]


def _from_slab(ref, rows, s, S):
    return ref[pl.ds(s, rows, stride=S), :]


def _adaln_route_kernel(y_ref, g_ref, sh_ref, sc_ref, r_ref, h_ref, sel_ref, gw_ref):
    h = (_rms(y_ref[...]) * g_ref[...]) * (1.0 + sc_ref[0]) + sh_ref[0]
    _to_slab(h_ref, h)
    logits = lax.dot_general(h, r_ref[...], (((1,), (0,)), ((), ())),
                             precision=lax.Precision.HIGHEST, preferred_element_type=F32)
    lane = lax.broadcasted_iota(jnp.int32, logits.shape, 1)
    neg = jnp.float32(-jnp.inf)
    logits = jnp.where(lane < N_EXPERTS, logits, neg)
    lane_f = lane.astype(F32)
    m1 = jnp.max(logits, axis=-1, keepdims=True)
    i1 = jnp.min(jnp.where(logits == m1, lane_f, float(LANE)), axis=-1, keepdims=True)
    rest = jnp.where(lane_f == i1, neg, logits)
    m2 = jnp.max(rest, axis=-1, keepdims=True)
    i2 = jnp.min(jnp.where(rest == m2, lane_f, float(LANE)), axis=-1, keepdims=True)
    e = jnp.exp(m2 - m1)
    w1 = 1.0 / (1.0 + e)
    w2 = e / (1.0 + e)
    i1, i2 = i1.astype(jnp.int32), i2.astype(jnp.int32)
    sel_ref[...] = jnp.where(lane == 0, i1, jnp.where(lane == 1, i2, 0))
    gw_ref[...] = jnp.where(lane == 0, w1, jnp.where(lane == 1, w2, 0.0))


def _adaln_in(y, gain, shift, scale, *, n_prompt, t_sample, tm, router=None):
    M, D = y.shape
    grp = functools.partial(_group_of, tm=tm, n_prompt=n_prompt, t_sample=t_sample)
    in_specs = [
        pl.BlockSpec((tm, D), lambda i: (i, 0)),
        pl.BlockSpec((1, D), lambda i: (0, 0)),
        pl.BlockSpec((1, 1, D), lambda i: (grp(i), 0, 0)),
        pl.BlockSpec((1, 1, D), lambda i: (grp(i), 0, 0)),
    ]
    args = [y, gain.reshape(1, D), shift, scale]
    if router is None:
        return pl.pallas_call(
            _adaln_in_kernel,
            out_shape=SDS((M, D), BF16),
            grid=(M // tm,),
            in_specs=in_specs,
            out_specs=pl.BlockSpec((tm, D), lambda i: (i, 0)),
            compiler_params=_cparams("parallel"),
            name="adaln_in",
        )(*args)
    r_pad = jnp.pad(router, ((0, 0), (0, LANE - router.shape[1])))
    return pl.pallas_call(
        _adaln_route_kernel,
        out_shape=(SDS((M * (D // LANE), LANE), F32), SDS((M, LANE), jnp.int32), SDS((M, LANE), F32)),
        grid=(M // tm,),
        in_specs=in_specs + [pl.BlockSpec((D, LANE), lambda i: (0, 0))],
        out_specs=(
            pl.BlockSpec((tm * (D // LANE), LANE), lambda i: (i, 0)),
            pl.BlockSpec((tm, LANE), lambda i: (i, 0)),
            pl.BlockSpec((tm, LANE), lambda i: (i, 0)),
        ),
        compiler_params=_cparams("parallel"),
        name="adaln_route",
    )(*args, r_pad)


def _mm_kernel(x_ref, w_ref, o_ref, wbf):
    @pl.when(pl.program_id(1) == 0)
    def _():
        wbf[...] = w_ref[...].astype(BF16)

    o_ref[...] = _dot(x_ref[...].astype(BF16), wbf[...]).astype(o_ref.dtype)


def _layer_spec(w, layer, block, index_map):
    if layer is None:
        return pl.BlockSpec(block, index_map)
    lead = w.ndim - len(block) - 1
    return pl.BlockSpec((None,) * (lead + 1) + tuple(block),
                        lambda *a: (layer,) + (0,) * lead + tuple(index_map(*a)))


def _mm_rope_kernel(x_ref, w_ref, cos_ref, sin_ref, o_ref, wbf, *, rope_from, half):
    j = pl.program_id(0)

    @pl.when(pl.program_id(1) == 0)
    def _():
        wbf[...] = w_ref[...].astype(BF16)

    acc = _dot(x_ref[...].astype(BF16), wbf[...])

    @pl.when(j < rope_from)
    def _():
        o_ref[...] = acc.astype(o_ref.dtype)

    @pl.when(j >= rope_from)
    def _():
        cos, sin = cos_ref[...], sin_ref[...]
        for c in range(acc.shape[1] // LANE):
            cols = slice(c * LANE, (c + 1) * LANE)
            o_ref[:, cols] = _rope(acc[:, cols], cos, sin, half).astype(o_ref.dtype)


def _mm(x, w, *, out_dtype, layer=None, rope=None, tm=1024, tn=1024):
    M, K = x.shape
    N = w.shape[-1]
    tm = _tile(M, tm)
    tn = _tile(N, tn, LANE)
    in_specs = [
        pl.BlockSpec((tm, K), lambda j, i: (i, 0)),
        _layer_spec(w, layer, (K, tn), lambda j, i: (0, j)),
    ]
    args = [x, w]
    body = _mm_kernel
    if rope is not None:
        cos, sin, first_col, half = rope
        assert first_col % tn == 0
        in_specs += [pl.BlockSpec((tm, LANE), lambda j, i: (i, 0))] * 2
        args += [cos, sin]
        body = functools.partial(_mm_rope_kernel, rope_from=first_col // tn, half=half)
    return pl.pallas_call(
        body,
        out_shape=SDS((M, N), out_dtype),
        grid=(N // tn, M // tm),
        in_specs=in_specs,
        out_specs=pl.BlockSpec((tm, tn), lambda j, i: (i, j)),
        scratch_shapes=[pltpu.VMEM((K, tn), BF16)],
        compiler_params=_cparams("parallel", "arbitrary"),
        name="mm",
    )(*args)


def _mm_resid_kernel(xp_ref, xs_ref, w_ref, y_ref, g_ref, gate_ref, o_ref, wbf, *, n_ptiles):
    i = pl.program_id(0)

    @pl.when(i == 0)
    def _():
        wbf[...] = w_ref[...].astype(BF16)

    x = jnp.where(i < n_ptiles, xp_ref[...], xs_ref[...])
    a = _dot(x.astype(BF16), wbf[...])
    o_ref[...] = y_ref[...] + gate_ref[0] * (_rms(a) * g_ref[...])


def _mm_resid(x_p, x_s, w, y, gain, gate, *, layer, n_prompt, t_sample, tm=256):
    M, D = y.shape
    K = w.shape[-2]
    tm = _row_tile(n_prompt, t_sample, tm)
    n_ptiles = n_prompt // tm
    grp = functools.partial(_group_of, tm=tm, n_prompt=n_prompt, t_sample=t_sample)
    return pl.pallas_call(
        functools.partial(_mm_resid_kernel, n_ptiles=n_ptiles),
        out_shape=SDS((M, D), F32),
        grid=(M // tm,),
        in_specs=[
            pl.BlockSpec((tm, K), lambda i: (jnp.minimum(i, n_ptiles - 1), 0)),
            pl.BlockSpec((tm, K), lambda i: (jnp.maximum(i - n_ptiles, 0), 0)),
            pl.BlockSpec((None, K, D), lambda i: (layer, 0, 0), pipeline_mode=pl.Buffered(1)),
            pl.BlockSpec((tm, D), lambda i: (i, 0)),
            pl.BlockSpec((1, D), lambda i: (0, 0)),
            pl.BlockSpec((1, 1, D), lambda i: (grp(i), 0, 0)),
        ],
        out_specs=pl.BlockSpec((tm, D), lambda i: (i, 0)),
        scratch_shapes=[pltpu.VMEM((K, D), BF16)],
        compiler_params=_cparams("arbitrary"),
        name="mm_resid",
    )(x_p, x_s, w, y, gain.reshape(1, D), gate)


def _cast_kernel(x_ref, o_ref):
    o_ref[...] = x_ref[...].astype(o_ref.dtype)


def _cast_bf16(w, layer, *, tr=1024):
    shape = w.shape[1:]
    R = math.prod(shape[:-1])
    N = shape[-1]
    w2 = w.reshape(w.shape[0] * R, N)
    tr = _tile(R, tr)
    nb = R // tr
    out = pl.pallas_call(
        _cast_kernel,
        out_shape=SDS((R, N), BF16),
        grid=(nb,),
        in_specs=[pl.BlockSpec((tr, N), lambda i: (layer * nb + i, 0))],
        out_specs=pl.BlockSpec((tr, N), lambda i: (i, 0)),
        compiler_params=_cparams("parallel"),
        name="cast_bf16",
    )(w2)
    return out.reshape(shape)


def _gate_up_kernel(x_ref, w1_ref, w3_ref, u_ref, wb1, wb3):
    @pl.when(pl.program_id(1) == 0)
    def _():
        wb1[...] = w1_ref[...].astype(BF16)
        wb3[...] = w3_ref[...].astype(BF16)

    x = x_ref[...]
    u_ref[...] = (_silu(_dot(x, wb1[...])) * _dot(x, wb3[...])).astype(u_ref.dtype)


def _gate_up(x, w1, w3, layer, *, tm=1024, tn=512):
    M, D = x.shape
    F = w1.shape[-1]
    tm = _tile(M, tm)
    tn = _tile(F, tn, LANE)
    wspec = _layer_spec(w1, layer, (D, tn), lambda j, i: (0, j))
    return pl.pallas_call(
        _gate_up_kernel,
        out_shape=SDS((M, F), BF16),
        grid=(F // tn, M // tm),
        in_specs=[pl.BlockSpec((tm, D), lambda j, i: (i, 0)), wspec, wspec],
        out_specs=pl.BlockSpec((tm, tn), lambda j, i: (i, j)),
        scratch_shapes=[pltpu.VMEM((D, tn), BF16), pltpu.VMEM((D, tn), BF16)],
        compiler_params=_cparams("parallel", "arbitrary"),
        name="gate_up",
    )(x, w1, w3)


def _down_resid_kernel(u_ref, w_ref, y_ref, g_ref, gate_ref, o_ref, *, nk):
    k = pl.program_id(1)

    @pl.when(k == 0)
    def _():
        o_ref[...] = jnp.zeros_like(o_ref)

    o_ref[...] += _dot(u_ref[...], w_ref[...])

    @pl.when(k == nk - 1)
    def _():
        o_ref[...] = y_ref[...] + gate_ref[0] * (_rms(o_ref[...]) * g_ref[...])


def _down_resid(u, w2b, y, gain, gate, *, n_prompt, t_sample, tm=1024, tk=1408):
    M, D = y.shape
    F = u.shape[1]
    tm = _row_tile(n_prompt, t_sample, tm)
    tk = _tile(F, tk, LANE)
    nk = F // tk
    grp = functools.partial(_group_of, tm=tm, n_prompt=n_prompt, t_sample=t_sample)
    return pl.pallas_call(
        functools.partial(_down_resid_kernel, nk=nk),
        out_shape=SDS((M, D), F32),
        grid=(M // tm, nk),
        in_specs=[
            pl.BlockSpec((tm, tk), lambda i, k: (i, k)),
            pl.BlockSpec((tk, D), lambda i, k: (k, 0)),
            pl.BlockSpec((tm, D), lambda i, k: (i, 0), pipeline_mode=pl.Buffered(1)),
            pl.BlockSpec((1, D), lambda i, k: (0, 0)),
            pl.BlockSpec((1, 1, D), lambda i, k: (grp(i), 0, 0)),
        ],
        out_specs=pl.BlockSpec((tm, D), lambda i, k: (i, 0)),
        compiler_params=_cparams("parallel", "arbitrary"),
        name="down_resid",
    )(u, w2b, y, gain.reshape(1, D), gate)


def _gather_sorted_kernel(tr_ref, src_ref, h_hbm, o_ref, xs, sem, *, tm, S, nt):
    i = pl.program_id(0)

    def start_tile(t, slot):
        def issue(r2, c):
            for pr in range(2):
                r = 2 * r2 + pr
                s0 = pl.multiple_of(src_ref[t * tm + r] * S, S)
                pltpu.make_async_copy(h_hbm.at[pl.ds(s0, S)], xs.at[slot, pl.ds(pl.multiple_of(r * S, S), S)],
                                      sem.at[slot]).start(priority=pr)
            return c

        lax.fori_loop(0, tm // 2, issue, 0)

    def finish_tile(slot):
        pltpu.make_async_copy(h_hbm.at[pl.ds(0, tm * S)], xs.at[slot], sem.at[slot]).wait()
        for s in range(S):
            o_ref[:, s * LANE:(s + 1) * LANE] = _from_slab(xs.at[slot], tm, s, S).astype(o_ref.dtype)

    @pl.when(jnp.logical_and(i == 0, tr_ref[0] > 0))
    def _():
        start_tile(0, 0)

    nxt = jnp.minimum(i + 1, nt - 1)
    for slot in range(2):
        @pl.when(jnp.logical_and(jnp.logical_and(i + 1 < nt, tr_ref[nxt] > 0), i % 2 == slot))
        def _():
            start_tile(i + 1, 1 - slot)

    @pl.when(tr_ref[i] == 0)
    def _():
        o_ref[...] = jnp.zeros_like(o_ref)

    for slot in range(2):
        @pl.when(jnp.logical_and(tr_ref[i] > 0, i % 2 == slot))
        def _():
            finish_tile(slot)


def _gather_sorted(h_slab, tile_rows, src_tok, *, tm, D):
    S = D // LANE
    nt = tile_rows.shape[0]
    grid_spec = pltpu.PrefetchScalarGridSpec(
        num_scalar_prefetch=2,
        grid=(nt,),
        in_specs=[pl.BlockSpec(memory_space=pl.ANY)],
        out_specs=pl.BlockSpec((tm, D), lambda i, tr, sr: (i, 0)),
        scratch_shapes=[pltpu.VMEM((2, tm * S, LANE), F32), pltpu.SemaphoreType.DMA((2,))],
    )
    return pl.pallas_call(
        functools.partial(_gather_sorted_kernel, tm=tm, S=S, nt=nt),
        out_shape=SDS((nt * tm, D), BF16),
        grid_spec=grid_spec,
        compiler_params=_cparams("arbitrary"),
        name="gather_sorted",
    )(tile_rows, src_tok, h_slab)


def _moe_gate_up_kernel(te_ref, tr_ref, tf_ref, x_ref, w1_ref, w3_ref, u_ref, wb1, wb3):
    i = pl.program_id(1)

    @pl.when(tf_ref[i] > 0)
    def _():
        wb1[...] = w1_ref[...].astype(BF16)
        wb3[...] = w3_ref[...].astype(BF16)

    def gate_up(rows):
        x = x_ref[rows, :]
        u_ref[rows, :] = (_silu(_dot(x, wb1[...])) * _dot(x, wb3[...])).astype(u_ref.dtype)

    nrows = tr_ref[i]
    half = x_ref.shape[0] // 2
    lower, upper = pl.ds(0, half), pl.ds(half, half)

    @pl.when(nrows > half)
    def _():
        gate_up(pl.ds(0, 2 * half))

    @pl.when(jnp.logical_and(nrows > 0, nrows <= half))
    def _():
        gate_up(lower)
        u_ref[upper, :] = jnp.zeros((half, u_ref.shape[1]), u_ref.dtype)

    @pl.when(nrows == 0)
    def _():
        u_ref[...] = jnp.zeros_like(u_ref)


def _moe_gate_up(x_sorted, tile_e, tile_rows, tile_first, w1, w3, layer, *, tm, tn=512):
    R, D = x_sorted.shape
    F = w1.shape[-1]
    tn = _tile(F, tn, LANE)
    wspec = pl.BlockSpec((None, None, D, tn), lambda j, i, te, tr, tf: (layer, te[i], 0, j))
    grid_spec = pltpu.PrefetchScalarGridSpec(
        num_scalar_prefetch=3,
        grid=(F // tn, R // tm),
        in_specs=[pl.BlockSpec((tm, D), lambda j, i, te, tr, tf: (i, 0)), wspec, wspec],
        out_specs=pl.BlockSpec((tm, tn), lambda j, i, te, tr, tf: (i, j)),
        scratch_shapes=[pltpu.VMEM((D, tn), BF16), pltpu.VMEM((D, tn), BF16)],
    )
    return pl.pallas_call(
        _moe_gate_up_kernel,
        out_shape=SDS((R, F), BF16),
        grid_spec=grid_spec,
        compiler_params=_cparams("parallel", "arbitrary"),
        name="moe_gate_up",
    )(tile_e, tile_rows, tile_first, x_sorted, w1, w3)


def _moe_down_kernel(te_ref, tr_ref, dst_ref, u_ref, w_ref, gate_ref, out_hbm, acc, stage, sem,
                     *, nk, nt, tm, S, chunk):
    i = pl.program_id(0)
    k = pl.program_id(1)
    nrows = tr_ref[i]
    base = i * tm

    def row_copy(slot, r, d0):
        return pltpu.make_async_copy(stage.at[slot, pl.ds(pl.multiple_of(r * S, S), S)],
                                     out_hbm.at[pl.ds(d0, S)], sem.at[slot])

    def drain(slot, n):
        for c in range(tm // chunk):
            @pl.when(n >= (c + 1) * chunk)
            def _():
                pltpu.make_async_copy(stage.at[slot, pl.ds(c * chunk * S, chunk * S)],
                                      out_hbm.at[pl.ds(0, chunk * S)], sem.at[slot]).wait()

        def one(r, c):
            row_copy(slot, r, 0).wait()
            return c

        lax.fori_loop((n // chunk) * chunk, n, one, 0)

    @pl.when(nrows > 0)
    def _():
        @pl.when(k == 0)
        def _():
            acc[...] = jnp.zeros_like(acc)

        half = tm // 2

        @pl.when(nrows > half)
        def _():
            acc[...] += _dot(u_ref[...], w_ref[0])

        @pl.when(nrows <= half)
        def _():
            acc[pl.ds(0, half), :] += _dot(u_ref[pl.ds(0, half), :], w_ref[0])

        for slot in range(2):
            @pl.when(jnp.logical_and(k == nk - 1, i % 2 == slot))
            def _():
                _to_slab(stage.at[slot], acc[...] * gate_ref[...])

                def issue_row(r, pr):
                    row_copy(slot, r, pl.multiple_of(dst_ref[base + r] * S, S)).start(priority=pr)

                def issue(r2, c):
                    for pr in range(2):
                        issue_row(2 * r2 + pr, pr)
                    return c

                lax.fori_loop(0, nrows // 2, issue, 0)

                @pl.when(nrows % 2 == 1)
                def _():
                    issue_row(nrows - 1, 0)

                @pl.when(i > 0)
                def _():
                    drain(1 - slot, tr_ref[jnp.maximum(i - 1, 0)])

                @pl.when(jnp.logical_or(i == nt - 1, tr_ref[jnp.minimum(i + 1, nt - 1)] == 0))
                def _():
                    drain(slot, nrows)


def _moe_down(u_sorted, tile_e, tile_rows, dst_row, gate_rows, w2b, *, n_tok, tm, tk=1408):
    R, F = u_sorted.shape
    D = w2b.shape[2]
    S = D // LANE
    tk = _tile(F, tk, LANE)
    nk = F // tk

    def kidx(i, k, tr):
        return jnp.where(tr[i] > 0, k, nk - 1)

    grid_spec = pltpu.PrefetchScalarGridSpec(
        num_scalar_prefetch=3,
        grid=(R // tm, nk),
        in_specs=[
            pl.BlockSpec((tm, tk), lambda i, k, te, tr, ds: (i, kidx(i, k, tr))),
            pl.BlockSpec((1, tk, D), lambda i, k, te, tr, ds: (te[i], kidx(i, k, tr), 0)),
            pl.BlockSpec((tm, 1), lambda i, k, te, tr, ds: (i, 0)),
        ],
        out_specs=pl.BlockSpec(memory_space=pl.ANY),
        scratch_shapes=[pltpu.VMEM((tm, D), F32), pltpu.VMEM((2, tm * S, LANE), F32),
                        pltpu.SemaphoreType.DMA((2,))],
    )
    return pl.pallas_call(
        functools.partial(_moe_down_kernel, nk=nk, nt=R // tm, tm=tm, S=S, chunk=_tile(tm, 64)),
        out_shape=SDS((2 * n_tok * S, LANE), F32),
        grid_spec=grid_spec,
        compiler_params=pltpu.CompilerParams(dimension_semantics=("arbitrary", "arbitrary"),
                                             vmem_limit_bytes=VMEM_LIMIT_BYTES, has_side_effects=True),
        name="moe_down",
    )(tile_e, tile_rows, dst_row, u_sorted, w2b, gate_rows)


def _combine_kernel(a_ref, b_ref, y_ref, g_ref, gate_ref, o_ref, *, tm, S):
    parts = [_from_slab(a_ref, tm, s, S) + _from_slab(b_ref, tm, s, S) for s in range(S)]
    ss = parts[0] * parts[0]
    for t in parts[1:]:
        ss = ss + t * t
    r = lax.rsqrt(jnp.sum(ss, axis=-1, keepdims=True) / (S * LANE) + EPS)
    for s, t in enumerate(parts):
        c = slice(s * LANE, (s + 1) * LANE)
        o_ref[:, c] = y_ref[:, c] + gate_ref[0][:, c] * (t * r * g_ref[:, c])


def _moe_combine(slots, y, gain, gate, *, n_prompt, t_sample, row0=0, n_rows=None, tm=256):
    M, D = y.shape
    S = D // LANE
    n_rows = M if n_rows is None else n_rows
    tm = _row_tile(n_prompt, t_sample, tm)
    nb, b0 = M // tm, row0 // tm
    grp = functools.partial(_group_of, tm=tm, n_prompt=n_prompt, t_sample=t_sample)
    return pl.pallas_call(
        functools.partial(_combine_kernel, tm=tm, S=S),
        out_shape=SDS((n_rows, D), F32),
        grid=(n_rows // tm,),
        in_specs=[
            pl.BlockSpec((tm * S, LANE), lambda i: (b0 + i, 0)),
            pl.BlockSpec((tm * S, LANE), lambda i: (nb + b0 + i, 0)),
            pl.BlockSpec((tm, D), lambda i: (b0 + i, 0)),
            pl.BlockSpec((1, D), lambda i: (0, 0)),
            pl.BlockSpec((1, 1, D), lambda i: (grp(b0 + i), 0, 0)),
        ],
        out_specs=pl.BlockSpec((tm, D), lambda i: (i, 0)),
        compiler_params=_cparams("parallel"),
        name="moe_combine",
    )(slots, slots, y, gain.reshape(1, D), gate)


def _dispatch_tables(sel, gw, tm):
    M = sel.shape[0]
    E = N_EXPERTS
    i32 = jnp.int32
    e_flat = sel.reshape(-1)
    pair = jnp.arange(2 * M, dtype=i32)
    onehot = (e_flat[:, None] == jnp.arange(E, dtype=i32)[None, :]).astype(i32)
    running = jnp.cumsum(onehot, axis=0)
    rank = jnp.sum(running * onehot, axis=1) - 1
    counts = running[-1]
    padded = ((counts + tm - 1) // tm) * tm
    ends_p = jnp.cumsum(padded)
    starts_p = ends_p - padded
    pos = starts_p[e_flat] + rank
    R = 2 * M + E * tm
    payload = jnp.stack([pair // 2, (pair % 2) * M + pair // 2,
                         lax.bitcast_convert_type(gw.reshape(-1), i32)], axis=1)
    table = jnp.zeros((R, 3), i32).at[pos].set(payload)
    src_tok, dst_row = table[:, 0], table[:, 1]
    gate_rows = lax.bitcast_convert_type(table[:, 2], F32)
    tile_start = jnp.arange(R // tm, dtype=i32) * tm
    tile_e = jnp.minimum(jnp.sum(ends_p[None, :] <= tile_start[:, None], axis=1), E - 1).astype(i32)
    tile_rows = jnp.clip(starts_p[tile_e] + counts[tile_e] - tile_start, 0, tm).astype(i32)
    tile_first = jnp.concatenate([jnp.ones((1,), i32), (tile_e[1:] != tile_e[:-1]).astype(i32)])
    return tile_e, tile_rows, tile_first, src_tok, dst_row, gate_rows.reshape(R, 1)


def _attn_kernel(*refs, n_parts, k_modes, v_mode, hb, scale, nk):
    q_refs = refs[:n_parts]
    k_refs = refs[n_parts:2 * n_parts]
    v_ref = refs[2 * n_parts]
    o_ref = refs[2 * n_parts + 1]
    m_sc, l_sc, acc_sc = refs[2 * n_parts + 2:]
    j = pl.program_id(3)
    c_exp = scale * math.log2(math.e)

    @pl.when(j == 0)
    def _():
        m_sc[...] = jnp.full_like(m_sc, -jnp.inf)
        l_sc[...] = jnp.zeros_like(l_sc)
        acc_sc[...] = jnp.zeros_like(acc_sc)

    def block(ref, mode, hh):
        x = ref[:, hh * LANE:(hh + 1) * LANE] if mode == "head" else ref[...]
        return x.astype(BF16)

    for hh in range(hb):
        q = [block(q_refs[p], "head", hh) for p in range(n_parts)]
        k = [block(k_refs[p], k_modes[p], hh) for p in range(n_parts)]
        q = q[0] if n_parts == 1 else jnp.concatenate(q, axis=1)
        k = k[0] if n_parts == 1 else jnp.concatenate(k, axis=1)
        s = _dot_nt(q, k)
        tiles = [s[:, c * LANE:(c + 1) * LANE] for c in range(s.shape[1] // LANE)]
        m_lane = tiles[0]
        for t in tiles[1:]:
            m_lane = jnp.maximum(m_lane, t)
        m_prev = m_sc[hh]
        m_new = jnp.maximum(m_prev, jnp.max(m_lane, axis=-1, keepdims=True))
        alpha = jnp.exp2((m_prev - m_new) * c_exp)
        p = [jnp.exp2((t - m_new) * c_exp) for t in tiles]
        l_lane = p[0]
        for t in p[1:]:
            l_lane = l_lane + t
        l_sc[hh] = alpha * l_sc[hh] + l_lane
        pb = jnp.concatenate([t.astype(BF16) for t in p], axis=1)
        acc_sc[hh] = alpha * acc_sc[hh] + _dot(pb, block(v_ref, v_mode, hh))
        m_sc[hh] = m_new

    @pl.when(j == nk - 1)
    def _():
        for hh in range(hb):
            l = jnp.sum(l_sc[hh], axis=-1, keepdims=True)
            o_ref[:, hh * LANE:(hh + 1) * LANE] = (acc_sc[hh] / l).astype(o_ref.dtype)


def _attention(q_parts, k_parts, v_part, *, B, T, S, H, hb, q_row0, k_row0, tq, tk, scale):
    tq = _tile(T, tq)
    tk = _tile(S, tk)
    nq, nk = T // tq, S // tk
    qb0, kb0 = q_row0 // tq, k_row0 // tk

    def q_spec(off):
        return pl.BlockSpec((tq, hb * LANE), lambda b, g, i, j: (qb0 + b * nq + i, off // hb + g))

    def k_spec(off, mode):
        if mode == "head":
            return pl.BlockSpec((tk, hb * LANE), lambda b, g, i, j: (kb0 + b * nk + j, off // hb + g))
        return pl.BlockSpec((tk, LANE), lambda b, g, i, j: (kb0 + b * nk + j, off + (g if mode == "group" else 0)))

    in_specs = [q_spec(off) for (_, off) in q_parts]
    in_specs += [k_spec(off, mode) for (_, off, mode) in k_parts]
    in_specs += [k_spec(v_part[1], v_part[2])]
    args = [a for (a, _) in q_parts] + [a for (a, _, _) in k_parts] + [v_part[0]]
    return pl.pallas_call(
        functools.partial(_attn_kernel, n_parts=len(q_parts), k_modes=[m for (_, _, m) in k_parts],
                          v_mode=v_part[2], hb=hb, scale=scale, nk=nk),
        out_shape=SDS((B * T, H * LANE), BF16),
        grid=(B, H // hb, nq, nk),
        in_specs=in_specs,
        out_specs=pl.BlockSpec((tq, hb * LANE), lambda b, g, i, j: (b * nq + i, g)),
        scratch_shapes=[pltpu.VMEM((hb, tq, LANE), F32), pltpu.VMEM((hb, tq, LANE), F32),
                        pltpu.VMEM((hb, tq, LANE), F32)],
        compiler_params=_cparams("parallel", "parallel", "parallel", "arbitrary"),
        name="attention",
    )(*args)


def _rmsnorm_cols_kernel(x_ref, g_ref, o_ref):
    o_ref[...] = (_rms(x_ref[...]) * g_ref[...]).astype(o_ref.dtype)


def _rmsnorm_cols(x, gain, *, col_block, width, out_dtype, tm=1024):
    M = x.shape[0]
    tm = _tile(M, tm)
    return pl.pallas_call(
        _rmsnorm_cols_kernel,
        out_shape=SDS((M, width), out_dtype),
        grid=(M // tm,),
        in_specs=[pl.BlockSpec((tm, width), lambda i: (i, col_block)),
                  pl.BlockSpec((1, width), lambda i: (0, 0))],
        out_specs=pl.BlockSpec((tm, width), lambda i: (i, 0)),
        compiler_params=_cparams("parallel"),
        name="rmsnorm_cols",
    )(x, gain.reshape(1, width))


def _rope(x, cos, sin, half):
    lane = lax.broadcasted_iota(jnp.int32, x.shape, 1)
    first = (lane % (2 * half)) < half
    rot = jnp.where(first, -pltpu.roll(x, LANE - half, 1), pltpu.roll(x, half, 1))
    return x * cos + rot * sin


def _head_rope_kernel(x_ref, g_ref, cos_ref, sin_ref, o_ref, *, norm, half, hw):
    cos, sin = cos_ref[...], sin_ref[...]
    for c in range(hw):
        cols = slice(c * LANE, (c + 1) * LANE)
        x = x_ref[:, cols]
        if norm:
            x = _rms(x) * g_ref[c]
        o_ref[:, cols] = _rope(x, cos, sin, half).astype(o_ref.dtype)


def _head_rope(x, gains, cos, sin, *, col_block0, n_heads, norm, half, out_dtype, n_rows=None, tm=1024):
    M = x.shape[0] if n_rows is None else n_rows
    tm = _tile(M, tm)
    hw = math.gcd(math.gcd(n_heads, col_block0) if col_block0 else n_heads, 4)
    return pl.pallas_call(
        functools.partial(_head_rope_kernel, norm=norm, half=half, hw=hw),
        out_shape=SDS((M, n_heads * LANE), out_dtype),
        grid=(M // tm, n_heads // hw),
        in_specs=[
            pl.BlockSpec((tm, hw * LANE), lambda i, h: (i, col_block0 // hw + h)),
            pl.BlockSpec((hw, 1, LANE), lambda i, h: (h, 0, 0)),
            pl.BlockSpec((tm, LANE), lambda i, h: (i, 0)),
            pl.BlockSpec((tm, LANE), lambda i, h: (i, 0)),
        ],
        out_specs=pl.BlockSpec((tm, hw * LANE), lambda i, h: (i, h)),
        compiler_params=_cparams("parallel", "parallel"),
        name="head_rope",
    )(x, gains, cos, sin)


def _rope_tables(n_prompt, dec_batch, t_sample, half):
    tok = jnp.arange(t_sample, dtype=jnp.int32)
    row, col = tok // GRID_W, tok % GRID_W
    inv = ROPE_THETA ** (-jnp.arange(half, dtype=F32) / half)

    def cs(pos):
        ang = pos.astype(F32)[:, None] * inv[None, :]
        return (jnp.concatenate([jnp.cos(ang), jnp.cos(ang)], -1),
                jnp.concatenate([jnp.sin(ang), jnp.sin(ang)], -1))

    cr, sr = cs(row)
    cc, sc = cs(col)
    pad = LANE - 4 * half
    cos = jnp.concatenate([cr, cc, jnp.ones((t_sample, pad), F32)], -1)
    sin = jnp.concatenate([sr, sc, jnp.zeros((t_sample, pad), F32)], -1)
    cos = jnp.concatenate([jnp.ones((n_prompt, LANE), F32), jnp.tile(cos, (dec_batch, 1))], 0)
    sin = jnp.concatenate([jnp.zeros((n_prompt, LANE), F32), jnp.tile(sin, (dec_batch, 1))], 0)
    return cos, sin


def _hgrn_constants(C, reverse):
    L = int(np.log2(C))
    t = np.arange(C)[:, None]
    u = np.arange(C)[None, :]
    Ws, Ms = [], []
    for lvl in range(L):
        b = 1 << lvl
        grp = t // (2 * b)
        if not reverse:
            bnd = grp * 2 * b + b - 1
            qside = (t % (2 * b)) >= b
            W = np.where(qside, (u > bnd) & (u <= t), (u > t) & (u <= bnd))
            kside_s = (u % (2 * b)) < b
        else:
            bnd = grp * 2 * b + b
            qside = (t % (2 * b)) < b
            W = np.where(qside, (u >= t) & (u < bnd), (u >= bnd) & (u < t))
            kside_s = (u % (2 * b)) >= b
        Ws.append(W)
        Ms.append(qside & kside_s & (grp == u // (2 * b)))
    if not reverse:
        Ws += [u <= t, u > t]
    else:
        Ws += [u >= t, u < t]
    Ms.append(t == u)
    W_all = jnp.asarray(np.concatenate(Ws, 0).astype(np.float32), BF16)
    M_all = jnp.asarray(np.stack(Ms).astype(np.float32), F32)
    return W_all, M_all


def _hgrn_decays(z, lb, W_all):
    f = lb + (1.0 - lb) * jax.nn.sigmoid(z)
    lf = jnp.log(f)
    hi = lf.astype(BF16)
    lo = (lf - hi.astype(F32)).astype(BF16)
    w = z.shape[1]
    d2 = _dot(W_all, jnp.concatenate([hi, lo], axis=1))
    return 1.0 - f, d2[:, :w] + d2[:, w:]


def _hgrn_scores(q, k, dn, m_ref, *, C, L):
    e_all = jnp.exp(dn)
    a = m_ref[L] * _dot_nt(q.astype(BF16), k.astype(BF16))
    for lvl in range(L):
        e = e_all[lvl * C:(lvl + 1) * C]
        a = a + m_ref[lvl] * _dot_nt((q * e).astype(BF16), (k * e).astype(BF16))
    eq = e_all[L * C:(L + 1) * C]
    ek = e_all[(L + 1) * C:(L + 2) * C]
    return a, (q * eq).astype(BF16), (k * ek).astype(BF16)


def _hgrn_kernel(*refs, C, L, nc, hb, has_init):
    (qf_ref, vf_ref, zf_ref, qb_ref, vb_ref, zb_ref, lbf_ref, lbb_ref,
     wf_ref, mf_ref, wb_ref, mb_ref) = refs[:12]
    rest = refs[12:]
    if has_init:
        s0_ref, rest = rest[0], rest[1:]
    of_ref, ob_ref, sfin_ref, stf, stb = rest
    c = pl.program_id(2)

    @pl.when(c == 0)
    def _():
        for hh in range(hb):
            if has_init:
                stf[hh] = s0_ref[0, 0, hh]
                stb[hh] = s0_ref[0, 1, hh]
            else:
                stf[hh] = jnp.zeros((LANE, LANE), F32)
                stb[hh] = jnp.zeros((LANE, LANE), F32)

    dirs = [(qf_ref, vf_ref, zf_ref, lbf_ref, wf_ref, mf_ref, stf, of_ref, C - 1),
            (qb_ref, vb_ref, zb_ref, lbb_ref, wb_ref, mb_ref, stb, ob_ref, 0)]
    decays = [_hgrn_decays(z_ref[...], lb_ref[0], w_ref[...]) for (_, _, z_ref, lb_ref, w_ref, _, _, _, _) in dirs]
    chains = []
    for (q_ref, v_ref, _, _, _, m_ref, st, o_ref, edge_row), (k_all, dn_all) in zip(dirs, decays):
        for hh in range(hb):
            cols = slice(hh * LANE, (hh + 1) * LANE)
            chains.append((q_ref[:, cols], v_ref[:, cols], k_all[:, cols], dn_all[:, cols], m_ref, st, o_ref,
                           edge_row, hh, cols))
    scores = [_hgrn_scores(q, k, dn, m_ref, C=C, L=L) for (q, _, k, dn, m_ref, _, _, _, _, _) in chains]
    for (q, v, k, dn, _, st, o_ref, edge_row, hh, cols), (a, qe, ke) in zip(chains, scores):
        vb = v.astype(BF16)
        o_ref[:, cols] = _dot(a.astype(BF16), vb) + _dot_nt(qe, st[hh].astype(BF16))
        g_edge = dn[L * C + edge_row:L * C + edge_row + 1]
        st[hh] = st[hh] * jnp.exp(g_edge) + _dot_tn(vb, ke)

    @pl.when(c == nc - 1)
    def _():
        for hh in range(hb):
            sfin_ref[0, 0, hh] = stf[hh].T
            sfin_ref[0, 1, hh] = stb[hh].T


def _hgrn_scan(proj, lb, s0t, *, B, T, row0, C=128, hb=8):
    D = proj.shape[1] // 5
    H = D // LANE
    C = _tile(T, C)
    L = int(np.log2(C))
    assert 1 << L == C
    nc = T // C
    rb0 = row0 // C
    HB = H // hb
    W = hb * LANE
    wf, mf = _hgrn_constants(C, False)
    wb, mb = _hgrn_constants(C, True)

    def fwd(col0):
        return pl.BlockSpec((C, W), lambda b, g, c: (rb0 + b * nc + c, col0 // hb + g))

    def bwd(col0):
        return pl.BlockSpec((C, W), lambda b, g, c: (rb0 + b * nc + nc - 1 - c, col0 // hb + g))

    def const(a):
        return pl.BlockSpec(a.shape, lambda b, g, c: (0,) * a.ndim)

    in_specs = [fwd(0), fwd(H), fwd(3 * H), bwd(0), bwd(H), bwd(4 * H),
                pl.BlockSpec((1, 1, W), lambda b, g, c: (0, 0, g)),
                pl.BlockSpec((1, 1, W), lambda b, g, c: (1, 0, g)),
                const(wf), const(mf), const(wb), const(mb)]
    args = [proj, proj, proj, proj, proj, proj, lb, lb, wf, mf, wb, mb]
    if s0t is not None:
        in_specs.append(pl.BlockSpec((1, 2, hb, LANE, LANE), lambda b, g, c: (b, 0, g, 0, 0)))
        args.append(s0t)
    return pl.pallas_call(
        functools.partial(_hgrn_kernel, C=C, L=L, nc=nc, hb=hb, has_init=s0t is not None),
        out_shape=(SDS((B * T, D), F32), SDS((B * T, D), F32), SDS((B, 2, H, LANE, LANE), F32)),
        grid=(B, HB, nc),
        in_specs=in_specs,
        out_specs=(
            pl.BlockSpec((C, W), lambda b, g, c: (b * nc + c, g)),
            pl.BlockSpec((C, W), lambda b, g, c: (b * nc + nc - 1 - c, g)),
            pl.BlockSpec((1, 2, hb, LANE, LANE), lambda b, g, c: (b, 0, g, 0, 0)),
        ),
        scratch_shapes=[pltpu.VMEM((hb, LANE, LANE), F32), pltpu.VMEM((hb, LANE, LANE), F32)],
        compiler_params=_cparams("parallel", "parallel", "arbitrary"),
        name="hgrn_scan",
    )(*args)


def _hgrn_gate_kernel(of_ref, ob_ref, g_ref, w_ref, o_ref, *, hw):
    for c in range(hw):
        cols = slice(c * LANE, (c + 1) * LANE)
        o = _rms(of_ref[:, cols] + ob_ref[:, cols]) * w_ref[c]
        o_ref[:, cols] = (o * _silu(g_ref[:, cols])).astype(o_ref.dtype)


def _hgrn_gate(o_f, o_b, proj, o_norm, *, row0, tm=1024, hw=4):
    M, D = o_f.shape
    H = D // LANE
    tm = _tile(math.gcd(M, row0) if row0 else M, tm)
    rb0 = row0 // tm
    blk = pl.BlockSpec((tm, hw * LANE), lambda i, h: (i, h))
    return pl.pallas_call(
        functools.partial(_hgrn_gate_kernel, hw=hw),
        out_shape=SDS((M, D), BF16),
        grid=(M // tm, H // hw),
        in_specs=[blk, blk,
                  pl.BlockSpec((tm, hw * LANE), lambda i, h: (rb0 + i, 2 * H // hw + h)),
                  pl.BlockSpec((hw, 1, LANE), lambda i, h: (h, 0, 0))],
        out_specs=blk,
        compiler_params=_cparams("parallel", "parallel"),
        name="hgrn_gate",
    )(o_f, o_b, proj, o_norm.reshape(H, 1, LANE))


def kernel(x_prompt, x_sample, c, cache_mla_ckv, cache_mla_kpe, state_hgrn, cache_gqa_k, cache_gqa_v, c_ctx, ada_w, ada_b, norm_w, mla_wq_a, mla_q_norm, mla_wq_b, mla_wkv_a, mla_kv_norm, mla_wkv_b, mla_wo, hgrn_w_in, hgrn_lb_logits, hgrn_o_norm, hgrn_wo, gqa_w_qkv, gqa_q_norm, gqa_k_norm, gqa_wo, ffn_w1, ffn_w3, ffn_w2, moe_router, moe_w1, moe_w3, moe_w2):
    Bp, Tp, D = x_prompt.shape
    Bs, Ts, _ = x_sample.shape
    P = cache_mla_ckv.shape[2]
    depth = ada_w.shape[0]
    NP, NS = Bp * Tp, Bs * Ts
    M = NP + NS
    G = 1 + Bs
    assert D == HG_HEADS * LANE
    rows = dict(n_prompt=NP, t_sample=Ts)

    y = jnp.concatenate([x_prompt.reshape(NP, D), x_sample.reshape(NS, D)], axis=0)
    cond8 = jnp.zeros((8, D), F32).at[0].set(c_ctx).at[1:G].set(c)
    mod = _modulation(cond8, ada_w, ada_b).reshape(depth, 8, 6, D)
    mod = jnp.transpose(mod, (0, 2, 1, 3))[:, :, :G, None, :]

    tm_tok = _row_tile(NP, Ts, 512)
    cos_mla, sin_mla = _rope_tables(NP, Bs, Ts, MLA_ROPE // 4)
    cos_gqa, sin_gqa = _rope_tables(NP, Bs, Ts, GQA_HD // 4)
    ones_g = jnp.ones((MLA_HEADS, 1, LANE), F32)

    ckv_list, kpe_list, hg_list, gk_list, gv_list = [], [], [], [], []
    for layer in range(depth):
        kind, j = layer % 3, layer // 3
        nw = norm_w[layer]
        sh1, sc1, g1, sh2, sc2, g2 = (mod[layer, k] for k in range(6))
        h = _adaln_in(y, nw[0], sh1, sc1, tm=tm_tok, **rows)

        if kind == 0:
            Hm = MLA_HEADS
            w_a = jnp.concatenate([mla_wq_a[j], mla_wkv_a[j],
                                   jnp.zeros((D, LANE - MLA_ROPE), F32)], axis=1)
            a = _mm(h, w_a, out_dtype=F32)
            ql, kvl = mla_wq_a.shape[2], mla_kv_norm.shape[1]
            qn = _rmsnorm_cols(a, mla_q_norm[j], col_block=0, width=ql, out_dtype=BF16)
            ckv = _rmsnorm_cols(a, mla_kv_norm[j], col_block=ql // kvl, width=kvl, out_dtype=F32)
            wqb = mla_wq_b[j].reshape(ql, Hm, MLA_NOPE + MLA_ROPE)
            wqb = jnp.concatenate([
                wqb[:, :, :MLA_NOPE].reshape(ql, Hm * MLA_NOPE),
                jnp.pad(wqb[:, :, MLA_NOPE:], ((0, 0), (0, 0), (0, LANE - MLA_ROPE))).reshape(ql, Hm * LANE),
            ], axis=1)
            q = _mm(qn, wqb, out_dtype=BF16, rope=(cos_mla, sin_mla, Hm * MLA_NOPE, MLA_ROPE // 4))
            pe_blk = (ql + kvl) // LANE
            k_pe = _head_rope(a, ones_g, cos_mla, sin_mla, col_block0=pe_blk, n_heads=1, norm=False,
                              half=MLA_ROPE // 4, out_dtype=BF16)
            wkvb = mla_wkv_b[j].reshape(kvl, Hm, MLA_NOPE + MLA_V)
            wkvb = jnp.concatenate([wkvb[:, :, :MLA_NOPE].reshape(kvl, Hm * MLA_NOPE),
                                    wkvb[:, :, MLA_NOPE:].reshape(kvl, Hm * MLA_V)], axis=1)
            ckv_s = jnp.concatenate([ckv[NP:].reshape(Bs, Ts, kvl), cache_mla_ckv[:, j]], axis=1)
            c_all = jnp.concatenate([ckv[:NP], ckv_s.reshape(Bs * (Ts + P), kvl)], axis=0)
            kv = _mm(c_all, wkvb, out_dtype=BF16)
            kpe_ctx = jnp.pad(cache_mla_kpe[:, j], ((0, 0), (0, 0), (0, LANE - MLA_ROPE))).astype(BF16)
            kpe_s = jnp.concatenate([k_pe[NP:].reshape(Bs, Ts, LANE), kpe_ctx], axis=1)
            kpe_all = jnp.concatenate([k_pe[:NP], kpe_s.reshape(Bs * (Ts + P), LANE)], axis=0)
            scale = (MLA_NOPE + MLA_ROPE) ** -0.5
            qp = [(q, 0), (q, Hm)]
            kp = [(kv, 0, "head"), (kpe_all, 0, "shared")]
            vp = (kv, Hm, "head")
            mix_p = _attention(qp, kp, vp, B=Bp, T=Tp, S=Tp, H=Hm, hb=4, q_row0=0, k_row0=0,
                               tq=256, tk=256, scale=scale)
            mix_s = _attention(qp, kp, vp, B=Bs, T=Ts, S=Ts + P, H=Hm, hb=4, q_row0=NP, k_row0=NP,
                               tq=2048, tk=512, scale=scale)
            w_o = mla_wo
            ckv_list.append(ckv[:NP].reshape(Bp, Tp, kvl))
            kpe_list.append(a[:NP, ql + kvl:ql + kvl + MLA_ROPE].reshape(Bp, Tp, MLA_ROPE))
        elif kind == 1:
            H = HG_HEADS
            p = jax.nn.softmax(hgrn_lb_logits.astype(F32), axis=1)
            cum = jnp.cumsum(p, axis=1)
            lb = (cum - cum[:, :1])[:, layer].reshape(2, 1, D)
            proj = _mm(h, hgrn_w_in, layer=j, out_dtype=F32)
            of_p, ob_p, st_p = _hgrn_scan(proj, lb, None, B=Bp, T=Tp, row0=0)
            s0t = jnp.swapaxes(state_hgrn[:, j].astype(F32), -1, -2)
            of_s, ob_s, _ = _hgrn_scan(proj, lb, s0t, B=Bs, T=Ts, row0=NP)
            mix_p = _hgrn_gate(of_p, ob_p, proj, hgrn_o_norm[j], row0=0)
            mix_s = _hgrn_gate(of_s, ob_s, proj, hgrn_o_norm[j], row0=NP)
            w_o = hgrn_wo
            hg_list.append(st_p)
        else:
            Hq, Hk = GQA_HEADS, GQA_KV_HEADS
            qkv = _mm(h, gqa_w_qkv, layer=j, out_dtype=F32)
            gains = jnp.concatenate([jnp.tile(gqa_q_norm[j][None], (Hq, 1)),
                                     jnp.tile(gqa_k_norm[j][None], (Hk, 1))], axis=0).reshape(Hq + Hk, 1, LANE)
            k_plain = _head_rope(qkv, gains[Hq:], cos_gqa, sin_gqa, col_block0=Hq, n_heads=Hk, norm=True,
                                 half=GQA_HD // 4, out_dtype=F32, n_rows=NP)
            qk = _head_rope(qkv, gains, cos_gqa, sin_gqa, col_block0=0, n_heads=Hq + Hk, norm=True,
                            half=GQA_HD // 4, out_dtype=BF16)
            kw = Hk * LANE
            k_new = qk[:, Hq * LANE:]
            v_new = qkv[:, (Hq + Hk) * LANE:].astype(BF16)
            k_s = jnp.concatenate([k_new[NP:].reshape(Bs, Ts, kw),
                                   cache_gqa_k[:, j].reshape(Bs, P, kw).astype(BF16)], axis=1)
            v_s = jnp.concatenate([v_new[NP:].reshape(Bs, Ts, kw),
                                   cache_gqa_v[:, j].reshape(Bs, P, kw).astype(BF16)], axis=1)
            k_all = jnp.concatenate([k_new[:NP], k_s.reshape(Bs * (Ts + P), kw)], axis=0)
            v_all = jnp.concatenate([v_new[:NP], v_s.reshape(Bs * (Ts + P), kw)], axis=0)
            scale = GQA_HD ** -0.5
            qp = [(qk, 0)]
            kp = [(k_all, 0, "group")]
            vp = (v_all, 0, "group")
            mix_p = _attention(qp, kp, vp, B=Bp, T=Tp, S=Tp, H=Hq, hb=Hq // Hk, q_row0=0, k_row0=0,
                               tq=256, tk=256, scale=scale)
            mix_s = _attention(qp, kp, vp, B=Bs, T=Ts, S=Ts + P, H=Hq, hb=Hq // Hk, q_row0=NP, k_row0=NP,
                               tq=2048, tk=512, scale=scale)
            w_o = gqa_wo
            gk_list.append(k_plain[:NP].reshape(Bp, Tp, Hk, GQA_HD))
            gv_list.append(qkv[:NP, (Hq + Hk) * LANE:].reshape(Bp, Tp, Hk, GQA_HD))

        y = _mm_resid(mix_p, mix_s, w_o, y, nw[1], g1, layer=j, **rows)

        fi = layer // 2
        if layer % 2 == 0:
            h2 = _adaln_in(y, nw[2], sh2, sc2, tm=tm_tok, **rows)
            u = _gate_up(h2, ffn_w1, ffn_w3, fi)
            y = _down_resid(u, _cast_bf16(ffn_w2, fi), y, nw[3], g2, **rows)
        else:
            h2, sel, gw = _adaln_in(y, nw[2], sh2, sc2, tm=tm_tok, router=moe_router[fi], **rows)
            tm_e = _tile(M, 1024)
            tile_e, tile_rows, tile_first, src_tok, dst_row, gate_rows = _dispatch_tables(
                sel[:, :2], gw[:, :2], tm_e)
            x_sorted = _gather_sorted(h2, tile_rows, src_tok, tm=tm_e, D=D)
            u = _moe_gate_up(x_sorted, tile_e, tile_rows, tile_first, moe_w1, moe_w3, fi, tm=tm_e)
            slots = _moe_down(u, tile_e, tile_rows, dst_row, gate_rows, _cast_bf16(moe_w2, fi),
                              n_tok=M, tm=tm_e)
            if layer == depth - 1:
                y_p = _moe_combine(slots, y, nw[3], g2, row0=0, n_rows=NP, **rows)
                y_s = _moe_combine(slots, y, nw[3], g2, row0=NP, n_rows=NS, **rows)
            else:
                y = _moe_combine(slots, y, nw[3], g2, **rows)

    if depth % 2:
        y_p, y_s = y[:NP], y[NP:]
    return (y_p.reshape(Bp, Tp, D), y_s.reshape(Bs, Ts, D),
            jnp.stack(ckv_list, axis=1), jnp.stack(kpe_list, axis=1), jnp.stack(hg_list, axis=1),
            jnp.stack(gk_list, axis=1), jnp.stack(gv_list, axis=1))
```

```python
import functools
import math

import numpy as np
import jax
import jax.numpy as jnp
from jax import lax
from jax.experimental import pallas as pl
from jax.experimental.pallas import tpu as pltpu

F32 = jnp.float32
BF16 = jnp.bfloat16
EPS = 1e-6
LANE = 128

GRID_W = 64
ROPE_THETA = 10000.0
MLA_HEADS, MLA_NOPE, MLA_ROPE, MLA_V = 16, 128, 64, 128
HG_HEADS = 16
GQA_HEADS, GQA_KV_HEADS, GQA_HD = 16, 4, 128
N_EXPERTS = 8

VMEM_LIMIT_BYTES = 56 * 1024 * 1024

SDS = jax.ShapeDtypeStruct


def _cparams(*sem):
    return pltpu.CompilerParams(dimension_semantics=sem, vmem_limit_bytes=VMEM_LIMIT_BYTES)


def _tile(n, pref, step=8):
    t = min(pref, n)
    while n % t:
        t -= step
    return t


def _dot(a, b):
    return jnp.dot(a, b, preferred_element_type=F32)


def _dot_nt(a, b):
    return lax.dot_general(a, b, (((1,), (1,)), ((), ())), preferred_element_type=F32)


def _dot_tn(a, b):
    return lax.dot_general(a, b, (((0,), (0,)), ((), ())), preferred_element_type=F32)


def _rms(x):
    return x * lax.rsqrt(jnp.mean(x * x, axis=-1, keepdims=True) + EPS)


def _silu(x):
    return x * jax.nn.sigmoid(x)


def _row_tile(n_prompt, t_sample, pref):
    return _tile(math.gcd(n_prompt, t_sample), pref)


def _group_of(i, tm, n_prompt, t_sample):
    r = i * tm
    return jnp.where(r < n_prompt, 0, 1 + (r - n_prompt) // t_sample)


def _mod_kernel(c_ref, w_ref, b_ref, o_ref):
    s = _silu(c_ref[...]).astype(BF16)
    o_ref[0] = _dot(s, w_ref[0].astype(BF16)) + b_ref[0]


def _modulation(cond8, ada_w, ada_b):
    L, D, N6 = ada_w.shape
    tn = _tile(N6, 1024)
    return pl.pallas_call(
        _mod_kernel,
        out_shape=SDS((L, 8, N6), F32),
        grid=(L, N6 // tn),
        in_specs=[
            pl.BlockSpec((8, D), lambda l, j: (0, 0)),
            pl.BlockSpec((1, D, tn), lambda l, j: (l, 0, j)),
            pl.BlockSpec((1, 1, tn), lambda l, j: (l, 0, j)),
        ],
        out_specs=pl.BlockSpec((1, 8, tn), lambda l, j: (l, 0, j)),
        compiler_params=_cparams("parallel", "parallel"),
        name="modulation",
    )(cond8, ada_w, ada_b.reshape(L, 1, N6))


def _adaln_in_kernel(y_ref, g_ref, sh_ref, sc_ref, h_ref):
    h = (_rms(y_ref[...]) * g_ref[...]) * (1.0 + sc_ref[0]) + sh_ref[0]
    h_ref[...] = h.astype(h_ref.dtype)


def _to_slab(ref, x):
    rows, D = x.shape
    S = D // LANE
    for s in range(S):
        ref[pl.ds(s, rows, stride=S), :] = x[:, s * LANE:(s + 1) * LANE]


def _from_slab(ref, rows, s, S):
    return ref[pl.ds(s, rows, stride=S), :]


def _adaln_route_kernel(y_ref, g_ref, sh_ref, sc_ref, r_ref, h_ref, sel_ref, gw_ref):
    h = (_rms(y_ref[...]) * g_ref[...]) * (1.0 + sc_ref[0]) + sh_ref[0]
    _to_slab(h_ref, h)
    logits = lax.dot_general(h, r_ref[...], (((1,), (0,)), ((), ())),
                             precision=lax.Precision.HIGHEST, preferred_element_type=F32)
    lane = lax.broadcasted_iota(jnp.int32, logits.shape, 1)
    neg = jnp.float32(-jnp.inf)
    logits = jnp.where(lane < N_EXPERTS, logits, neg)
    lane_f = lane.astype(F32)
    m1 = jnp.max(logits, axis=-1, keepdims=True)
    i1 = jnp.min(jnp.where(logits == m1, lane_f, float(LANE)), axis=-1, keepdims=True)
    rest = jnp.where(lane_f == i1, neg, logits)
    m2 = jnp.max(rest, axis=-1, keepdims=True)
    i2 = jnp.min(jnp.where(rest == m2, lane_f, float(LANE)), axis=-1, keepdims=True)
    e = jnp.exp(m2 - m1)
    w1 = 1.0 / (1.0 + e)
    w2 = e / (1.0 + e)
    i1, i2 = i1.astype(jnp.int32), i2.astype(jnp.int32)
    sel_ref[...] = jnp.where(lane == 0, i1, jnp.where(lane == 1, i2, 0))
    gw_ref[...] = jnp.where(lane == 0, w1, jnp.where(lane == 1, w2, 0.0))


def _adaln_in(y, gain, shift, scale, *, n_prompt, t_sample, tm, router=None):
    M, D = y.shape
    grp = functools.partial(_group_of, tm=tm, n_prompt=n_prompt, t_sample=t_sample)
    in_specs = [
        pl.BlockSpec((tm, D), lambda i: (i, 0)),
        pl.BlockSpec((1, D), lambda i: (0, 0)),
        pl.BlockSpec((1, 1, D), lambda i: (grp(i), 0, 0)),
        pl.BlockSpec((1, 1, D), lambda i: (grp(i), 0, 0)),
    ]
    args = [y, gain.reshape(1, D), shift, scale]
    if router is None:
        return pl.pallas_call(
            _adaln_in_kernel,
            out_shape=SDS((M, D), BF16),
            grid=(M // tm,),
            in_specs=in_specs,
            out_specs=pl.BlockSpec((tm, D), lambda i: (i, 0)),
            compiler_params=_cparams("parallel"),
            name="adaln_in",
        )(*args)
    r_pad = jnp.pad(router, ((0, 0), (0, LANE - router.shape[1])))
    return pl.pallas_call(
        _adaln_route_kernel,
        out_shape=(SDS((M * (D // LANE), LANE), F32), SDS((M, LANE), jnp.int32), SDS((M, LANE), F32)),
        grid=(M // tm,),
        in_specs=in_specs + [pl.BlockSpec((D, LANE), lambda i: (0, 0))],
        out_specs=(
            pl.BlockSpec((tm * (D // LANE), LANE), lambda i: (i, 0)),
            pl.BlockSpec((tm, LANE), lambda i: (i, 0)),
            pl.BlockSpec((tm, LANE), lambda i: (i, 0)),
        ),
        compiler_params=_cparams("parallel"),
        name="adaln_route",
    )(*args, r_pad)


def _mm_kernel(x_ref, w_ref, o_ref, wbf):
    @pl.when(pl.program_id(1) == 0)
    def _():
        wbf[...] = w_ref[...].astype(BF16)

    o_ref[...] = _dot(x_ref[...].astype(BF16), wbf[...]).astype(o_ref.dtype)


def _layer_spec(w, layer, block, index_map):
    if layer is None:
        return pl.BlockSpec(block, index_map)
    lead = w.ndim - len(block) - 1
    return pl.BlockSpec((None,) * (lead + 1) + tuple(block),
                        lambda *a: (layer,) + (0,) * lead + tuple(index_map(*a)))


def _mm_rope_kernel(x_ref, w_ref, cos_ref, sin_ref, o_ref, wbf, *, rope_from, half):
    j = pl.program_id(0)

    @pl.when(pl.program_id(1) == 0)
    def _():
        wbf[...] = w_ref[...].astype(BF16)

    acc = _dot(x_ref[...].astype(BF16), wbf[...])

    @pl.when(j < rope_from)
    def _():
        o_ref[...] = acc.astype(o_ref.dtype)

    @pl.when(j >= rope_from)
    def _():
        cos, sin = cos_ref[...], sin_ref[...]
        for c in range(acc.shape[1] // LANE):
            cols = slice(c * LANE, (c + 1) * LANE)
            o_ref[:, cols] = _rope(acc[:, cols], cos, sin, half).astype(o_ref.dtype)


def _mm(x, w, *, out_dtype, layer=None, rope=None, tm=1024, tn=1024):
    M, K = x.shape
    N = w.shape[-1]
    tm = _tile(M, tm)
    tn = _tile(N, tn, LANE)
    in_specs = [
        pl.BlockSpec((tm, K), lambda j, i: (i, 0)),
        _layer_spec(w, layer, (K, tn), lambda j, i: (0, j)),
    ]
    args = [x, w]
    body = _mm_kernel
    if rope is not None:
        cos, sin, first_col, half = rope
        assert first_col % tn == 0
        in_specs += [pl.BlockSpec((tm, LANE), lambda j, i: (i, 0))] * 2
        args += [cos, sin]
        body = functools.partial(_mm_rope_kernel, rope_from=first_col // tn, half=half)
    return pl.pallas_call(
        body,
        out_shape=SDS((M, N), out_dtype),
        grid=(N // tn, M // tm),
        in_specs=in_specs,
        out_specs=pl.BlockSpec((tm, tn), lambda j, i: (i, j)),
        scratch_shapes=[pltpu.VMEM((K, tn), BF16)],
        compiler_params=_cparams("parallel", "arbitrary"),
        name="mm",
    )(*args)


def _mm_resid_kernel(xp_ref, xs_ref, w_ref, y_ref, g_ref, gate_ref, o_ref, wbf, *, n_ptiles):
    i = pl.program_id(0)

    @pl.when(i == 0)
    def _():
        wbf[...] = w_ref[...].astype(BF16)

    x = jnp.where(i < n_ptiles, xp_ref[...], xs_ref[...])
    a = _dot(x.astype(BF16), wbf[...])
    o_ref[...] = y_ref[...] + gate_ref[0] * (_rms(a) * g_ref[...])


def _mm_resid(x_p, x_s, w, y, gain, gate, *, layer, n_prompt, t_sample, tm=512):
    M, D = y.shape
    K = w.shape[-2]
    tm = _row_tile(n_prompt, t_sample, tm)
    n_ptiles = n_prompt // tm
    grp = functools.partial(_group_of, tm=tm, n_prompt=n_prompt, t_sample=t_sample)
    return pl.pallas_call(
        functools.partial(_mm_resid_kernel, n_ptiles=n_ptiles),
        out_shape=SDS((M, D), F32),
        grid=(M // tm,),
        in_specs=[
            pl.BlockSpec((tm, K), lambda i: (jnp.minimum(i, n_ptiles - 1), 0)),
            pl.BlockSpec((tm, K), lambda i: (jnp.maximum(i - n_ptiles, 0), 0)),
            pl.BlockSpec((None, K, D), lambda i: (layer, 0, 0), pipeline_mode=pl.Buffered(1)),
            pl.BlockSpec((tm, D), lambda i: (i, 0)),
            pl.BlockSpec((1, D), lambda i: (0, 0)),
            pl.BlockSpec((1, 1, D), lambda i: (grp(i), 0, 0)),
        ],
        out_specs=pl.BlockSpec((tm, D), lambda i: (i, 0)),
        scratch_shapes=[pltpu.VMEM((K, D), BF16)],
        compiler_params=_cparams("arbitrary"),
        name="mm_resid",
    )(x_p, x_s, w, y, gain.reshape(1, D), gate)


def _cast_kernel(x_ref, o_ref):
    o_ref[...] = x_ref[...].astype(o_ref.dtype)


def _cast_bf16(w, layer, *, tr=1024):
    shape = w.shape[1:]
    R = math.prod(shape[:-1])
    N = shape[-1]
    w2 = w.reshape(w.shape[0] * R, N)
    tr = _tile(R, tr)
    nb = R // tr
    out = pl.pallas_call(
        _cast_kernel,
        out_shape=SDS((R, N), BF16),
        grid=(nb,),
        in_specs=[pl.BlockSpec((tr, N), lambda i: (layer * nb + i, 0))],
        out_specs=pl.BlockSpec((tr, N), lambda i: (i, 0)),
        compiler_params=_cparams("parallel"),
        name="cast_bf16",
    )(w2)
    return out.reshape(shape)


def _gate_up_kernel(x_ref, w1_ref, w3_ref, u_ref, wb1, wb3):
    @pl.when(pl.program_id(1) == 0)
    def _():
        wb1[...] = w1_ref[...].astype(BF16)
        wb3[...] = w3_ref[...].astype(BF16)

    x = x_ref[...]
    u_ref[...] = (_silu(_dot(x, wb1[...])) * _dot(x, wb3[...])).astype(u_ref.dtype)


def _gate_up(x, w1, w3, layer, *, tm=1024, tn=512):
    M, D = x.shape
    F = w1.shape[-1]
    tm = _tile(M, tm)
    tn = _tile(F, tn, LANE)
    wspec = _layer_spec(w1, layer, (D, tn), lambda j, i: (0, j))
    return pl.pallas_call(
        _gate_up_kernel,
        out_shape=SDS((M, F), BF16),
        grid=(F // tn, M // tm),
        in_specs=[pl.BlockSpec((tm, D), lambda j, i: (i, 0)), wspec, wspec],
        out_specs=pl.BlockSpec((tm, tn), lambda j, i: (i, j)),
        scratch_shapes=[pltpu.VMEM((D, tn), BF16), pltpu.VMEM((D, tn), BF16)],
        compiler_params=_cparams("parallel", "arbitrary"),
        name="gate_up",
    )(x, w1, w3)


def _down_resid_kernel(u_ref, w_ref, y_ref, g_ref, gate_ref, o_ref, *, nk):
    k = pl.program_id(1)

    @pl.when(k == 0)
    def _():
        o_ref[...] = jnp.zeros_like(o_ref)

    o_ref[...] += _dot(u_ref[...], w_ref[...])

    @pl.when(k == nk - 1)
    def _():
        o_ref[...] = y_ref[...] + gate_ref[0] * (_rms(o_ref[...]) * g_ref[...])


def _down_resid(u, w2b, y, gain, gate, *, n_prompt, t_sample, tm=1024, tk=1408):
    M, D = y.shape
    F = u.shape[1]
    tm = _row_tile(n_prompt, t_sample, tm)
    tk = _tile(F, tk, LANE)
    nk = F // tk
    grp = functools.partial(_group_of, tm=tm, n_prompt=n_prompt, t_sample=t_sample)
    return pl.pallas_call(
        functools.partial(_down_resid_kernel, nk=nk),
        out_shape=SDS((M, D), F32),
        grid=(M // tm, nk),
        in_specs=[
            pl.BlockSpec((tm, tk), lambda i, k: (i, k)),
            pl.BlockSpec((tk, D), lambda i, k: (k, 0)),
            pl.BlockSpec((tm, D), lambda i, k: (i, 0), pipeline_mode=pl.Buffered(1)),
            pl.BlockSpec((1, D), lambda i, k: (0, 0)),
            pl.BlockSpec((1, 1, D), lambda i, k: (grp(i), 0, 0)),
        ],
        out_specs=pl.BlockSpec((tm, D), lambda i, k: (i, 0)),
        compiler_params=_cparams("parallel", "arbitrary"),
        name="down_resid",
    )(u, w2b, y, gain.reshape(1, D), gate)


def _gather_sorted_kernel(tr_ref, src_ref, h_hbm, o_ref, xs, sem, *, tm, S, nt):
    i = pl.program_id(0)

    def start_tile(t, slot):
        def issue(r2, c):
            for pr in range(2):
                r = 2 * r2 + pr
                s0 = pl.multiple_of(src_ref[t * tm + r] * S, S)
                pltpu.make_async_copy(h_hbm.at[pl.ds(s0, S)], xs.at[slot, pl.ds(pl.multiple_of(r * S, S), S)],
                                      sem.at[slot]).start(priority=pr)
            return c

        lax.fori_loop(0, tm // 2, issue, 0)

    def finish_tile(slot):
        pltpu.make_async_copy(h_hbm.at[pl.ds(0, tm * S)], xs.at[slot], sem.at[slot]).wait()
        for s in range(S):
            o_ref[:, s * LANE:(s + 1) * LANE] = _from_slab(xs.at[slot], tm, s, S).astype(o_ref.dtype)

    @pl.when(jnp.logical_and(i == 0, tr_ref[0] > 0))
    def _():
        start_tile(0, 0)

    nxt = jnp.minimum(i + 1, nt - 1)
    for slot in range(2):
        @pl.when(jnp.logical_and(jnp.logical_and(i + 1 < nt, tr_ref[nxt] > 0), i % 2 == slot))
        def _():
            start_tile(i + 1, 1 - slot)

    @pl.when(tr_ref[i] == 0)
    def _():
        o_ref[...] = jnp.zeros_like(o_ref)

    for slot in range(2):
        @pl.when(jnp.logical_and(tr_ref[i] > 0, i % 2 == slot))
        def _():
            finish_tile(slot)


def _gather_sorted(h_slab, tile_rows, src_tok, *, tm, D):
    S = D // LANE
    nt = tile_rows.shape[0]
    grid_spec = pltpu.PrefetchScalarGridSpec(
        num_scalar_prefetch=2,
        grid=(nt,),
        in_specs=[pl.BlockSpec(memory_space=pl.ANY)],
        out_specs=pl.BlockSpec((tm, D), lambda i, tr, sr: (i, 0)),
        scratch_shapes=[pltpu.VMEM((2, tm * S, LANE), F32), pltpu.SemaphoreType.DMA((2,))],
    )
    return pl.pallas_call(
        functools.partial(_gather_sorted_kernel, tm=tm, S=S, nt=nt),
        out_shape=SDS((nt * tm, D), BF16),
        grid_spec=grid_spec,
        compiler_params=_cparams("arbitrary"),
        name="gather_sorted",
    )(tile_rows, src_tok, h_slab)


def _moe_gate_up_kernel(te_ref, tr_ref, tf_ref, x_ref, w1_ref, w3_ref, u_ref, wb1, wb3):
    i = pl.program_id(1)

    @pl.when(tf_ref[i] > 0)
    def _():
        wb1[...] = w1_ref[...].astype(BF16)
        wb3[...] = w3_ref[...].astype(BF16)

    def gate_up(rows):
        x = x_ref[rows, :]
        u_ref[rows, :] = (_silu(_dot(x, wb1[...])) * _dot(x, wb3[...])).astype(u_ref.dtype)

    nrows = tr_ref[i]
    half = x_ref.shape[0] // 2
    lower, upper = pl.ds(0, half), pl.ds(half, half)

    @pl.when(nrows > half)
    def _():
        gate_up(pl.ds(0, 2 * half))

    @pl.when(jnp.logical_and(nrows > 0, nrows <= half))
    def _():
        gate_up(lower)
        u_ref[upper, :] = jnp.zeros((half, u_ref.shape[1]), u_ref.dtype)

    @pl.when(nrows == 0)
    def _():
        u_ref[...] = jnp.zeros_like(u_ref)


def _moe_gate_up(x_sorted, tile_e, tile_rows, tile_first, w1, w3, layer, *, tm, tn=512):
    R, D = x_sorted.shape
    F = w1.shape[-1]
    tn = _tile(F, tn, LANE)
    wspec = pl.BlockSpec((None, None, D, tn), lambda j, i, te, tr, tf: (layer, te[i], 0, j))
    grid_spec = pltpu.PrefetchScalarGridSpec(
        num_scalar_prefetch=3,
        grid=(F // tn, R // tm),
        in_specs=[pl.BlockSpec((tm, D), lambda j, i, te, tr, tf: (i, 0)), wspec, wspec],
        out_specs=pl.BlockSpec((tm, tn), lambda j, i, te, tr, tf: (i, j)),
        scratch_shapes=[pltpu.VMEM((D, tn), BF16), pltpu.VMEM((D, tn), BF16)],
    )
    return pl.pallas_call(
        _moe_gate_up_kernel,
        out_shape=SDS((R, F), BF16),
        grid_spec=grid_spec,
        compiler_params=_cparams("parallel", "arbitrary"),
        name="moe_gate_up",
    )(tile_e, tile_rows, tile_first, x_sorted, w1, w3)


def _moe_down_kernel(te_ref, tr_ref, dst_ref, u_ref, w_ref, gate_ref, out_hbm, acc, stage, sem,
                     *, nk, nt, tm, S, chunk):
    i = pl.program_id(0)
    k = pl.program_id(1)
    nrows = tr_ref[i]
    base = i * tm

    def row_copy(slot, r, d0):
        return pltpu.make_async_copy(stage.at[slot, pl.ds(pl.multiple_of(r * S, S), S)],
                                     out_hbm.at[pl.ds(d0, S)], sem.at[slot])

    def drain(slot, n):
        for c in range(tm // chunk):
            @pl.when(n >= (c + 1) * chunk)
            def _():
                pltpu.make_async_copy(stage.at[slot, pl.ds(c * chunk * S, chunk * S)],
                                      out_hbm.at[pl.ds(0, chunk * S)], sem.at[slot]).wait()

        def one(r, c):
            row_copy(slot, r, 0).wait()
            return c

        lax.fori_loop((n // chunk) * chunk, n, one, 0)

    @pl.when(nrows > 0)
    def _():
        @pl.when(k == 0)
        def _():
            acc[...] = jnp.zeros_like(acc)

        half = tm // 2

        @pl.when(nrows > half)
        def _():
            acc[...] += _dot(u_ref[...], w_ref[0])

        @pl.when(nrows <= half)
        def _():
            acc[pl.ds(0, half), :] += _dot(u_ref[pl.ds(0, half), :], w_ref[0])

        for slot in range(2):
            @pl.when(jnp.logical_and(k == nk - 1, i % 2 == slot))
            def _():
                _to_slab(stage.at[slot], acc[...] * gate_ref[...])

                def issue_row(r, pr):
                    row_copy(slot, r, pl.multiple_of(dst_ref[base + r] * S, S)).start(priority=pr)

                def issue(r2, c):
                    for pr in range(2):
                        issue_row(2 * r2 + pr, pr)
                    return c

                lax.fori_loop(0, nrows // 2, issue, 0)

                @pl.when(nrows % 2 == 1)
                def _():
                    issue_row(nrows - 1, 0)

                @pl.when(i > 0)
                def _():
                    drain(1 - slot, tr_ref[jnp.maximum(i - 1, 0)])

                @pl.when(jnp.logical_or(i == nt - 1, tr_ref[jnp.minimum(i + 1, nt - 1)] == 0))
                def _():
                    drain(slot, nrows)


def _moe_down(u_sorted, tile_e, tile_rows, dst_row, gate_rows, w2b, *, n_tok, tm, tk=1408):
    R, F = u_sorted.shape
    D = w2b.shape[2]
    S = D // LANE
    tk = _tile(F, tk, LANE)
    nk = F // tk

    def kidx(i, k, tr):
        return jnp.where(tr[i] > 0, k, nk - 1)

    grid_spec = pltpu.PrefetchScalarGridSpec(
        num_scalar_prefetch=3,
        grid=(R // tm, nk),
        in_specs=[
            pl.BlockSpec((tm, tk), lambda i, k, te, tr, ds: (i, kidx(i, k, tr))),
            pl.BlockSpec((1, tk, D), lambda i, k, te, tr, ds: (te[i], kidx(i, k, tr), 0)),
            pl.BlockSpec((tm, 1), lambda i, k, te, tr, ds: (i, 0)),
        ],
        out_specs=pl.BlockSpec(memory_space=pl.ANY),
        scratch_shapes=[pltpu.VMEM((tm, D), F32), pltpu.VMEM((2, tm * S, LANE), F32),
                        pltpu.SemaphoreType.DMA((2,))],
    )
    return pl.pallas_call(
        functools.partial(_moe_down_kernel, nk=nk, nt=R // tm, tm=tm, S=S, chunk=_tile(tm, 64)),
        out_shape=SDS((2 * n_tok * S, LANE), F32),
        grid_spec=grid_spec,
        compiler_params=pltpu.CompilerParams(dimension_semantics=("arbitrary", "arbitrary"),
                                             vmem_limit_bytes=VMEM_LIMIT_BYTES, has_side_effects=True),
        name="moe_down",
    )(tile_e, tile_rows, dst_row, u_sorted, w2b, gate_rows)


def _combine_kernel(a_ref, b_ref, y_ref, g_ref, gate_ref, o_ref, *, tm, S):
    parts = [_from_slab(a_ref, tm, s, S) + _from_slab(b_ref, tm, s, S) for s in range(S)]
    ss = parts[0] * parts[0]
    for t in parts[1:]:
        ss = ss + t * t
    r = lax.rsqrt(jnp.sum(ss, axis=-1, keepdims=True) / (S * LANE) + EPS)
    for s, t in enumerate(parts):
        c = slice(s * LANE, (s + 1) * LANE)
        o_ref[:, c] = y_ref[:, c] + gate_ref[0][:, c] * (t * r * g_ref[:, c])


def _moe_combine(slots, y, gain, gate, *, n_prompt, t_sample, row0=0, n_rows=None, tm=512):
    M, D = y.shape
    S = D // LANE
    n_rows = M if n_rows is None else n_rows
    tm = _row_tile(n_prompt, t_sample, tm)
    nb, b0 = M // tm, row0 // tm
    grp = functools.partial(_group_of, tm=tm, n_prompt=n_prompt, t_sample=t_sample)
    return pl.pallas_call(
        functools.partial(_combine_kernel, tm=tm, S=S),
        out_shape=SDS((n_rows, D), F32),
        grid=(n_rows // tm,),
        in_specs=[
            pl.BlockSpec((tm * S, LANE), lambda i: (b0 + i, 0)),
            pl.BlockSpec((tm * S, LANE), lambda i: (nb + b0 + i, 0)),
            pl.BlockSpec((tm, D), lambda i: (b0 + i, 0)),
            pl.BlockSpec((1, D), lambda i: (0, 0)),
            pl.BlockSpec((1, 1, D), lambda i: (grp(b0 + i), 0, 0)),
        ],
        out_specs=pl.BlockSpec((tm, D), lambda i: (i, 0)),
        compiler_params=_cparams("parallel"),
        name="moe_combine",
    )(slots, slots, y, gain.reshape(1, D), gate)


def _dispatch_tables(sel, gw, tm):
    M = sel.shape[0]
    E = N_EXPERTS
    i32 = jnp.int32
    e_flat = sel.reshape(-1)
    pair = jnp.arange(2 * M, dtype=i32)
    onehot = (e_flat[:, None] == jnp.arange(E, dtype=i32)[None, :]).astype(i32)
    running = jnp.cumsum(onehot, axis=0)
    rank = jnp.sum(running * onehot, axis=1) - 1
    counts = running[-1]
    padded = ((counts + tm - 1) // tm) * tm
    ends_p = jnp.cumsum(padded)
    starts_p = ends_p - padded
    pos = starts_p[e_flat] + rank
    R = 2 * M + E * tm
    payload = jnp.stack([pair // 2, (pair % 2) * M + pair // 2,
                         lax.bitcast_convert_type(gw.reshape(-1), i32)], axis=1)
    table = jnp.zeros((R, 3), i32).at[pos].set(payload)
    src_tok, dst_row = table[:, 0], table[:, 1]
    gate_rows = lax.bitcast_convert_type(table[:, 2], F32)
    tile_start = jnp.arange(R // tm, dtype=i32) * tm
    tile_e = jnp.minimum(jnp.sum(ends_p[None, :] <= tile_start[:, None], axis=1), E - 1).astype(i32)
    tile_rows = jnp.clip(starts_p[tile_e] + counts[tile_e] - tile_start, 0, tm).astype(i32)
    tile_first = jnp.concatenate([jnp.ones((1,), i32), (tile_e[1:] != tile_e[:-1]).astype(i32)])
    return tile_e, tile_rows, tile_first, src_tok, dst_row, gate_rows.reshape(R, 1)


def _attn_kernel(*refs, n_parts, k_modes, v_mode, hb, scale, nk):
    q_refs = refs[:n_parts]
    k_refs = refs[n_parts:2 * n_parts]
    v_ref = refs[2 * n_parts]
    o_ref = refs[2 * n_parts + 1]
    m_sc, l_sc, acc_sc = refs[2 * n_parts + 2:]
    j = pl.program_id(3)
    c_exp = scale * math.log2(math.e)

    @pl.when(j == 0)
    def _():
        m_sc[...] = jnp.full_like(m_sc, -jnp.inf)
        l_sc[...] = jnp.zeros_like(l_sc)
        acc_sc[...] = jnp.zeros_like(acc_sc)

    def block(ref, mode, hh):
        x = ref[:, hh * LANE:(hh + 1) * LANE] if mode == "head" else ref[...]
        return x.astype(BF16)

    for hh in range(hb):
        q = [block(q_refs[p], "head", hh) for p in range(n_parts)]
        k = [block(k_refs[p], k_modes[p], hh) for p in range(n_parts)]
        q = q[0] if n_parts == 1 else jnp.concatenate(q, axis=1)
        k = k[0] if n_parts == 1 else jnp.concatenate(k, axis=1)
        s = _dot_nt(q, k)
        tiles = [s[:, c * LANE:(c + 1) * LANE] for c in range(s.shape[1] // LANE)]
        m_lane = tiles[0]
        for t in tiles[1:]:
            m_lane = jnp.maximum(m_lane, t)
        m_prev = m_sc[hh]
        m_new = jnp.maximum(m_prev, jnp.max(m_lane, axis=-1, keepdims=True))
        alpha = jnp.exp2((m_prev - m_new) * c_exp)
        p = [jnp.exp2((t - m_new) * c_exp) for t in tiles]
        l_lane = p[0]
        for t in p[1:]:
            l_lane = l_lane + t
        l_sc[hh] = alpha * l_sc[hh] + l_lane
        pb = jnp.concatenate([t.astype(BF16) for t in p], axis=1)
        acc_sc[hh] = alpha * acc_sc[hh] + _dot(pb, block(v_ref, v_mode, hh))
        m_sc[hh] = m_new

    @pl.when(j == nk - 1)
    def _():
        for hh in range(hb):
            l = jnp.sum(l_sc[hh], axis=-1, keepdims=True)
            o_ref[:, hh * LANE:(hh + 1) * LANE] = (acc_sc[hh] / l).astype(o_ref.dtype)


def _attention(q_parts, k_parts, v_part, *, B, T, S, H, hb, q_row0, k_row0, tq, tk, scale):
    tq = _tile(T, tq)
    tk = _tile(S, tk)
    nq, nk = T // tq, S // tk
    qb0, kb0 = q_row0 // tq, k_row0 // tk

    def q_spec(off):
        return pl.BlockSpec((tq, hb * LANE), lambda b, g, i, j: (qb0 + b * nq + i, off // hb + g))

    def k_spec(off, mode):
        if mode == "head":
            return pl.BlockSpec((tk, hb * LANE), lambda b, g, i, j: (kb0 + b * nk + j, off // hb + g))
        return pl.BlockSpec((tk, LANE), lambda b, g, i, j: (kb0 + b * nk + j, off + (g if mode == "group" else 0)))

    in_specs = [q_spec(off) for (_, off) in q_parts]
    in_specs += [k_spec(off, mode) for (_, off, mode) in k_parts]
    in_specs += [k_spec(v_part[1], v_part[2])]
    args = [a for (a, _) in q_parts] + [a for (a, _, _) in k_parts] + [v_part[0]]
    return pl.pallas_call(
        functools.partial(_attn_kernel, n_parts=len(q_parts), k_modes=[m for (_, _, m) in k_parts],
                          v_mode=v_part[2], hb=hb, scale=scale, nk=nk),
        out_shape=SDS((B * T, H * LANE), BF16),
        grid=(B, H // hb, nq, nk),
        in_specs=in_specs,
        out_specs=pl.BlockSpec((tq, hb * LANE), lambda b, g, i, j: (b * nq + i, g)),
        scratch_shapes=[pltpu.VMEM((hb, tq, LANE), F32), pltpu.VMEM((hb, tq, LANE), F32),
                        pltpu.VMEM((hb, tq, LANE), F32)],
        compiler_params=_cparams("parallel", "parallel", "parallel", "arbitrary"),
        name="attention",
    )(*args)


def _rmsnorm_cols_kernel(x_ref, g_ref, o_ref):
    o_ref[...] = (_rms(x_ref[...]) * g_ref[...]).astype(o_ref.dtype)


def _rmsnorm_cols(x, gain, *, col_block, width, out_dtype, tm=1024):
    M = x.shape[0]
    tm = _tile(M, tm)
    return pl.pallas_call(
        _rmsnorm_cols_kernel,
        out_shape=SDS((M, width), out_dtype),
        grid=(M // tm,),
        in_specs=[pl.BlockSpec((tm, width), lambda i: (i, col_block)),
                  pl.BlockSpec((1, width), lambda i: (0, 0))],
        out_specs=pl.BlockSpec((tm, width), lambda i: (i, 0)),
        compiler_params=_cparams("parallel"),
        name="rmsnorm_cols",
    )(x, gain.reshape(1, width))


def _rope(x, cos, sin, half):
    lane = lax.broadcasted_iota(jnp.int32, x.shape, 1)
    first = (lane % (2 * half)) < half
    rot = jnp.where(first, -pltpu.roll(x, LANE - half, 1), pltpu.roll(x, half, 1))
    return x * cos + rot * sin


def _head_rope_kernel(x_ref, g_ref, cos_ref, sin_ref, o_ref, *, norm, half, hw):
    cos, sin = cos_ref[...], sin_ref[...]
    for c in range(hw):
        cols = slice(c * LANE, (c + 1) * LANE)
        x = x_ref[:, cols]
        if norm:
            x = _rms(x) * g_ref[c]
        o_ref[:, cols] = _rope(x, cos, sin, half).astype(o_ref.dtype)


def _head_rope(x, gains, cos, sin, *, col_block0, n_heads, norm, half, out_dtype, n_rows=None, tm=1024):
    M = x.shape[0] if n_rows is None else n_rows
    tm = _tile(M, tm)
    hw = math.gcd(math.gcd(n_heads, col_block0) if col_block0 else n_heads, 4)
    return pl.pallas_call(
        functools.partial(_head_rope_kernel, norm=norm, half=half, hw=hw),
        out_shape=SDS((M, n_heads * LANE), out_dtype),
        grid=(M // tm, n_heads // hw),
        in_specs=[
            pl.BlockSpec((tm, hw * LANE), lambda i, h: (i, col_block0 // hw + h)),
            pl.BlockSpec((hw, 1, LANE), lambda i, h: (h, 0, 0)),
            pl.BlockSpec((tm, LANE), lambda i, h: (i, 0)),
            pl.BlockSpec((tm, LANE), lambda i, h: (i, 0)),
        ],
        out_specs=pl.BlockSpec((tm, hw * LANE), lambda i, h: (i, h)),
        compiler_params=_cparams("parallel", "parallel"),
        name="head_rope",
    )(x, gains, cos, sin)


def _rope_tables(n_prompt, dec_batch, t_sample, half):
    tok = jnp.arange(t_sample, dtype=jnp.int32)
    row, col = tok // GRID_W, tok % GRID_W
    inv = ROPE_THETA ** (-jnp.arange(half, dtype=F32) / half)

    def cs(pos):
        ang = pos.astype(F32)[:, None] * inv[None, :]
        return (jnp.concatenate([jnp.cos(ang), jnp.cos(ang)], -1),
                jnp.concatenate([jnp.sin(ang), jnp.sin(ang)], -1))

    cr, sr = cs(row)
    cc, sc = cs(col)
    pad = LANE - 4 * half
    cos = jnp.concatenate([cr, cc, jnp.ones((t_sample, pad), F32)], -1)
    sin = jnp.concatenate([sr, sc, jnp.zeros((t_sample, pad), F32)], -1)
    cos = jnp.concatenate([jnp.ones((n_prompt, LANE), F32), jnp.tile(cos, (dec_batch, 1))], 0)
    sin = jnp.concatenate([jnp.zeros((n_prompt, LANE), F32), jnp.tile(sin, (dec_batch, 1))], 0)
    return cos, sin


def _hgrn_constants(C, reverse):
    L = int(np.log2(C))
    t = np.arange(C)[:, None]
    u = np.arange(C)[None, :]
    Ws, Ms = [], []
    for lvl in range(L):
        b = 1 << lvl
        grp = t // (2 * b)
        if not reverse:
            bnd = grp * 2 * b + b - 1
            qside = (t % (2 * b)) >= b
            W = np.where(qside, (u > bnd) & (u <= t), (u > t) & (u <= bnd))
            kside_s = (u % (2 * b)) < b
        else:
            bnd = grp * 2 * b + b
            qside = (t % (2 * b)) < b
            W = np.where(qside, (u >= t) & (u < bnd), (u >= bnd) & (u < t))
            kside_s = (u % (2 * b)) >= b
        Ws.append(W)
        Ms.append(qside & kside_s & (grp == u // (2 * b)))
    if not reverse:
        Ws += [u <= t, u > t]
    else:
        Ws += [u >= t, u < t]
    Ms.append(t == u)
    W_all = jnp.asarray(np.concatenate(Ws, 0).astype(np.float32), BF16)
    M_all = jnp.asarray(np.stack(Ms).astype(np.float32), F32)
    return W_all, M_all


def _hgrn_decays(z, lb, W_all):
    f = lb + (1.0 - lb) * jax.nn.sigmoid(z)
    lf = jnp.log(f)
    hi = lf.astype(BF16)
    lo = (lf - hi.astype(F32)).astype(BF16)
    w = z.shape[1]
    d2 = _dot(W_all, jnp.concatenate([hi, lo], axis=1))
    return 1.0 - f, d2[:, :w] + d2[:, w:]


def _hgrn_scores(q, k, dn, m_ref, *, C, L):
    e_all = jnp.exp(dn)
    a = m_ref[L] * _dot_nt(q.astype(BF16), k.astype(BF16))
    for lvl in range(L):
        e = e_all[lvl * C:(lvl + 1) * C]
        a = a + m_ref[lvl] * _dot_nt((q * e).astype(BF16), (k * e).astype(BF16))
    eq = e_all[L * C:(L + 1) * C]
    ek = e_all[(L + 1) * C:(L + 2) * C]
    return a, (q * eq).astype(BF16), (k * ek).astype(BF16)


def _hgrn_kernel(*refs, C, L, nc, hb, has_init):
    (qf_ref, vf_ref, zf_ref, qb_ref, vb_ref, zb_ref, lbf_ref, lbb_ref,
     wf_ref, mf_ref, wb_ref, mb_ref) = refs[:12]
    rest = refs[12:]
    if has_init:
        s0_ref, rest = rest[0], rest[1:]
    of_ref, ob_ref, sfin_ref, stf, stb = rest
    c = pl.program_id(2)

    @pl.when(c == 0)
    def _():
        for hh in range(hb):
            if has_init:
                stf[hh] = s0_ref[0, 0, hh]
                stb[hh] = s0_ref[0, 1, hh]
            else:
                stf[hh] = jnp.zeros((LANE, LANE), F32)
                stb[hh] = jnp.zeros((LANE, LANE), F32)

    dirs = [(qf_ref, vf_ref, zf_ref, lbf_ref, wf_ref, mf_ref, stf, of_ref, C - 1),
            (qb_ref, vb_ref, zb_ref, lbb_ref, wb_ref, mb_ref, stb, ob_ref, 0)]
    decays = [_hgrn_decays(z_ref[...], lb_ref[0], w_ref[...]) for (_, _, z_ref, lb_ref, w_ref, _, _, _, _) in dirs]
    chains = []
    for (q_ref, v_ref, _, _, _, m_ref, st, o_ref, edge_row), (k_all, dn_all) in zip(dirs, decays):
        for hh in range(hb):
            cols = slice(hh * LANE, (hh + 1) * LANE)
            chains.append((q_ref[:, cols], v_ref[:, cols], k_all[:, cols], dn_all[:, cols], m_ref, st, o_ref,
                           edge_row, hh, cols))
    scores = [_hgrn_scores(q, k, dn, m_ref, C=C, L=L) for (q, _, k, dn, m_ref, _, _, _, _, _) in chains]
    for (q, v, k, dn, _, st, o_ref, edge_row, hh, cols), (a, qe, ke) in zip(chains, scores):
        vb = v.astype(BF16)
        o_ref[:, cols] = _dot(a.astype(BF16), vb) + _dot_nt(qe, st[hh].astype(BF16))
        g_edge = dn[L * C + edge_row:L * C + edge_row + 1]
        st[hh] = st[hh] * jnp.exp(g_edge) + _dot_tn(vb, ke)

    @pl.when(c == nc - 1)
    def _():
        for hh in range(hb):
            sfin_ref[0, 0, hh] = stf[hh].T
            sfin_ref[0, 1, hh] = stb[hh].T


def _hgrn_scan(proj, lb, s0t, *, B, T, row0, C=128, hb=8):
    D = proj.shape[1] // 5
    H = D // LANE
    C = _tile(T, C)
    L = int(np.log2(C))
    assert 1 << L == C
    nc = T // C
    rb0 = row0 // C
    HB = H // hb
    W = hb * LANE
    wf, mf = _hgrn_constants(C, False)
    wb, mb = _hgrn_constants(C, True)

    def fwd(col0):
        return pl.BlockSpec((C, W), lambda b, g, c: (rb0 + b * nc + c, col0 // hb + g))

    def bwd(col0):
        return pl.BlockSpec((C, W), lambda b, g, c: (rb0 + b * nc + nc - 1 - c, col0 // hb + g))

    def const(a):
        return pl.BlockSpec(a.shape, lambda b, g, c: (0,) * a.ndim)

    in_specs = [fwd(0), fwd(H), fwd(3 * H), bwd(0), bwd(H), bwd(4 * H),
                pl.BlockSpec((1, 1, W), lambda b, g, c: (0, 0, g)),
                pl.BlockSpec((1, 1, W), lambda b, g, c: (1, 0, g)),
                const(wf), const(mf), const(wb), const(mb)]
    args = [proj, proj, proj, proj, proj, proj, lb, lb, wf, mf, wb, mb]
    if s0t is not None:
        in_specs.append(pl.BlockSpec((1, 2, hb, LANE, LANE), lambda b, g, c: (b, 0, g, 0, 0)))
        args.append(s0t)
    return pl.pallas_call(
        functools.partial(_hgrn_kernel, C=C, L=L, nc=nc, hb=hb, has_init=s0t is not None),
        out_shape=(SDS((B * T, D), F32), SDS((B * T, D), F32), SDS((B, 2, H, LANE, LANE), F32)),
        grid=(B, HB, nc),
        in_specs=in_specs,
        out_specs=(
            pl.BlockSpec((C, W), lambda b, g, c: (b * nc + c, g)),
            pl.BlockSpec((C, W), lambda b, g, c: (b * nc + nc - 1 - c, g)),
            pl.BlockSpec((1, 2, hb, LANE, LANE), lambda b, g, c: (b, 0, g, 0, 0)),
        ),
        scratch_shapes=[pltpu.VMEM((hb, LANE, LANE), F32), pltpu.VMEM((hb, LANE, LANE), F32)],
        compiler_params=_cparams("parallel", "parallel", "arbitrary"),
        name="hgrn_scan",
    )(*args)


def _hgrn_gate_kernel(of_ref, ob_ref, g_ref, w_ref, o_ref, *, hw):
    for c in range(hw):
        cols = slice(c * LANE, (c + 1) * LANE)
        o = _rms(of_ref[:, cols] + ob_ref[:, cols]) * w_ref[c]
        o_ref[:, cols] = (o * _silu(g_ref[:, cols])).astype(o_ref.dtype)


def _hgrn_gate(o_f, o_b, proj, o_norm, *, row0, tm=1024, hw=4):
    M, D = o_f.shape
    H = D // LANE
    tm = _tile(math.gcd(M, row0) if row0 else M, tm)
    rb0 = row0 // tm
    blk = pl.BlockSpec((tm, hw * LANE), lambda i, h: (i, h))
    return pl.pallas_call(
        functools.partial(_hgrn_gate_kernel, hw=hw),
        out_shape=SDS((M, D), BF16),
        grid=(M // tm, H // hw),
        in_specs=[blk, blk,
                  pl.BlockSpec((tm, hw * LANE), lambda i, h: (rb0 + i, 2 * H // hw + h)),
                  pl.BlockSpec((hw, 1, LANE), lambda i, h: (h, 0, 0))],
        out_specs=blk,
        compiler_params=_cparams("parallel", "parallel"),
        name="hgrn_gate",
    )(o_f, o_b, proj, o_norm.reshape(H, 1, LANE))


def kernel(x_prompt, x_sample, c, cache_mla_ckv, cache_mla_kpe, state_hgrn, cache_gqa_k, cache_gqa_v, c_ctx, ada_w, ada_b, norm_w, mla_wq_a, mla_q_norm, mla_wq_b, mla_wkv_a, mla_kv_norm, mla_wkv_b, mla_wo, hgrn_w_in, hgrn_lb_logits, hgrn_o_norm, hgrn_wo, gqa_w_qkv, gqa_q_norm, gqa_k_norm, gqa_wo, ffn_w1, ffn_w3, ffn_w2, moe_router, moe_w1, moe_w3, moe_w2):
    Bp, Tp, D = x_prompt.shape
    Bs, Ts, _ = x_sample.shape
    P = cache_mla_ckv.shape[2]
    depth = ada_w.shape[0]
    NP, NS = Bp * Tp, Bs * Ts
    M = NP + NS
    G = 1 + Bs
    assert D == HG_HEADS * LANE
    rows = dict(n_prompt=NP, t_sample=Ts)

    y = jnp.concatenate([x_prompt.reshape(NP, D), x_sample.reshape(NS, D)], axis=0)
    cond8 = jnp.zeros((8, D), F32).at[0].set(c_ctx).at[1:G].set(c)
    mod = _modulation(cond8, ada_w, ada_b).reshape(depth, 8, 6, D)
    mod = jnp.transpose(mod, (0, 2, 1, 3))[:, :, :G, None, :]

    tm_tok = _row_tile(NP, Ts, 1024)
    cos_mla, sin_mla = _rope_tables(NP, Bs, Ts, MLA_ROPE // 4)
    cos_gqa, sin_gqa = _rope_tables(NP, Bs, Ts, GQA_HD // 4)
    ones_g = jnp.ones((MLA_HEADS, 1, LANE), F32)

    ckv_list, kpe_list, hg_list, gk_list, gv_list = [], [], [], [], []
    for layer in range(depth):
        kind, j = layer % 3, layer // 3
        nw = norm_w[layer]
        sh1, sc1, g1, sh2, sc2, g2 = (mod[layer, k] for k in range(6))
        h = _adaln_in(y, nw[0], sh1, sc1, tm=tm_tok, **rows)

        if kind == 0:
            Hm = MLA_HEADS
            w_a = jnp.concatenate([mla_wq_a[j], mla_wkv_a[j],
                                   jnp.zeros((D, LANE - MLA_ROPE), F32)], axis=1)
            a = _mm(h, w_a, out_dtype=F32, tn=w_a.shape[1])
            ql, kvl = mla_wq_a.shape[2], mla_kv_norm.shape[1]
            qn = _rmsnorm_cols(a, mla_q_norm[j], col_block=0, width=ql, out_dtype=BF16)
            ckv = _rmsnorm_cols(a, mla_kv_norm[j], col_block=ql // kvl, width=kvl, out_dtype=F32)
            wqb = mla_wq_b[j].reshape(ql, Hm, MLA_NOPE + MLA_ROPE)
            wqb = jnp.concatenate([
                wqb[:, :, :MLA_NOPE].reshape(ql, Hm * MLA_NOPE),
                jnp.pad(wqb[:, :, MLA_NOPE:], ((0, 0), (0, 0), (0, LANE - MLA_ROPE))).reshape(ql, Hm * LANE),
            ], axis=1)
            q = _mm(qn, wqb, out_dtype=BF16, rope=(cos_mla, sin_mla, Hm * MLA_NOPE, MLA_ROPE // 4))
            pe_blk = (ql + kvl) // LANE
            k_pe = _head_rope(a, ones_g, cos_mla, sin_mla, col_block0=pe_blk, n_heads=1, norm=False,
                              half=MLA_ROPE // 4, out_dtype=BF16)
            wkvb = mla_wkv_b[j].reshape(kvl, Hm, MLA_NOPE + MLA_V)
            wkvb = jnp.concatenate([wkvb[:, :, :MLA_NOPE].reshape(kvl, Hm * MLA_NOPE),
                                    wkvb[:, :, MLA_NOPE:].reshape(kvl, Hm * MLA_V)], axis=1)
            ckv_s = jnp.concatenate([ckv[NP:].reshape(Bs, Ts, kvl), cache_mla_ckv[:, j]], axis=1)
            c_all = jnp.concatenate([ckv[:NP], ckv_s.reshape(Bs * (Ts + P), kvl)], axis=0)
            kv = _mm(c_all, wkvb, out_dtype=BF16)
            kpe_ctx = jnp.pad(cache_mla_kpe[:, j], ((0, 0), (0, 0), (0, LANE - MLA_ROPE))).astype(BF16)
            kpe_s = jnp.concatenate([k_pe[NP:].reshape(Bs, Ts, LANE), kpe_ctx], axis=1)
            kpe_all = jnp.concatenate([k_pe[:NP], kpe_s.reshape(Bs * (Ts + P), LANE)], axis=0)
            scale = (MLA_NOPE + MLA_ROPE) ** -0.5
            qp = [(q, 0), (q, Hm)]
            kp = [(kv, 0, "head"), (kpe_all, 0, "shared")]
            vp = (kv, Hm, "head")
            mix_p = _attention(qp, kp, vp, B=Bp, T=Tp, S=Tp, H=Hm, hb=4, q_row0=0, k_row0=0,
                               tq=256, tk=256, scale=scale)
            mix_s = _attention(qp, kp, vp, B=Bs, T=Ts, S=Ts + P, H=Hm, hb=4, q_row0=NP, k_row0=NP,
                               tq=2048, tk=512, scale=scale)
            w_o = mla_wo
            ckv_list.append(ckv[:NP].reshape(Bp, Tp, kvl))
            kpe_list.append(a[:NP, ql + kvl:ql + kvl + MLA_ROPE].reshape(Bp, Tp, MLA_ROPE))
        elif kind == 1:
            H = HG_HEADS
            p = jax.nn.softmax(hgrn_lb_logits.astype(F32), axis=1)
            cum = jnp.cumsum(p, axis=1)
            lb = (cum - cum[:, :1])[:, layer].reshape(2, 1, D)
            proj = _mm(h, hgrn_w_in, layer=j, out_dtype=F32)
            of_p, ob_p, st_p = _hgrn_scan(proj, lb, None, B=Bp, T=Tp, row0=0)
            s0t = jnp.swapaxes(state_hgrn[:, j].astype(F32), -1, -2)
            of_s, ob_s, _ = _hgrn_scan(proj, lb, s0t, B=Bs, T=Ts, row0=NP)
            mix_p = _hgrn_gate(of_p, ob_p, proj, hgrn_o_norm[j], row0=0)
            mix_s = _hgrn_gate(of_s, ob_s, proj, hgrn_o_norm[j], row0=NP)
            w_o = hgrn_wo
            hg_list.append(st_p)
        else:
            Hq, Hk = GQA_HEADS, GQA_KV_HEADS
            qkv = _mm(h, gqa_w_qkv, layer=j, out_dtype=F32)
            gains = jnp.concatenate([jnp.tile(gqa_q_norm[j][None], (Hq, 1)),
                                     jnp.tile(gqa_k_norm[j][None], (Hk, 1))], axis=0).reshape(Hq + Hk, 1, LANE)
            k_plain = _head_rope(qkv, gains[Hq:], cos_gqa, sin_gqa, col_block0=Hq, n_heads=Hk, norm=True,
                                 half=GQA_HD // 4, out_dtype=F32, n_rows=NP)
            qk = _head_rope(qkv, gains, cos_gqa, sin_gqa, col_block0=0, n_heads=Hq + Hk, norm=True,
                            half=GQA_HD // 4, out_dtype=BF16)
            kw = Hk * LANE
            k_new = qk[:, Hq * LANE:]
            v_new = qkv[:, (Hq + Hk) * LANE:].astype(BF16)
            k_s = jnp.concatenate([k_new[NP:].reshape(Bs, Ts, kw),
                                   cache_gqa_k[:, j].reshape(Bs, P, kw).astype(BF16)], axis=1)
            v_s = jnp.concatenate([v_new[NP:].reshape(Bs, Ts, kw),
                                   cache_gqa_v[:, j].reshape(Bs, P, kw).astype(BF16)], axis=1)
            k_all = jnp.concatenate([k_new[:NP], k_s.reshape(Bs * (Ts + P), kw)], axis=0)
            v_all = jnp.concatenate([v_new[:NP], v_s.reshape(Bs * (Ts + P), kw)], axis=0)
            scale = GQA_HD ** -0.5
            qp = [(qk, 0)]
            kp = [(k_all, 0, "group")]
            vp = (v_all, 0, "group")
            mix_p = _attention(qp, kp, vp, B=Bp, T=Tp, S=Tp, H=Hq, hb=Hq // Hk, q_row0=0, k_row0=0,
                               tq=256, tk=256, scale=scale)
            mix_s = _attention(qp, kp, vp, B=Bs, T=Ts, S=Ts + P, H=Hq, hb=Hq // Hk, q_row0=NP, k_row0=NP,
                               tq=2048, tk=512, scale=scale)
            w_o = gqa_wo
            gk_list.append(k_plain[:NP].reshape(Bp, Tp, Hk, GQA_HD))
            gv_list.append(qkv[:NP, (Hq + Hk) * LANE:].reshape(Bp, Tp, Hk, GQA_HD))

        y = _mm_resid(mix_p, mix_s, w_o, y, nw[1], g1, layer=j, **rows)

        fi = layer // 2
        if layer % 2 == 0:
            h2 = _adaln_in(y, nw[2], sh2, sc2, tm=tm_tok, **rows)
            u = _gate_up(h2, ffn_w1, ffn_w3, fi)
            y = _down_resid(u, _cast_bf16(ffn_w2, fi), y, nw[3], g2, **rows)
        else:
            h2, sel, gw = _adaln_in(y, nw[2], sh2, sc2, tm=tm_tok, router=moe_router[fi], **rows)
            tm_e = _tile(M, 1024)
            tile_e, tile_rows, tile_first, src_tok, dst_row, gate_rows = _dispatch_tables(
                sel[:, :2], gw[:, :2], tm_e)
            x_sorted = _gather_sorted(h2, tile_rows, src_tok, tm=tm_e, D=D)
            u = _moe_gate_up(x_sorted, tile_e, tile_rows, tile_first, moe_w1, moe_w3, fi, tm=tm_e)
            slots = _moe_down(u, tile_e, tile_rows, dst_row, gate_rows, _cast_bf16(moe_w2, fi),
                              n_tok=M, tm=tm_e)
            if layer == depth - 1:
                y_p = _moe_combine(slots, y, nw[3], g2, row0=0, n_rows=NP, **rows)
                y_s = _moe_combine(slots, y, nw[3], g2, row0=NP, n_rows=NS, **rows)
            else:
                y = _moe_combine(slots, y, nw[3], g2, **rows)

    if depth % 2:
        y_p, y_s = y[:NP], y[NP:]
    return (y_p.reshape(Bp, Tp, D), y_s.reshape(Bs, Ts, D),
            jnp.stack(ckv_list, axis=1), jnp.stack(kpe_list, axis=1), jnp.stack(hg_list, axis=1),
            jnp.stack(gk_list, axis=1), jnp.stack(gv_list, axis=1))
```
